```python
import jax, jax.numpy as jnp
from jax import lax
import numpy as np

D_MODEL = 2048
BATCH = 16
SEQ = 2048
DEPTH = 1
DEC_BATCH = 2
DEC_SEQ = 4096
PAST_LEN = 128

CONV_CH = D_MODEL // 2
CONV_K = 3
N_HEADS = D_MODEL // 128
QK_NOPE = 128
QK_ROPE = 64
QK_DIM = QK_NOPE + QK_ROPE
V_DIM = 128
Q_LORA = D_MODEL // 4
KV_LORA = D_MODEL // 4
ROPE_THETA = 10000.0
Q_BLOCK = 128
N_GROUPS = 8
EXP_PER_GROUP = 8
N_EXPERTS = N_GROUPS * EXP_PER_GROUP
TOP_K = 2
D_EXPERT = D_MODEL // 4
ROW_BLOCK = 128
EPS = 1e-6
IN_SIZES = (CONV_CH, CONV_CH, CONV_CH, Q_LORA, KV_LORA, QK_ROPE, D_MODEL, D_MODEL)
D_IN = 3 * CONV_CH + Q_LORA + KV_LORA + QK_ROPE + 2 * D_MODEL

kernel_name = 'hybrid_conv_mla_hiermoe_encoder'


def _rmsnorm(x, g):
    xf = x.astype(jnp.float32)
    y = xf * lax.rsqrt(jnp.mean(xf * xf, axis=-1, keepdims=True) + EPS) * g.astype(jnp.float32)
    return y.astype(x.dtype)


def _rope(x):
    s, r = x.shape[1], x.shape[-1]
    inv = ROPE_THETA ** (-jnp.arange(0, r, 2, dtype=jnp.float32) / r)
    ang = jnp.arange(s, dtype=jnp.float32)[:, None] * inv[None, :]
    cos = jnp.cos(ang)[None, :, None, :]
    sin = jnp.sin(ang)[None, :, None, :]
    xf = x.astype(jnp.float32)
    x1, x2 = xf[..., : r // 2], xf[..., r // 2:]
    return jnp.concatenate([x1 * cos - x2 * sin, x1 * sin + x2 * cos], axis=-1).astype(x.dtype)


def _short_conv(u, w):
    up = jnp.pad(u, ((0, 0), (1, 1), (0, 0)))
    return up[:, :-2] * w[0] + up[:, 1:-1] * w[1] + up[:, 2:] * w[2]


def _attention(q, k, v):
    b, h, s, dk = q.shape
    scale = dk ** -0.5
    qb = q.reshape(b, h, s // Q_BLOCK, Q_BLOCK, dk).transpose(2, 0, 1, 3, 4)

    def one_block(qblk):
        sc = jnp.einsum('bhqd,bhkd->bhqk', qblk, k, preferred_element_type=jnp.float32) * scale
        p = jax.nn.softmax(sc, axis=-1)
        return jnp.einsum('bhqk,bhkd->bhqd', p.astype(v.dtype), v)

    o = lax.map(one_block, qb)
    return o.transpose(1, 0, 3, 2, 4).reshape(b, s, h * v.shape[-1])


def _mixer(xn, w_in, conv_w, q_a_norm_g, w_uq, kv_a_norm_g, w_ukv, q_norm_g, k_norm_g,
           w_conv_out, w_attn_out, w_out):
    b, s, _ = xn.shape
    proj = xn @ w_in
    u, g_b, g_c, q_lat, kv_lat, k_pe, gate_c, gate_a = jnp.split(
        proj, list(np.cumsum(IN_SIZES)[:-1]), axis=-1)
    conv_out = (g_b * _short_conv(g_c * u, conv_w)) @ w_conv_out
    q = (_rmsnorm(q_lat, q_a_norm_g) @ w_uq).reshape(b, s, N_HEADS, QK_DIM)
    kv = (_rmsnorm(kv_lat, kv_a_norm_g) @ w_ukv).reshape(b, s, N_HEADS, QK_NOPE + V_DIM)
    k_nope, v = kv[..., :QK_NOPE], kv[..., QK_NOPE:]
    k_rot = jnp.broadcast_to(k_pe[:, :, None, :], (b, s, N_HEADS, QK_ROPE))
    k = jnp.concatenate([k_nope, k_rot], axis=-1)
    q = _rmsnorm(q, q_norm_g)
    k = _rmsnorm(k, k_norm_g)
    q = jnp.concatenate([q[..., :QK_NOPE], _rope(q[..., QK_NOPE:])], axis=-1)
    k = jnp.concatenate([k[..., :QK_NOPE], _rope(k[..., QK_NOPE:])], axis=-1)
    o = _attention(q.transpose(0, 2, 1, 3), k.transpose(0, 2, 1, 3), v.transpose(0, 2, 1, 3))
    attn_out = o @ w_attn_out
    merged = jax.nn.sigmoid(gate_c) * conv_out + jax.nn.sigmoid(gate_a) * attn_out
    return merged @ w_out


def _moe(xn, w_router_group, b_router_group, w_router_exp, b_router_exp, w_gate, w_up, w_down):
    b, s, d = xn.shape
    t = b * s
    xf = xn.reshape(t, d)
    g_prob = jax.nn.softmax((xf @ w_router_group).astype(jnp.float32) + b_router_group.astype(jnp.float32), axis=-1)
    g_sel = jnp.argmax(g_prob, axis=-1)
    g_p = jnp.take_along_axis(g_prob, g_sel[:, None], axis=-1)[:, 0]
    e_logits = ((xf @ w_router_exp).astype(jnp.float32) + b_router_exp.astype(jnp.float32)).reshape(t, N_GROUPS, EXP_PER_GROUP)
    e_in_group = jnp.take_along_axis(e_logits, g_sel[:, None, None], axis=1)[:, 0]
    top_p, top_i = lax.top_k(jax.nn.softmax(e_in_group, axis=-1), TOP_K)
    top_p = top_p / jnp.sum(top_p, axis=-1, keepdims=True)
    gate = g_p[:, None] * top_p
    expert = (g_sel[:, None] * EXP_PER_GROUP + top_i).astype(jnp.int32)
    n = t * TOP_K
    e_flat = expert.reshape(n)
    w_flat = gate.reshape(n)
    tok = jnp.arange(n, dtype=jnp.int32) // TOP_K
    order = jnp.argsort(e_flat)
    e_sorted, tok_sorted, w_sorted = e_flat[order], tok[order], w_flat[order]
    counts = jnp.bincount(e_flat, length=N_EXPERTS)
    padded = (counts + ROW_BLOCK - 1) // ROW_BLOCK * ROW_BLOCK
    raw_start = jnp.cumsum(counts) - counts
    pad_end = jnp.cumsum(padded)
    pad_start = pad_end - padded
    dest = pad_start[e_sorted] + (jnp.arange(n, dtype=jnp.int32) - raw_start[e_sorted])
    n_rows = n + N_EXPERTS * ROW_BLOCK
    n_blocks = n_rows // ROW_BLOCK
    row_tok = jnp.full((n_rows,), t, jnp.int32).at[dest].set(tok_sorted)
    row_w = jnp.zeros((n_rows,), xn.dtype).at[dest].set(w_sorted.astype(xn.dtype))
    rows = jnp.concatenate([xf, jnp.zeros((1, d), xf.dtype)], axis=0)[row_tok]
    blk_start = jnp.arange(n_blocks, dtype=pad_end.dtype) * ROW_BLOCK
    blk_exp = jnp.minimum(jnp.searchsorted(pad_end, blk_start, side='right'), N_EXPERTS - 1)

    def expert_block(args):
        xb, e = args
        h = jax.nn.silu(xb @ w_gate[e]) * (xb @ w_up[e])
        return h @ w_down[e]

    out = lax.map(expert_block, (rows.reshape(n_blocks, ROW_BLOCK, d), blk_exp)).reshape(n_rows, d)
    y = jax.ops.segment_sum(out * row_w[:, None], row_tok, num_segments=t + 1)[:t]
    return y.reshape(b, s, d)


def _trunk(x, ln1_g, w_in, conv_w, q_a_norm_g, w_uq, kv_a_norm_g, w_ukv, q_norm_g, k_norm_g,
           w_conv_out, w_attn_out, w_out, ln2_g, w_router_group, b_router_group,
           w_router_exp, b_router_exp, w_gate, w_up, w_down):
    for l in range(DEPTH):
        h = x + _mixer(_rmsnorm(x, ln1_g[l]), w_in[l], conv_w[l], q_a_norm_g[l], w_uq[l],
                       kv_a_norm_g[l], w_ukv[l], q_norm_g[l], k_norm_g[l],
                       w_conv_out[l], w_attn_out[l], w_out[l])
        x = h + _moe(_rmsnorm(h, ln2_g[l]), w_router_group[l], b_router_group[l],
                     w_router_exp[l], b_router_exp[l], w_gate[l], w_up[l], w_down[l])
    return x


def setup_inputs(seed: int = 0) -> dict:
    key = jax.random.key(seed)
    ks = jax.random.split(key, 24)

    def nrm(k, shape, scale):
        return jax.random.normal(k, shape, jnp.float32) * scale

    def gain(k, n):
        return 1.0 + 0.01 * jax.random.normal(k, (DEPTH, n), jnp.float32)

    L = DEPTH
    return {
        'x_prompt': nrm(ks[0], (BATCH, SEQ, D_MODEL), 1.0),
        'x_sample': nrm(ks[1], (DEC_BATCH, DEC_SEQ, D_MODEL), 1.0),
        'ln1_g': gain(ks[2], D_MODEL),
        'w_in': nrm(ks[3], (L, D_MODEL, D_IN), D_MODEL ** -0.5),
        'conv_w': nrm(ks[4], (L, CONV_K, CONV_CH), CONV_K ** -0.5),
        'q_a_norm_g': gain(ks[5], Q_LORA),
        'w_uq': nrm(ks[6], (L, Q_LORA, N_HEADS * QK_DIM), Q_LORA ** -0.5),
        'kv_a_norm_g': gain(ks[7], KV_LORA),
        'w_ukv': nrm(ks[8], (L, KV_LORA, N_HEADS * (QK_NOPE + V_DIM)), KV_LORA ** -0.5),
        'q_norm_g': gain(ks[9], QK_DIM),
        'k_norm_g': gain(ks[10], QK_DIM),
        'w_conv_out': nrm(ks[11], (L, CONV_CH, D_MODEL), CONV_CH ** -0.5),
        'w_attn_out': nrm(ks[12], (L, N_HEADS * V_DIM, D_MODEL), (N_HEADS * V_DIM) ** -0.5),
        'w_out': nrm(ks[13], (L, D_MODEL, D_MODEL), D_MODEL ** -0.5),
        'ln2_g': gain(ks[14], D_MODEL),
        'w_router_group': nrm(ks[15], (L, D_MODEL, N_GROUPS), D_MODEL ** -0.5),
        'b_router_group': nrm(ks[16], (L, N_GROUPS), 0.01),
        'w_router_exp': nrm(ks[17], (L, D_MODEL, N_EXPERTS), D_MODEL ** -0.5),
        'b_router_exp': nrm(ks[18], (L, N_EXPERTS), 0.01),
        'w_gate': nrm(ks[19], (L, N_EXPERTS, D_MODEL, D_EXPERT), D_MODEL ** -0.5),
        'w_up': nrm(ks[20], (L, N_EXPERTS, D_MODEL, D_EXPERT), D_MODEL ** -0.5),
        'w_down': nrm(ks[21], (L, N_EXPERTS, D_EXPERT, D_MODEL), D_EXPERT ** -0.5),
    }


def reference(x_prompt, x_sample, ln1_g, w_in, conv_w, q_a_norm_g, w_uq, kv_a_norm_g, w_ukv,
              q_norm_g, k_norm_g, w_conv_out, w_attn_out, w_out, ln2_g, w_router_group,
              b_router_group, w_router_exp, b_router_exp, w_gate, w_up, w_down):
    y_prompt = _trunk(x_prompt, ln1_g, w_in, conv_w, q_a_norm_g, w_uq, kv_a_norm_g, w_ukv,
                      q_norm_g, k_norm_g, w_conv_out, w_attn_out, w_out, ln2_g, w_router_group,
                      b_router_group, w_router_exp, b_router_exp, w_gate, w_up, w_down)
    y_sample = _trunk(x_sample, ln1_g, w_in, conv_w, q_a_norm_g, w_uq, kv_a_norm_g, w_ukv,
                      q_norm_g, k_norm_g, w_conv_out, w_attn_out, w_out, ln2_g, w_router_group,
                      b_router_group, w_router_exp, b_router_exp, w_gate, w_up, w_down)
    return (y_prompt, y_sample)
```

```python
import functools
import math

import jax
import jax.numpy as jnp
from jax import lax
from jax.experimental import pallas as pl
from jax.experimental.pallas import tpu as pltpu

F32 = jnp.float32
BF16 = jnp.bfloat16

D_MODEL = 2048
CONV_CH = 1024
N_HEADS = 16
QK_NOPE = 128
QK_ROPE = 64
HALF_ROPE = QK_ROPE // 2
QK_DIM = QK_NOPE + QK_ROPE
QK_PAD = 256
V_DIM = 128
Q_LORA = 512
KV_LORA = 512
ROPE_THETA = 10000.0
N_GROUPS = 8
EXP_PER_GROUP = 8
N_EXPERTS = 64
D_EXPERT = 512
EPS = 1e-6
PROJ_W = 8192
KPE_OFF = 4096
LANES = 128
EXP_LANE0 = N_GROUPS
ROW_BLOCK = 256
VMEM_LIMIT = 56 * 1024 * 1024

_NT = (((1,), (1,)), ((), ()))


def _cparams(sem):
    return pltpu.CompilerParams(dimension_semantics=sem, vmem_limit_bytes=VMEM_LIMIT)


def _inproj_body(x_ref, g_ref, w_ref, wpe_ref, out_ref, kpe_ref, xn_ref):
    @pl.when(pl.program_id(1) == 0)
    def _():
        x = x_ref[...]
        ms = jnp.mean(x * x, axis=-1, keepdims=True)
        xn = (x * lax.rsqrt(ms + EPS) * g_ref[...]).astype(BF16)
        xn_ref[...] = xn
        kpe_ref[...] = jnp.dot(xn, wpe_ref[...], preferred_element_type=F32)

    out_ref[...] = jnp.dot(xn_ref[...], w_ref[...], preferred_element_type=F32).astype(BF16)


def _in_proj(x2d, ln1_g, w_main, w_pe, tm, tn):
    t = x2d.shape[0]
    return pl.pallas_call(
        _inproj_body,
        grid=(t // tm, PROJ_W // tn),
        in_specs=[
            pl.BlockSpec((tm, D_MODEL), lambda i, j: (i, 0)),
            pl.BlockSpec((1, D_MODEL), lambda i, j: (0, 0)),
            pl.BlockSpec((D_MODEL, tn), lambda i, j: (0, j)),
            pl.BlockSpec((D_MODEL, LANES), lambda i, j: (0, 0)),
        ],
        out_specs=[
            pl.BlockSpec((tm, tn), lambda i, j: (i, j)),
            pl.BlockSpec((tm, LANES), lambda i, j: (i, 0)),
        ],
        out_shape=[
            jax.ShapeDtypeStruct((t, PROJ_W), BF16),
            jax.ShapeDtypeStruct((t, LANES), F32),
        ],
        scratch_shapes=[pltpu.VMEM((tm, D_MODEL), BF16)],
        compiler_params=_cparams(("arbitrary", "arbitrary")),
        name="in_proj",
    )(x2d, ln1_g, w_main, w_pe)


def _qkv_body(ql_ref, kvl_ref, kpe_ref, cosk_ref, sink_ref, cost_ref, sint_ref,
              gqa_ref, gkva_ref, gq_ref, gkn_ref, gkr_ref, wq_ref, wkn_ref, wv_ref,
              qt_ref, k_ref, vt_ref):
    def latent_norm(ref, g_ref):
        v = ref[...].astype(F32)
        ms = jnp.mean(v * v, axis=-1, keepdims=True)
        return (v * lax.rsqrt(ms + EPS) * g_ref[...]).astype(BF16)

    qn = latent_norm(ql_ref, gqa_ref)
    kvn = latent_norm(kvl_ref, gkva_ref)
    tm = qn.shape[0]

    kn = jnp.dot(kvn, wkn_ref[...], preferred_element_type=F32)
    kpe = kpe_ref[...]
    ss_pe = jnp.sum(kpe * kpe, axis=-1, keepdims=True)
    kr = kpe * gkr_ref[...]
    kr = kr * cosk_ref[...] + pltpu.roll(kr, 2 * HALF_ROPE, axis=1) * sink_ref[...]
    gkn = gkn_ref[...]
    for h in range(N_HEADS):
        kh = kn[:, h * QK_NOPE:(h + 1) * QK_NOPE]
        ss = jnp.sum(kh * kh, axis=-1, keepdims=True) + ss_pe
        r = lax.rsqrt(ss * (1.0 / QK_DIM) + EPS)
        k_ref[0, h, :, 0:QK_NOPE] = (kh * gkn * r).astype(BF16)
        k_ref[0, h, :, QK_NOPE:QK_PAD] = (kr * r).astype(BF16)

    vt = lax.dot_general(wv_ref[...], kvn, _NT, preferred_element_type=F32)
    for h in range(N_HEADS):
        vt_ref[0, h] = vt[h * V_DIM:(h + 1) * V_DIM, :].astype(BF16)

    cost = cost_ref[...]
    sint = sint_ref[...]
    gq = gq_ref[...]
    zeros = jnp.zeros((HALF_ROPE, tm), BF16)
    for h in range(N_HEADS):
        qt = lax.dot_general(wq_ref[h * QK_PAD:(h + 1) * QK_PAD, :], qn, _NT,
                             preferred_element_type=F32)
        ss = jnp.sum(qt * qt, axis=0, keepdims=True)
        r = lax.rsqrt(ss * (1.0 / QK_DIM) + EPS)
        qs = qt * gq * r
        x1 = qs[QK_NOPE:QK_NOPE + HALF_ROPE]
        x2 = qs[QK_NOPE + 2 * HALF_ROPE:QK_NOPE + 3 * HALF_ROPE]
        qt_ref[0, h, 0:QK_NOPE, :] = qs[0:QK_NOPE].astype(BF16)
        qt_ref[0, h, QK_NOPE:QK_NOPE + HALF_ROPE, :] = (x1 * cost - x2 * sint).astype(BF16)
        qt_ref[0, h, QK_NOPE + HALF_ROPE:QK_NOPE + 2 * HALF_ROPE, :] = zeros
        qt_ref[0, h, QK_NOPE + 2 * HALF_ROPE:QK_NOPE + 3 * HALF_ROPE, :] = (
            x1 * sint + x2 * cost).astype(BF16)
        qt_ref[0, h, QK_NOPE + 3 * HALF_ROPE:QK_PAD, :] = zeros


def _qkv(proj, kpe, tabs, wts, b, s, tm):
    ns = s // tm
    cos_k, sin_k, cos_t, sin_t = tabs
    gqa, gkva, gq, gkn, gkr, wq_t, wkn, wv_t = wts
    const = lambda shape: pl.BlockSpec(shape, lambda bi, i: (0,) * len(shape))
    return pl.pallas_call(
        _qkv_body,
        grid=(b, ns),
        in_specs=[
            pl.BlockSpec((tm, Q_LORA), lambda bi, i: (bi * ns + i, 3 * CONV_CH // Q_LORA)),
            pl.BlockSpec((tm, KV_LORA), lambda bi, i: (bi * ns + i, 3 * CONV_CH // KV_LORA + 1)),
            pl.BlockSpec((tm, LANES), lambda bi, i: (bi * ns + i, 0)),
            pl.BlockSpec((tm, LANES), lambda bi, i: (i, 0)),
            pl.BlockSpec((tm, LANES), lambda bi, i: (i, 0)),
            pl.BlockSpec((HALF_ROPE, tm), lambda bi, i: (0, i)),
            pl.BlockSpec((HALF_ROPE, tm), lambda bi, i: (0, i)),
            const((1, Q_LORA)), const((1, KV_LORA)), const((QK_PAD, 1)),
            const((1, LANES)), const((1, LANES)),
            const((N_HEADS * QK_PAD, Q_LORA)),
            const((KV_LORA, N_HEADS * QK_NOPE)),
            const((N_HEADS * V_DIM, KV_LORA)),
        ],
        out_specs=[
            pl.BlockSpec((1, N_HEADS, QK_PAD, tm), lambda bi, i: (bi, 0, 0, i)),
            pl.BlockSpec((1, N_HEADS, tm, QK_PAD), lambda bi, i: (bi, 0, i, 0)),
            pl.BlockSpec((1, N_HEADS, V_DIM, tm), lambda bi, i: (bi, 0, 0, i)),
        ],
        out_shape=[
            jax.ShapeDtypeStruct((b, N_HEADS, QK_PAD, s), BF16),
            jax.ShapeDtypeStruct((b, N_HEADS, s, QK_PAD), BF16),
            jax.ShapeDtypeStruct((b, N_HEADS, V_DIM, s), BF16),
        ],
        compiler_params=_cparams(("arbitrary", "arbitrary")),
        name="qkv",
    )(proj, proj, kpe, cos_k, sin_k, cos_t, sin_t, gqa, gkva, gq, gkn, gkr, wq_t, wkn, wv_t)


def _attn_body(qt_ref, k_ref, vt_ref, o_ref, s_ref, *, kc):
    s_len = k_ref.shape[2]
    qt = qt_ref[0, 0]
    m = None
    for c in range(s_len // kc):
        sc = jnp.dot(k_ref[0, 0, c * kc:(c + 1) * kc, :], qt, preferred_element_type=F32)
        s_ref[c * kc:(c + 1) * kc, :] = sc
        mc = jnp.max(sc, axis=0, keepdims=True)
        m = mc if m is None else jnp.maximum(m, mc)
    l = None
    acc = None
    for c in range(s_len // kc):
        p = jnp.exp2(s_ref[c * kc:(c + 1) * kc, :] - m)
        lc = jnp.sum(p, axis=0, keepdims=True)
        l = lc if l is None else l + lc
        pv = jnp.dot(vt_ref[0, 0, :, c * kc:(c + 1) * kc], p.astype(BF16),
                     preferred_element_type=F32)
        acc = pv if acc is None else acc + pv
    o_ref[0] = (acc * (1.0 / l)).T.astype(BF16)


def _attention(qt, k, vt, tq, kc):
    b, _, _, s = qt.shape
    return pl.pallas_call(
        functools.partial(_attn_body, kc=kc),
        grid=(b, N_HEADS, s // tq),
        in_specs=[
            pl.BlockSpec((1, 1, QK_PAD, tq), lambda bi, h, i: (bi, h, 0, i)),
            pl.BlockSpec((1, 1, s, QK_PAD), lambda bi, h, i: (bi, h, 0, 0)),
            pl.BlockSpec((1, 1, V_DIM, s), lambda bi, h, i: (bi, h, 0, 0)),
        ],
        out_specs=pl.BlockSpec((1, tq, V_DIM), lambda bi, h, i: (bi, i, h)),
        out_shape=jax.ShapeDtypeStruct((b, s, N_HEADS * V_DIM), BF16),
        scratch_shapes=[pltpu.VMEM((s, tq), F32)],
        compiler_params=_cparams(("arbitrary", "arbitrary", "arbitrary")),
        name="attention",
    )(qt, k, vt)


def _merge_body(u_ref, gb_ref, gc_ref, up_ref, gcp_ref, un_ref, gcn_ref, o_ref, gtc_ref, gta_ref,
                cw_ref, wc_ref, wa_ref, out_ref):
    i = pl.program_id(1)
    last = pl.num_programs(1) - 1
    halo = up_ref.shape[0]
    v = gc_ref[...].astype(F32) * u_ref[...].astype(F32)
    tm = v.shape[0]
    v_before = gcp_ref[halo - 1:halo, :].astype(F32) * up_ref[halo - 1:halo, :].astype(F32)
    v_after = gcn_ref[0:1, :].astype(F32) * un_ref[0:1, :].astype(F32)
    v_before = jnp.where(i == 0, 0.0, v_before)
    v_after = jnp.where(i == last, 0.0, v_after)
    row = lax.broadcasted_iota(jnp.int32, (tm, 1), 0)
    v_prev = jnp.where(row == 0, v_before, pltpu.roll(v, 1, axis=0))
    v_next = jnp.where(row == tm - 1, v_after, pltpu.roll(v, tm - 1, axis=0))
    cw = cw_ref[...]
    conv = v_prev * cw[0:1, :] + v * cw[1:2, :] + v_next * cw[2:3, :]
    z = (gb_ref[...].astype(F32) * conv).astype(BF16)
    conv_out = jnp.dot(z, wc_ref[...], preferred_element_type=F32)
    attn_out = jnp.dot(o_ref[...], wa_ref[...], preferred_element_type=F32)
    merged = (jax.nn.sigmoid(gtc_ref[...].astype(F32)) * conv_out
              + jax.nn.sigmoid(gta_ref[...].astype(F32)) * attn_out)
    out_ref[...] = merged.astype(BF16)


def _merge(proj, o2d, conv_w, wc, wa, b, s, tm, halo):
    ns = s // tm
    t = b * s
    hb = tm // halo
    nh = t // halo
    row = lambda bi, i: bi * ns + i
    prev = lambda bi, i: (jnp.maximum(row(bi, i) * hb - 1, 0))
    nxt = lambda bi, i: (jnp.minimum((row(bi, i) + 1) * hb, nh - 1))
    gate0 = KPE_OFF // D_MODEL
    return pl.pallas_call(
        _merge_body,
        grid=(b, ns),
        in_specs=[
            pl.BlockSpec((tm, CONV_CH), lambda bi, i: (row(bi, i), 0)),
            pl.BlockSpec((tm, CONV_CH), lambda bi, i: (row(bi, i), 1)),
            pl.BlockSpec((tm, CONV_CH), lambda bi, i: (row(bi, i), 2)),
            pl.BlockSpec((halo, CONV_CH), lambda bi, i: (prev(bi, i), 0)),
            pl.BlockSpec((halo, CONV_CH), lambda bi, i: (prev(bi, i), 2)),
            pl.BlockSpec((halo, CONV_CH), lambda bi, i: (nxt(bi, i), 0)),
            pl.BlockSpec((halo, CONV_CH), lambda bi, i: (nxt(bi, i), 2)),
            pl.BlockSpec((tm, D_MODEL), lambda bi, i: (row(bi, i), 0)),
            pl.BlockSpec((tm, D_MODEL), lambda bi, i: (row(bi, i), gate0)),
            pl.BlockSpec((tm, D_MODEL), lambda bi, i: (row(bi, i), gate0 + 1)),
            pl.BlockSpec((3, CONV_CH), lambda bi, i: (0, 0)),
            pl.BlockSpec((CONV_CH, D_MODEL), lambda bi, i: (0, 0)),
            pl.BlockSpec((D_MODEL, D_MODEL), lambda bi, i: (0, 0)),
        ],
        out_specs=pl.BlockSpec((tm, D_MODEL), lambda bi, i: (row(bi, i), 0)),
        out_shape=jax.ShapeDtypeStruct((t, D_MODEL), BF16),
        compiler_params=_cparams(("arbitrary", "arbitrary")),
        name="merge",
    )(proj, proj, proj, proj, proj, proj, proj, o2d, proj, proj, conv_w, wc, wa)


def _out_route_body(x_ref, mg_ref, wo_ref, g2_ref, wrh_ref, wrl_ref, br_ref,
                    h_ref, xn_ref, route_ref, routet_ref, cnt_ref, base_ref):
    step = pl.program_id(0)

    @pl.when(step == 0)
    def _():
        base_ref[...] = jnp.zeros_like(base_ref)

    h = x_ref[...] + jnp.dot(mg_ref[...], wo_ref[...], preferred_element_type=F32)
    h_ref[...] = h
    ms = jnp.mean(h * h, axis=-1, keepdims=True)
    xn = h * lax.rsqrt(ms + EPS) * g2_ref[...]
    xn_ref[...] = xn
    tm = xn.shape[0]

    hi = xn.astype(BF16)
    lo = (xn - hi.astype(F32)).astype(BF16)
    wrh = wrh_ref[...]
    logits = (jnp.dot(hi, wrh, preferred_element_type=F32)
              + jnp.dot(lo, wrh, preferred_element_type=F32)
              + jnp.dot(hi, wrl_ref[...], preferred_element_type=F32)) + br_ref[...]

    lane = lax.broadcasted_iota(jnp.int32, (tm, LANES), 1)
    neg = -jnp.inf
    lg = jnp.where(lane < N_GROUPS, logits, neg)
    gmax = jnp.max(lg, axis=-1, keepdims=True)
    g_p = 1.0 / jnp.sum(jnp.exp(lg - gmax), axis=-1, keepdims=True)
    g_sel = jnp.min(jnp.where(lg == gmax, lane, LANES), axis=-1, keepdims=True)
    lo_lane = EXP_LANE0 + g_sel * EXP_PER_GROUP
    le = jnp.where((lane >= lo_lane) & (lane < lo_lane + EXP_PER_GROUP), logits, neg)
    m1 = jnp.max(le, axis=-1, keepdims=True)
    i1 = jnp.min(jnp.where(le == m1, lane, LANES), axis=-1, keepdims=True)
    le2 = jnp.where(lane == i1, neg, le)
    m2 = jnp.max(le2, axis=-1, keepdims=True)
    i2 = jnp.min(jnp.where(le2 == m2, lane, LANES), axis=-1, keepdims=True)
    e2 = jnp.exp(m2 - m1)
    gate1 = g_p / (1.0 + e2)
    gate2 = g_p * e2 / (1.0 + e2)

    sel1 = lane == i1
    sel2 = lane == i2
    onehot = jnp.where(sel1 | sel2, 1.0, 0.0)
    r_i = lax.broadcasted_iota(jnp.int32, (tm, tm), 0)
    c_i = lax.broadcasted_iota(jnp.int32, (tm, tm), 1)
    lower = jnp.where(r_i > c_i, 1.0, 0.0).astype(BF16)
    before = jnp.dot(lower, onehot.astype(BF16), preferred_element_type=F32) + base_ref[...]
    rank1 = jnp.sum(jnp.where(sel1, before, 0.0), axis=-1, keepdims=True)
    rank2 = jnp.sum(jnp.where(sel2, before, 0.0), axis=-1, keepdims=True)
    base_ref[...] = base_ref[...] + jnp.sum(onehot, axis=0, keepdims=True)
    cnt_ref[...] = base_ref[...]

    e1f = (i1 - EXP_LANE0).astype(F32)
    e2f = (i2 - EXP_LANE0).astype(F32)
    cols = (e1f, e2f, gate1, gate2, rank1, rank2)
    route = jnp.zeros((tm, LANES), F32)
    for c, val in enumerate(cols):
        route = jnp.where(lane == c, val, route)
    route_ref[...] = route
    routet_ref[...] = route.T[0:8, :]


def _out_route(x2d, merged, wo, ln2_g, wr_hi, wr_lo, br, tm):
    t = x2d.shape[0]
    const = lambda shape: pl.BlockSpec(shape, lambda i: (0,) * len(shape))
    return pl.pallas_call(
        _out_route_body,
        grid=(t // tm,),
        in_specs=[
            pl.BlockSpec((tm, D_MODEL), lambda i: (i, 0)),
            pl.BlockSpec((tm, D_MODEL), lambda i: (i, 0)),
            const((D_MODEL, D_MODEL)), const((1, D_MODEL)),
            const((D_MODEL, LANES)), const((D_MODEL, LANES)), const((1, LANES)),
        ],
        out_specs=[
            pl.BlockSpec((tm, D_MODEL), lambda i: (i, 0)),
            pl.BlockSpec((tm, D_MODEL), lambda i: (i, 0)),
            pl.BlockSpec((tm, LANES), lambda i: (i, 0)),
            pl.BlockSpec((8, tm), lambda i: (0, i)),
            const((1, LANES)),
        ],
        out_shape=[
            jax.ShapeDtypeStruct((t, D_MODEL), F32),
            jax.ShapeDtypeStruct((t, D_MODEL), F32),
            jax.ShapeDtypeStruct((t, LANES), F32),
            jax.ShapeDtypeStruct((8, t), F32),
            jax.ShapeDtypeStruct((1, LANES), F32),
        ],
        scratch_shapes=[pltpu.VMEM((1, LANES), F32)],
        compiler_params=_cparams(("arbitrary",)),
        name="out_route",
    )(x2d, merged, wo, ln2_g, wr_hi, wr_lo, br)


def _plan_body(cnt_ref, routet_ref, pos_ref, start_ref, blk_exp_ref, blk_src_ref, nvalid_ref):
    n_blocks = blk_exp_ref.shape[0]

    def per_expert(e, acc):
        c = cnt_ref[EXP_LANE0 + e]
        nb = (c + ROW_BLOCK - 1) // ROW_BLOCK
        start_ref[e] = acc * ROW_BLOCK

        def fill(j, carry):
            blk_exp_ref[acc + j] = e
            blk_src_ref[acc + j] = acc + j
            return carry

        lax.fori_loop(0, nb, fill, 0)
        return acc + nb

    nvalid = lax.fori_loop(0, N_EXPERTS, per_expert, 0)
    nvalid_ref[0] = nvalid
    last_exp = blk_exp_ref[nvalid - 1]

    def tail(j, carry):
        blk_exp_ref[j] = last_exp
        blk_src_ref[j] = nvalid - 1
        return carry

    lax.fori_loop(nvalid, n_blocks, tail, 0)

    ef = routet_ref[0:2, :]
    off = jnp.zeros_like(ef)
    for e in range(N_EXPERTS):
        off = jnp.where(ef == float(e), start_ref[e].astype(F32), off)
    pos_ref[...] = (off + routet_ref[4:6, :]).astype(jnp.int32)


def _plan(counts_i32, routet, n_blocks):
    t = routet.shape[1]
    smem = lambda: pl.BlockSpec(memory_space=pltpu.SMEM)
    return pl.pallas_call(
        _plan_body,
        in_specs=[smem(), pl.BlockSpec(memory_space=pltpu.VMEM)],
        out_specs=[pl.BlockSpec(memory_space=pltpu.VMEM), smem(), smem(), smem(), smem()],
        out_shape=[
            jax.ShapeDtypeStruct((2, t), jnp.int32),
            jax.ShapeDtypeStruct((N_EXPERTS,), jnp.int32),
            jax.ShapeDtypeStruct((n_blocks,), jnp.int32),
            jax.ShapeDtypeStruct((n_blocks,), jnp.int32),
            jax.ShapeDtypeStruct((1,), jnp.int32),
        ],
        compiler_params=pltpu.CompilerParams(vmem_limit_bytes=VMEM_LIMIT),
        name="plan",
    )(counts_i32, routet)


def _row_copy(src_hbm, src_row, dst_hbm, dst_row, sem):
    return pltpu.make_async_copy(src_hbm.at[pl.ds(src_row, 1)], dst_hbm.at[pl.ds(dst_row, 1)], sem)


def _dispatch_body(pos_ref, cnt_ref, start_ref, xn_hbm, zero_hbm, xs_hbm, sem, *, tc):
    step = pl.program_id(0)

    @pl.when(step == 0)
    def _():
        def per_expert(e, carry):
            c = cnt_ref[EXP_LANE0 + e]
            end = (c + ROW_BLOCK - 1) // ROW_BLOCK * ROW_BLOCK
            base = start_ref[e]

            def fill(r, inner):
                _row_copy(zero_hbm, 0, xs_hbm, base + r, sem).start()
                return inner

            def fill_done(r, inner):
                _row_copy(zero_hbm, 0, xs_hbm, 0, sem).wait()
                return inner

            lax.fori_loop(c, end, fill, 0)
            lax.fori_loop(c, end, fill_done, 0)
            return carry

        lax.fori_loop(0, N_EXPERTS, per_expert, 0)

    def issue(t, carry):
        tok = step * tc + t
        _row_copy(xn_hbm, tok, xs_hbm, pos_ref[0, t], sem).start()
        _row_copy(xn_hbm, tok, xs_hbm, pos_ref[1, t], sem).start()
        return carry

    lax.fori_loop(0, tc, issue, 0)

    def drain(t, carry):
        _row_copy(xn_hbm, 0, xs_hbm, 0, sem).wait()
        _row_copy(xn_hbm, 0, xs_hbm, 0, sem).wait()
        return carry

    lax.fori_loop(0, tc, drain, 0)


def _dispatch(pos_t, counts_i32, starts, xn2, zero_rows, n_rows, tc):
    t = xn2.shape[0]
    return pl.pallas_call(
        functools.partial(_dispatch_body, tc=tc),
        grid=(t // tc,),
        in_specs=[
            pl.BlockSpec((2, tc), lambda i: (0, i), memory_space=pltpu.SMEM),
            pl.BlockSpec(memory_space=pltpu.SMEM),
            pl.BlockSpec(memory_space=pltpu.SMEM),
            pl.BlockSpec(memory_space=pl.ANY),
            pl.BlockSpec(memory_space=pl.ANY),
        ],
        out_specs=pl.BlockSpec(memory_space=pl.ANY),
        out_shape=jax.ShapeDtypeStruct((n_rows, D_MODEL), F32),
        scratch_shapes=[pltpu.SemaphoreType.DMA(())],
        compiler_params=pltpu.CompilerParams(dimension_semantics=("arbitrary",),
                                             has_side_effects=True),
        name="dispatch",
    )(pos_t, counts_i32, starts, xn2, zero_rows)


def _expert_body(blk_exp_ref, blk_src_ref, nvalid_ref, xs_ref, wg_ref, wu_ref, wd_ref, ys_ref):
    @pl.when(pl.program_id(0) < nvalid_ref[0])
    def _():
        xb = xs_ref[...].astype(BF16)
        g = jnp.dot(xb, wg_ref[0], preferred_element_type=F32)
        u = jnp.dot(xb, wu_ref[0], preferred_element_type=F32)
        hmid = (g * jax.nn.sigmoid(g) * u).astype(BF16)
        ys_ref[...] = jnp.dot(hmid, wd_ref[0], preferred_element_type=F32)


def _experts(blk_exp, blk_src, nvalid, xs, wg, wu, wd):
    n_rows = xs.shape[0]
    n_blocks = n_rows // ROW_BLOCK
    grid_spec = pltpu.PrefetchScalarGridSpec(
        num_scalar_prefetch=3,
        grid=(n_blocks,),
        in_specs=[
            pl.BlockSpec((ROW_BLOCK, D_MODEL), lambda i, be, bs, nv: (bs[i], 0)),
            pl.BlockSpec((1, D_MODEL, D_EXPERT), lambda i, be, bs, nv: (be[i], 0, 0)),
            pl.BlockSpec((1, D_MODEL, D_EXPERT), lambda i, be, bs, nv: (be[i], 0, 0)),
            pl.BlockSpec((1, D_EXPERT, D_MODEL), lambda i, be, bs, nv: (be[i], 0, 0)),
        ],
        out_specs=pl.BlockSpec((ROW_BLOCK, D_MODEL), lambda i, be, bs, nv: (bs[i], 0)),
    )
    return pl.pallas_call(
        _expert_body,
        grid_spec=grid_spec,
        out_shape=jax.ShapeDtypeStruct((n_rows, D_MODEL), F32),
        compiler_params=_cparams(("arbitrary",)),
        name="experts",
    )(blk_exp, blk_src, nvalid, xs, wg, wu, wd)


def _combine_body(pos_ref, ys_hbm, h_ref, route_ref, y_ref, buf_ref, sem):
    tm = h_ref.shape[0]

    def issue(t, carry):
        pltpu.make_async_copy(ys_hbm.at[pl.ds(pos_ref[0, t], 1)], buf_ref.at[0, pl.ds(t, 1)], sem).start()
        pltpu.make_async_copy(ys_hbm.at[pl.ds(pos_ref[1, t], 1)], buf_ref.at[1, pl.ds(t, 1)], sem).start()
        return carry

    lax.fori_loop(0, tm, issue, 0)

    def drain(t, carry):
        pltpu.make_async_copy(ys_hbm.at[pl.ds(0, 1)], buf_ref.at[0, pl.ds(t, 1)], sem).wait()
        pltpu.make_async_copy(ys_hbm.at[pl.ds(0, 1)], buf_ref.at[1, pl.ds(t, 1)], sem).wait()
        return carry

    lax.fori_loop(0, tm, drain, 0)
    route = route_ref[...]
    y_ref[...] = h_ref[...] + route[:, 2:3] * buf_ref[0] + route[:, 3:4] * buf_ref[1]


def _combine(pos_t, ys, h, route, tm):
    t = h.shape[0]
    return pl.pallas_call(
        _combine_body,
        grid=(t // tm,),
        in_specs=[
            pl.BlockSpec((2, tm), lambda i: (0, i), memory_space=pltpu.SMEM),
            pl.BlockSpec(memory_space=pl.ANY),
            pl.BlockSpec((tm, D_MODEL), lambda i: (i, 0)),
            pl.BlockSpec((tm, LANES), lambda i: (i, 0)),
        ],
        out_specs=pl.BlockSpec((tm, D_MODEL), lambda i: (i, 0)),
        out_shape=jax.ShapeDtypeStruct((t, D_MODEL), F32),
        scratch_shapes=[pltpu.VMEM((2, tm, D_MODEL), F32), pltpu.SemaphoreType.DMA(())],
        compiler_params=_cparams(("arbitrary",)),
        name="combine",
    )(pos_t, ys, h, route)


def _pad_rope(a, axis):
    x1, x2 = jnp.split(a, 2, axis=axis)
    z = jnp.zeros_like(x1)
    return jnp.concatenate([x1, z, x2, z], axis=axis)


def _prepare(ln1_g, w_in, conv_w, q_a_norm_g, w_uq, kv_a_norm_g, w_ukv, q_norm_g, k_norm_g,
             w_conv_out, w_attn_out, w_out, ln2_g, w_router_group, b_router_group,
             w_router_exp, b_router_exp, w_gate, w_up, w_down):
    w_in0 = w_in[0]
    w_main = jnp.concatenate([w_in0[:, :KPE_OFF], w_in0[:, KPE_OFF + QK_ROPE:]], axis=1).astype(BF16)
    w_pe = _pad_rope(w_in0[:, KPE_OFF:KPE_OFF + QK_ROPE], 1).astype(BF16)

    wq = w_uq[0].reshape(Q_LORA, N_HEADS, QK_DIM)
    wq = jnp.concatenate([wq[:, :, :QK_NOPE], _pad_rope(wq[:, :, QK_NOPE:], 2)], axis=2)
    wq_t = wq.reshape(Q_LORA, N_HEADS * QK_PAD).T.astype(BF16)
    wkv = w_ukv[0].reshape(KV_LORA, N_HEADS, QK_NOPE + V_DIM)
    wkn = wkv[:, :, :QK_NOPE].reshape(KV_LORA, N_HEADS * QK_NOPE).astype(BF16)
    wv_t = wkv[:, :, QK_NOPE:].reshape(KV_LORA, N_HEADS * V_DIM).T.astype(BF16)

    qg = q_norm_g[0]
    score_scale = QK_DIM ** -0.5 * math.log2(math.e)
    gq = (jnp.concatenate([qg[:QK_NOPE], _pad_rope(qg[QK_NOPE:], 0)]) * score_scale).reshape(QK_PAD, 1)
    kg = k_norm_g[0]
    gkn = kg[:QK_NOPE].reshape(1, LANES)
    gkr = _pad_rope(kg[QK_NOPE:], 0).reshape(1, LANES)

    wr = jnp.concatenate([w_router_group[0], w_router_exp[0],
                          jnp.zeros((D_MODEL, LANES - N_GROUPS - N_EXPERTS), F32)], axis=1)
    wr_hi = wr.astype(BF16)
    wr_lo = (wr - wr_hi.astype(F32)).astype(BF16)
    br = jnp.concatenate([b_router_group[0], b_router_exp[0],
                          jnp.zeros((LANES - N_GROUPS - N_EXPERTS,), F32)]).reshape(1, LANES)
    return dict(
        ln1_g=ln1_g, w_main=w_main, w_pe=w_pe, conv_w=conv_w[0],
        qkv=(q_a_norm_g, kv_a_norm_g, gq, gkn, gkr, wq_t, wkn, wv_t),
        wc=w_conv_out[0].astype(BF16), wa=w_attn_out[0].astype(BF16), wo=w_out[0].astype(BF16),
        ln2_g=ln2_g, wr_hi=wr_hi, wr_lo=wr_lo, br=br,
        wg=w_gate[0].astype(BF16), wu=w_up[0].astype(BF16), wd=w_down[0].astype(BF16),
    )


def _rope_tables(s):
    inv = ROPE_THETA ** (-jnp.arange(0, QK_ROPE, 2, dtype=F32) / QK_ROPE)
    ang = jnp.arange(s, dtype=F32)[:, None] * inv[None, :]
    cos, sin = jnp.cos(ang), jnp.sin(ang)
    z = jnp.zeros_like(cos)
    cos_k = jnp.concatenate([cos, z, cos, z], axis=1)
    sin_k = jnp.concatenate([-sin, z, sin, z], axis=1)
    return cos_k, sin_k, cos.T, sin.T


def _tiles(b, s):
    t = b * s
    pick = lambda n, pref: pref if n % pref == 0 else n
    return dict(
        in_tm=pick(t, 512), in_tn=1024,
        qkv_tm=pick(s, 256),
        attn_tq=pick(s, 512), attn_kc=pick(s, 512),
        merge_tm=pick(s, 256), halo=16,
        route_tm=pick(t, 512),
        disp_tc=pick(t, 2048),
        comb_tm=pick(t, 256),
    )


def _layer(x, p):
    b, s, _ = x.shape
    t = b * s
    tl = _tiles(b, s)
    x2d = x.reshape(t, D_MODEL)
    proj, kpe = _in_proj(x2d, p["ln1_g"], p["w_main"], p["w_pe"], tl["in_tm"], tl["in_tn"])
    qt, k, vt = _qkv(proj, kpe, _rope_tables(s), p["qkv"], b, s, tl["qkv_tm"])
    o = _attention(qt, k, vt, tl["attn_tq"], tl["attn_kc"])
    merged = _merge(proj, o.reshape(t, D_MODEL), p["conv_w"], p["wc"], p["wa"], b, s,
                    tl["merge_tm"], tl["halo"])
    h, xn2, route, route_t, counts = _out_route(x2d, merged, p["wo"], p["ln2_g"], p["wr_hi"],
                                                p["wr_lo"], p["br"], tl["route_tm"])
    n_blocks = 2 * t // ROW_BLOCK + N_EXPERTS
    counts_i32 = counts.reshape(LANES).astype(jnp.int32)
    pos_t, starts, blk_exp, blk_src, nvalid = _plan(counts_i32, route_t, n_blocks)
    xs = _dispatch(pos_t, counts_i32, starts, xn2, jnp.zeros((8, D_MODEL), F32),
                   n_blocks * ROW_BLOCK, tl["disp_tc"])
    ys = _experts(blk_exp, blk_src, nvalid, xs, p["wg"], p["wu"], p["wd"])
    y = _combine(pos_t, ys, h, route, tl["comb_tm"])
    return y.reshape(b, s, D_MODEL)


def kernel(x_prompt, x_sample, ln1_g, w_in, conv_w, q_a_norm_g, w_uq, kv_a_norm_g, w_ukv, q_norm_g,
           k_norm_g, w_conv_out, w_attn_out, w_out, ln2_g, w_router_group, b_router_group,
           w_router_exp, b_router_exp, w_gate, w_up, w_down):
    p = _prepare(ln1_g, w_in, conv_w, q_a_norm_g, w_uq, kv_a_norm_g, w_ukv, q_norm_g, k_norm_g,
                 w_conv_out, w_attn_out, w_out, ln2_g, w_router_group, b_router_group,
                 w_router_exp, b_router_exp, w_gate, w_up, w_down)
    return (_layer(x_prompt, p), _layer(x_sample, p))
```

```python
import functools
import math

import jax
import jax.numpy as jnp
from jax import lax
from jax.experimental import pallas as pl
from jax.experimental.pallas import tpu as pltpu

F32 = jnp.float32
BF16 = jnp.bfloat16

D_MODEL = 2048
CONV_CH = 1024
N_HEADS = 16
QK_NOPE = 128
QK_ROPE = 64
HALF_ROPE = QK_ROPE // 2
QK_DIM = QK_NOPE + QK_ROPE
QK_PAD = 256
V_DIM = 128
Q_LORA = 512
KV_LORA = 512
ROPE_THETA = 10000.0
N_GROUPS = 8
EXP_PER_GROUP = 8
N_EXPERTS = 64
D_EXPERT = 512
EPS = 1e-6
PROJ_W = 8192
KPE_OFF = 4096
LANES = 128
EXP_LANE0 = N_GROUPS
ROW_BLOCK = 256
VMEM_LIMIT = 56 * 1024 * 1024

_NT = (((1,), (1,)), ((), ()))


def _cparams(sem):
    return pltpu.CompilerParams(dimension_semantics=sem, vmem_limit_bytes=VMEM_LIMIT)


def _inproj_body(x_ref, g_ref, w_ref, wpe_ref, out_ref, kpe_ref, xn_ref):
    @pl.when(pl.program_id(1) == 0)
    def _():
        x = x_ref[...]
        ms = jnp.mean(x * x, axis=-1, keepdims=True)
        xn = (x * lax.rsqrt(ms + EPS) * g_ref[...]).astype(BF16)
        xn_ref[...] = xn
        kpe_ref[...] = jnp.dot(xn, wpe_ref[...], preferred_element_type=F32)

    out_ref[...] = jnp.dot(xn_ref[...], w_ref[...], preferred_element_type=F32).astype(BF16)


def _in_proj(x2d, ln1_g, w_main, w_pe, tm, tn):
    t = x2d.shape[0]
    return pl.pallas_call(
        _inproj_body,
        grid=(t // tm, PROJ_W // tn),
        in_specs=[
            pl.BlockSpec((tm, D_MODEL), lambda i, j: (i, 0)),
            pl.BlockSpec((1, D_MODEL), lambda i, j: (0, 0)),
            pl.BlockSpec((D_MODEL, tn), lambda i, j: (0, j)),
            pl.BlockSpec((D_MODEL, LANES), lambda i, j: (0, 0)),
        ],
        out_specs=[
            pl.BlockSpec((tm, tn), lambda i, j: (i, j)),
            pl.BlockSpec((tm, LANES), lambda i, j: (i, 0)),
        ],
        out_shape=[
            jax.ShapeDtypeStruct((t, PROJ_W), BF16),
            jax.ShapeDtypeStruct((t, LANES), F32),
        ],
        scratch_shapes=[pltpu.VMEM((tm, D_MODEL), BF16)],
        compiler_params=_cparams(("arbitrary", "arbitrary")),
        name="in_proj",
    )(x2d, ln1_g, w_main, w_pe)


def _qkv_body(ql_ref, kvl_ref, kpe_ref, cosk_ref, sink_ref, cost_ref, sint_ref,
              gqa_ref, gkva_ref, gq_ref, gkn_ref, gkr_ref, wq_ref, wkn_ref, wv_ref,
              qt_ref, k_ref, vt_ref):
    def latent_norm(ref, g_ref):
        v = ref[...].astype(F32)
        ms = jnp.mean(v * v, axis=-1, keepdims=True)
        return (v * lax.rsqrt(ms + EPS) * g_ref[...]).astype(BF16)

    qn = latent_norm(ql_ref, gqa_ref)
    kvn = latent_norm(kvl_ref, gkva_ref)
    tm = qn.shape[0]

    kn = jnp.dot(kvn, wkn_ref[...], preferred_element_type=F32)
    kpe = kpe_ref[...]
    ss_pe = jnp.sum(kpe * kpe, axis=-1, keepdims=True)
    kr = kpe * gkr_ref[...]
    kr = kr * cosk_ref[...] + pltpu.roll(kr, 2 * HALF_ROPE, axis=1) * sink_ref[...]
    gkn = gkn_ref[...]
    for h in range(N_HEADS):
        kh = kn[:, h * QK_NOPE:(h + 1) * QK_NOPE]
        ss = jnp.sum(kh * kh, axis=-1, keepdims=True) + ss_pe
        r = lax.rsqrt(ss * (1.0 / QK_DIM) + EPS)
        k_ref[0, h, :, 0:QK_NOPE] = (kh * gkn * r).astype(BF16)
        k_ref[0, h, :, QK_NOPE:QK_PAD] = (kr * r).astype(BF16)

    vt = lax.dot_general(wv_ref[...], kvn, _NT, preferred_element_type=F32)
    for h in range(N_HEADS):
        vt_ref[0, h] = vt[h * V_DIM:(h + 1) * V_DIM, :].astype(BF16)

    cost = cost_ref[...]
    sint = sint_ref[...]
    gq = gq_ref[...]
    zeros = jnp.zeros((HALF_ROPE, tm), BF16)
    for h in range(N_HEADS):
        qt = lax.dot_general(wq_ref[h * QK_PAD:(h + 1) * QK_PAD, :], qn, _NT,
                             preferred_element_type=F32)
        ss = jnp.sum(qt * qt, axis=0, keepdims=True)
        r = lax.rsqrt(ss * (1.0 / QK_DIM) + EPS)
        qs = qt * gq * r
        x1 = qs[QK_NOPE:QK_NOPE + HALF_ROPE]
        x2 = qs[QK_NOPE + 2 * HALF_ROPE:QK_NOPE + 3 * HALF_ROPE]
        qt_ref[0, h, 0:QK_NOPE, :] = qs[0:QK_NOPE].astype(BF16)
        qt_ref[0, h, QK_NOPE:QK_NOPE + HALF_ROPE, :] = (x1 * cost - x2 * sint).astype(BF16)
        qt_ref[0, h, QK_NOPE + HALF_ROPE:QK_NOPE + 2 * HALF_ROPE, :] = zeros
        qt_ref[0, h, QK_NOPE + 2 * HALF_ROPE:QK_NOPE + 3 * HALF_ROPE, :] = (
            x1 * sint + x2 * cost).astype(BF16)
        qt_ref[0, h, QK_NOPE + 3 * HALF_ROPE:QK_PAD, :] = zeros


def _qkv(proj, kpe, tabs, wts, b, s, tm):
    ns = s // tm
    cos_k, sin_k, cos_t, sin_t = tabs
    gqa, gkva, gq, gkn, gkr, wq_t, wkn, wv_t = wts
    const = lambda shape: pl.BlockSpec(shape, lambda bi, i: (0,) * len(shape))
    return pl.pallas_call(
        _qkv_body,
        grid=(b, ns),
        in_specs=[
            pl.BlockSpec((tm, Q_LORA), lambda bi, i: (bi * ns + i, 3 * CONV_CH // Q_LORA)),
            pl.BlockSpec((tm, KV_LORA), lambda bi, i: (bi * ns + i, 3 * CONV_CH // KV_LORA + 1)),
            pl.BlockSpec((tm, LANES), lambda bi, i: (bi * ns + i, 0)),
            pl.BlockSpec((tm, LANES), lambda bi, i: (i, 0)),
            pl.BlockSpec((tm, LANES), lambda bi, i: (i, 0)),
            pl.BlockSpec((HALF_ROPE, tm), lambda bi, i: (0, i)),
            pl.BlockSpec((HALF_ROPE, tm), lambda bi, i: (0, i)),
            const((1, Q_LORA)), const((1, KV_LORA)), const((QK_PAD, 1)),
            const((1, LANES)), const((1, LANES)),
            const((N_HEADS * QK_PAD, Q_LORA)),
            const((KV_LORA, N_HEADS * QK_NOPE)),
            const((N_HEADS * V_DIM, KV_LORA)),
        ],
        out_specs=[
            pl.BlockSpec((1, N_HEADS, QK_PAD, tm), lambda bi, i: (bi, 0, 0, i)),
            pl.BlockSpec((1, N_HEADS, tm, QK_PAD), lambda bi, i: (bi, 0, i, 0)),
            pl.BlockSpec((1, N_HEADS, V_DIM, tm), lambda bi, i: (bi, 0, 0, i)),
        ],
        out_shape=[
            jax.ShapeDtypeStruct((b, N_HEADS, QK_PAD, s), BF16),
            jax.ShapeDtypeStruct((b, N_HEADS, s, QK_PAD), BF16),
            jax.ShapeDtypeStruct((b, N_HEADS, V_DIM, s), BF16),
        ],
        compiler_params=_cparams(("arbitrary", "arbitrary")),
        name="qkv",
    )(proj, proj, kpe, cos_k, sin_k, cos_t, sin_t, gqa, gkva, gq, gkn, gkr, wq_t, wkn, wv_t)


def _attn_body(qt_ref, k_ref, vt_ref, o_ref, s_a, m_a, s_b, m_b, *, kc):
    n = pl.program_id(0)
    s_len = k_ref.shape[2]
    chunks = [(c * kc, (c + 1) * kc) for c in range(s_len // kc)]

    @pl.when(n == 0)
    def _():
        s_b[...] = jnp.zeros_like(s_b)
        m_b[...] = jnp.zeros_like(m_b)

    def step(s_w, m_w, s_r, m_r):
        m_prev = m_r[...]
        qt = qt_ref[0, 0]
        l = None
        acc = None
        m = None
        for lo, hi in chunks:
            p = jnp.exp2(s_r[lo:hi, :] - m_prev)
            lc = jnp.sum(p, axis=0, keepdims=True)
            l = lc if l is None else l + lc
            pv = jnp.dot(vt_ref[0, 0, :, lo:hi], p.astype(BF16), preferred_element_type=F32)
            acc = pv if acc is None else acc + pv

            sc = jnp.dot(k_ref[0, 0, lo:hi, :], qt, preferred_element_type=F32)
            s_w[lo:hi, :] = sc
            mc = jnp.max(sc, axis=0, keepdims=True)
            m = mc if m is None else jnp.maximum(m, mc)
        o_ref[0] = (acc * (1.0 / l)).T.astype(BF16)
        m_w[...] = m

    @pl.when(n % 2 == 0)
    def _():
        step(s_a, m_a, s_b, m_b)

    @pl.when(n % 2 == 1)
    def _():
        step(s_b, m_b, s_a, m_a)


def _attention(qt, k, vt, tq, kc):
    b, _, _, s = qt.shape
    nq = s // tq
    n_tiles = b * N_HEADS * nq

    def bhi(tile):
        return tile // (N_HEADS * nq), (tile // nq) % N_HEADS, tile % nq

    def score_tile(n):
        return bhi(jnp.minimum(n, n_tiles - 1))

    def value_tile(n):
        return bhi(jnp.maximum(n - 1, 0))

    def qt_map(n):
        bi, h, i = score_tile(n)
        return bi, h, 0, i

    def k_map(n):
        bi, h, _ = score_tile(n)
        return bi, h, 0, 0

    def vt_map(n):
        bi, h, _ = value_tile(n)
        return bi, h, 0, 0

    def o_map(n):
        bi, h, i = value_tile(n)
        return bi, i, h

    return pl.pallas_call(
        functools.partial(_attn_body, kc=kc),
        grid=(n_tiles + 1,),
        in_specs=[
            pl.BlockSpec((1, 1, QK_PAD, tq), qt_map),
            pl.BlockSpec((1, 1, s, QK_PAD), k_map),
            pl.BlockSpec((1, 1, V_DIM, s), vt_map),
        ],
        out_specs=pl.BlockSpec((1, tq, V_DIM), o_map),
        out_shape=jax.ShapeDtypeStruct((b, s, N_HEADS * V_DIM), BF16),
        scratch_shapes=[pltpu.VMEM((s, tq), F32), pltpu.VMEM((1, tq), F32),
                        pltpu.VMEM((s, tq), F32), pltpu.VMEM((1, tq), F32)],
        compiler_params=_cparams(("arbitrary",)),
        name="attention",
    )(qt, k, vt)


def _merge_body(u_ref, gb_ref, gc_ref, up_ref, gcp_ref, un_ref, gcn_ref, o_ref, gtc_ref, gta_ref,
                cw_ref, wc_ref, wa_ref, out_ref):
    i = pl.program_id(1)
    last = pl.num_programs(1) - 1
    halo = up_ref.shape[0]
    v = gc_ref[...].astype(F32) * u_ref[...].astype(F32)
    tm = v.shape[0]
    v_before = gcp_ref[halo - 1:halo, :].astype(F32) * up_ref[halo - 1:halo, :].astype(F32)
    v_after = gcn_ref[0:1, :].astype(F32) * un_ref[0:1, :].astype(F32)
    v_before = jnp.where(i == 0, 0.0, v_before)
    v_after = jnp.where(i == last, 0.0, v_after)
    row = lax.broadcasted_iota(jnp.int32, (tm, 1), 0)
    v_prev = jnp.where(row == 0, v_before, pltpu.roll(v, 1, axis=0))
    v_next = jnp.where(row == tm - 1, v_after, pltpu.roll(v, tm - 1, axis=0))
    cw = cw_ref[...]
    conv = v_prev * cw[0:1, :] + v * cw[1:2, :] + v_next * cw[2:3, :]
    z = (gb_ref[...].astype(F32) * conv).astype(BF16)
    conv_out = jnp.dot(z, wc_ref[...], preferred_element_type=F32)
    attn_out = jnp.dot(o_ref[...], wa_ref[...], preferred_element_type=F32)
    merged = (jax.nn.sigmoid(gtc_ref[...].astype(F32)) * conv_out
              + jax.nn.sigmoid(gta_ref[...].astype(F32)) * attn_out)
    out_ref[...] = merged.astype(BF16)


def _merge(proj, o2d, conv_w, wc, wa, b, s, tm, halo):
    ns = s // tm
    t = b * s
    hb = tm // halo
    nh = t // halo
    row = lambda bi, i: bi * ns + i
    prev = lambda bi, i: (jnp.maximum(row(bi, i) * hb - 1, 0))
    nxt = lambda bi, i: (jnp.minimum((row(bi, i) + 1) * hb, nh - 1))
    gate0 = KPE_OFF // D_MODEL
    return pl.pallas_call(
        _merge_body,
        grid=(b, ns),
        in_specs=[
            pl.BlockSpec((tm, CONV_CH), lambda bi, i: (row(bi, i), 0)),
            pl.BlockSpec((tm, CONV_CH), lambda bi, i: (row(bi, i), 1)),
            pl.BlockSpec((tm, CONV_CH), lambda bi, i: (row(bi, i), 2)),
            pl.BlockSpec((halo, CONV_CH), lambda bi, i: (prev(bi, i), 0)),
            pl.BlockSpec((halo, CONV_CH), lambda bi, i: (prev(bi, i), 2)),
            pl.BlockSpec((halo, CONV_CH), lambda bi, i: (nxt(bi, i), 0)),
            pl.BlockSpec((halo, CONV_CH), lambda bi, i: (nxt(bi, i), 2)),
            pl.BlockSpec((tm, D_MODEL), lambda bi, i: (row(bi, i), 0)),
            pl.BlockSpec((tm, D_MODEL), lambda bi, i: (row(bi, i), gate0)),
            pl.BlockSpec((tm, D_MODEL), lambda bi, i: (row(bi, i), gate0 + 1)),
            pl.BlockSpec((3, CONV_CH), lambda bi, i: (0, 0)),
            pl.BlockSpec((CONV_CH, D_MODEL), lambda bi, i: (0, 0)),
            pl.BlockSpec((D_MODEL, D_MODEL), lambda bi, i: (0, 0)),
        ],
        out_specs=pl.BlockSpec((tm, D_MODEL), lambda bi, i: (row(bi, i), 0)),
        out_shape=jax.ShapeDtypeStruct((t, D_MODEL), BF16),
        compiler_params=_cparams(("arbitrary", "arbitrary")),
        name="merge",
    )(proj, proj, proj, proj, proj, proj, proj, o2d, proj, proj, conv_w, wc, wa)


def _out_route_body(x_ref, mg_ref, wo_ref, g2_ref, wrh_ref, wrl_ref, br_ref,
                    h_ref, xn_ref, route_ref, routet_ref, cnt_ref, base_ref):
    step = pl.program_id(0)

    @pl.when(step == 0)
    def _():
        base_ref[...] = jnp.zeros_like(base_ref)

    h = x_ref[...] + jnp.dot(mg_ref[...], wo_ref[...], preferred_element_type=F32)
    h_ref[...] = h
    ms = jnp.mean(h * h, axis=-1, keepdims=True)
    xn = h * lax.rsqrt(ms + EPS) * g2_ref[...]
    xn_ref[...] = xn
    tm = xn.shape[0]

    hi = xn.astype(BF16)
    lo = (xn - hi.astype(F32)).astype(BF16)
    wrh = wrh_ref[...]
    logits = (jnp.dot(hi, wrh, preferred_element_type=F32)
              + jnp.dot(lo, wrh, preferred_element_type=F32)
              + jnp.dot(hi, wrl_ref[...], preferred_element_type=F32)) + br_ref[...]

    lane = lax.broadcasted_iota(jnp.int32, (tm, LANES), 1)
    neg = -jnp.inf
    lg = jnp.where(lane < N_GROUPS, logits, neg)
    gmax = jnp.max(lg, axis=-1, keepdims=True)
    g_p = 1.0 / jnp.sum(jnp.exp(lg - gmax), axis=-1, keepdims=True)
    g_sel = jnp.min(jnp.where(lg == gmax, lane, LANES), axis=-1, keepdims=True)
    lo_lane = EXP_LANE0 + g_sel * EXP_PER_GROUP
    le = jnp.where((lane >= lo_lane) & (lane < lo_lane + EXP_PER_GROUP), logits, neg)
    m1 = jnp.max(le, axis=-1, keepdims=True)
    i1 = jnp.min(jnp.where(le == m1, lane, LANES), axis=-1, keepdims=True)
    le2 = jnp.where(lane == i1, neg, le)
    m2 = jnp.max(le2, axis=-1, keepdims=True)
    i2 = jnp.min(jnp.where(le2 == m2, lane, LANES), axis=-1, keepdims=True)
    e2 = jnp.exp(m2 - m1)
    gate1 = g_p / (1.0 + e2)
    gate2 = g_p * e2 / (1.0 + e2)

    sel1 = lane == i1
    sel2 = lane == i2
    onehot = jnp.where(sel1 | sel2, 1.0, 0.0)
    r_i = lax.broadcasted_iota(jnp.int32, (tm, tm), 0)
    c_i = lax.broadcasted_iota(jnp.int32, (tm, tm), 1)
    lower = jnp.where(r_i > c_i, 1.0, 0.0).astype(BF16)
    before = jnp.dot(lower, onehot.astype(BF16), preferred_element_type=F32) + base_ref[...]
    rank1 = jnp.sum(jnp.where(sel1, before, 0.0), axis=-1, keepdims=True)
    rank2 = jnp.sum(jnp.where(sel2, before, 0.0), axis=-1, keepdims=True)
    base_ref[...] = base_ref[...] + jnp.sum(onehot, axis=0, keepdims=True)
    cnt_ref[...] = base_ref[...]

    e1f = (i1 - EXP_LANE0).astype(F32)
    e2f = (i2 - EXP_LANE0).astype(F32)
    cols = (e1f, e2f, gate1, gate2, rank1, rank2)
    route = jnp.zeros((tm, LANES), F32)
    for c, val in enumerate(cols):
        route = jnp.where(lane == c, val, route)
    route_ref[...] = route
    routet_ref[...] = route.T[0:8, :]


def _out_route(x2d, merged, wo, ln2_g, wr_hi, wr_lo, br, tm):
    t = x2d.shape[0]
    const = lambda shape: pl.BlockSpec(shape, lambda i: (0,) * len(shape))
    return pl.pallas_call(
        _out_route_body,
        grid=(t // tm,),
        in_specs=[
            pl.BlockSpec((tm, D_MODEL), lambda i: (i, 0)),
            pl.BlockSpec((tm, D_MODEL), lambda i: (i, 0)),
            const((D_MODEL, D_MODEL)), const((1, D_MODEL)),
            const((D_MODEL, LANES)), const((D_MODEL, LANES)), const((1, LANES)),
        ],
        out_specs=[
            pl.BlockSpec((tm, D_MODEL), lambda i: (i, 0)),
            pl.BlockSpec((tm, D_MODEL), lambda i: (i, 0)),
            pl.BlockSpec((tm, LANES), lambda i: (i, 0)),
            pl.BlockSpec((8, tm), lambda i: (0, i)),
            const((1, LANES)),
        ],
        out_shape=[
            jax.ShapeDtypeStruct((t, D_MODEL), F32),
            jax.ShapeDtypeStruct((t, D_MODEL), F32),
            jax.ShapeDtypeStruct((t, LANES), F32),
            jax.ShapeDtypeStruct((8, t), F32),
            jax.ShapeDtypeStruct((1, LANES), F32),
        ],
        scratch_shapes=[pltpu.VMEM((1, LANES), F32)],
        compiler_params=_cparams(("arbitrary",)),
        name="out_route",
    )(x2d, merged, wo, ln2_g, wr_hi, wr_lo, br)


def _plan_body(cnt_ref, routet_ref, pos_ref, start_ref, blk_exp_ref, blk_src_ref, nvalid_ref):
    n_blocks = blk_exp_ref.shape[0]

    def per_expert(e, acc):
        c = cnt_ref[EXP_LANE0 + e]
        nb = (c + ROW_BLOCK - 1) // ROW_BLOCK
        start_ref[e] = acc * ROW_BLOCK

        def fill(j, carry):
            blk_exp_ref[acc + j] = e
            blk_src_ref[acc + j] = acc + j
            return carry

        lax.fori_loop(0, nb, fill, 0)
        return acc + nb

    nvalid = lax.fori_loop(0, N_EXPERTS, per_expert, 0)
    nvalid_ref[0] = nvalid
    last_exp = blk_exp_ref[nvalid - 1]

    def tail(j, carry):
        blk_exp_ref[j] = last_exp
        blk_src_ref[j] = nvalid - 1
        return carry

    lax.fori_loop(nvalid, n_blocks, tail, 0)

    ef = routet_ref[0:2, :]
    off = jnp.zeros_like(ef)
    for e in range(N_EXPERTS):
        off = jnp.where(ef == float(e), start_ref[e].astype(F32), off)
    pos_ref[...] = (off + routet_ref[4:6, :]).astype(jnp.int32)


def _plan(counts_i32, routet, n_blocks):
    t = routet.shape[1]
    smem = lambda: pl.BlockSpec(memory_space=pltpu.SMEM)
    return pl.pallas_call(
        _plan_body,
        in_specs=[smem(), pl.BlockSpec(memory_space=pltpu.VMEM)],
        out_specs=[pl.BlockSpec(memory_space=pltpu.VMEM), smem(), smem(), smem(), smem()],
        out_shape=[
            jax.ShapeDtypeStruct((2, t), jnp.int32),
            jax.ShapeDtypeStruct((N_EXPERTS,), jnp.int32),
            jax.ShapeDtypeStruct((n_blocks,), jnp.int32),
            jax.ShapeDtypeStruct((n_blocks,), jnp.int32),
            jax.ShapeDtypeStruct((1,), jnp.int32),
        ],
        compiler_params=pltpu.CompilerParams(vmem_limit_bytes=VMEM_LIMIT),
        name="plan",
    )(counts_i32, routet)


def _dispatch_body(pos_ref, cnt_ref, start_ref, nvalid_ref, xn_ref, xs_hbm, zero_ref, sem):
    tm = xn_ref.shape[0]
    n_blocks = xs_hbm.shape[0] // ROW_BLOCK

    def put(src_ref, src_row, dst_row):
        return pltpu.make_async_copy(src_ref.at[pl.ds(src_row, 1)], xs_hbm.at[pl.ds(dst_row, 1)], sem)

    def put_block(blk):
        return pltpu.make_async_copy(zero_ref, xs_hbm.at[pl.ds(blk * ROW_BLOCK, ROW_BLOCK)], sem)

    @pl.when(pl.program_id(0) == 0)
    def _():
        zero_ref[...] = jnp.zeros_like(zero_ref)

        def tail(blk, carry):
            put_block(blk).start()
            return carry

        def tail_done(blk, carry):
            put_block(0).wait()
            return carry

        lax.fori_loop(nvalid_ref[0], n_blocks, tail, 0)
        lax.fori_loop(nvalid_ref[0], n_blocks, tail_done, 0)

        def per_expert(e, carry):
            c = cnt_ref[EXP_LANE0 + e]
            end = (c + ROW_BLOCK - 1) // ROW_BLOCK * ROW_BLOCK
            base = start_ref[e]

            def fill(r, inner):
                put(zero_ref, 0, base + r).start()
                return inner

            def fill_done(r, inner):
                put(zero_ref, 0, 0).wait()
                return inner

            lax.fori_loop(c, end, fill, 0)
            lax.fori_loop(c, end, fill_done, 0)
            return carry

        lax.fori_loop(0, N_EXPERTS, per_expert, 0)

    def issue(t, carry):
        put(xn_ref, t, pos_ref[0, t]).start()
        put(xn_ref, t, pos_ref[1, t]).start()
        return carry

    lax.fori_loop(0, tm, issue, 0, unroll=8)

    def drain(t, carry):
        put(xn_ref, 0, 0).wait()
        put(xn_ref, 0, 0).wait()
        return carry

    lax.fori_loop(0, tm, drain, 0, unroll=8)


def _dispatch(pos_t, counts_i32, starts, nvalid, xn2, n_rows, tm):
    t = xn2.shape[0]
    smem = lambda: pl.BlockSpec(memory_space=pltpu.SMEM)
    return pl.pallas_call(
        _dispatch_body,
        grid=(t // tm,),
        in_specs=[
            pl.BlockSpec((2, tm), lambda i: (0, i), memory_space=pltpu.SMEM),
            smem(), smem(), smem(),
            pl.BlockSpec((tm, D_MODEL), lambda i: (i, 0)),
        ],
        out_specs=pl.BlockSpec(memory_space=pl.ANY),
        out_shape=jax.ShapeDtypeStruct((n_rows, D_MODEL), F32),
        scratch_shapes=[pltpu.VMEM((ROW_BLOCK, D_MODEL), F32), pltpu.SemaphoreType.DMA(())],
        compiler_params=pltpu.CompilerParams(dimension_semantics=("arbitrary",),
                                             vmem_limit_bytes=VMEM_LIMIT, has_side_effects=True),
        name="dispatch",
    )(pos_t, counts_i32, starts, nvalid, xn2)


def _expert_body(blk_exp_ref, blk_src_ref, nvalid_ref, xs_ref, wg_ref, wu_ref, wd_ref, ys_ref):
    valid = pl.program_id(0) < nvalid_ref[0]

    @pl.when(valid)
    def _():
        xb = xs_ref[...].astype(BF16)
        g = jnp.dot(xb, wg_ref[0], preferred_element_type=F32)
        u = jnp.dot(xb, wu_ref[0], preferred_element_type=F32)
        hmid = (g * jax.nn.sigmoid(g) * u).astype(BF16)
        ys_ref[...] = jnp.dot(hmid, wd_ref[0], preferred_element_type=F32)

    @pl.when(jnp.logical_not(valid))
    def _():
        ys_ref[...] = jnp.zeros_like(ys_ref)


def _experts(blk_exp, blk_src, nvalid, xs, wg, wu, wd):
    n_rows = xs.shape[0]
    n_blocks = n_rows // ROW_BLOCK
    grid_spec = pltpu.PrefetchScalarGridSpec(
        num_scalar_prefetch=3,
        grid=(n_blocks,),
        in_specs=[
            pl.BlockSpec((ROW_BLOCK, D_MODEL), lambda i, be, bs, nv: (bs[i], 0)),
            pl.BlockSpec((1, D_MODEL, D_EXPERT), lambda i, be, bs, nv: (be[i], 0, 0)),
            pl.BlockSpec((1, D_MODEL, D_EXPERT), lambda i, be, bs, nv: (be[i], 0, 0)),
            pl.BlockSpec((1, D_EXPERT, D_MODEL), lambda i, be, bs, nv: (be[i], 0, 0)),
        ],
        out_specs=pl.BlockSpec((ROW_BLOCK, D_MODEL), lambda i, be, bs, nv: (i, 0)),
    )
    return pl.pallas_call(
        _expert_body,
        grid_spec=grid_spec,
        out_shape=jax.ShapeDtypeStruct((n_rows, D_MODEL), F32),
        compiler_params=_cparams(("arbitrary",)),
        name="experts",
    )(blk_exp, blk_src, nvalid, xs, wg, wu, wd)


def _combine_body(pos_ref, pos_next_ref, ys_hbm, h_ref, route_ref, y_ref, buf_ref, sems):
    n = pl.program_id(0)
    tm = h_ref.shape[0]
    slot = n % 2

    def row_copy(p_ref, k, t, to_slot):
        return pltpu.make_async_copy(ys_hbm.at[pl.ds(p_ref[k, t], 1)],
                                     buf_ref.at[to_slot, k, pl.ds(t, 1)], sems.at[to_slot])

    def gather(p_ref, to_slot):
        def issue(t, carry):
            row_copy(p_ref, 0, t, to_slot).start()
            row_copy(p_ref, 1, t, to_slot).start()
            return carry

        lax.fori_loop(0, tm, issue, 0, unroll=8)

    @pl.when(n == 0)
    def _():
        gather(pos_ref, 0)

    @pl.when(n + 1 < pl.num_programs(0))
    def _():
        gather(pos_next_ref, 1 - slot)

    def drain(t, carry):
        row_copy(pos_ref, 0, t, slot).wait()
        row_copy(pos_ref, 1, t, slot).wait()
        return carry

    lax.fori_loop(0, tm, drain, 0, unroll=8)
    route = route_ref[...]
    y_ref[...] = h_ref[...] + route[:, 2:3] * buf_ref[slot, 0] + route[:, 3:4] * buf_ref[slot, 1]


def _combine(pos_t, ys, h, route, tm):
    t = h.shape[0]
    nt = t // tm
    return pl.pallas_call(
        _combine_body,
        grid=(nt,),
        in_specs=[
            pl.BlockSpec((2, tm), lambda i: (0, i), memory_space=pltpu.SMEM),
            pl.BlockSpec((2, tm), lambda i: (0, jnp.minimum(i + 1, nt - 1)), memory_space=pltpu.SMEM),
            pl.BlockSpec(memory_space=pl.ANY),
            pl.BlockSpec((tm, D_MODEL), lambda i: (i, 0)),
            pl.BlockSpec((tm, LANES), lambda i: (i, 0)),
        ],
        out_specs=pl.BlockSpec((tm, D_MODEL), lambda i: (i, 0)),
        out_shape=jax.ShapeDtypeStruct((t, D_MODEL), F32),
        scratch_shapes=[pltpu.VMEM((2, 2, tm, D_MODEL), F32), pltpu.SemaphoreType.DMA((2,))],
        compiler_params=_cparams(("arbitrary",)),
        name="combine",
    )(pos_t, pos_t, ys, h, route)


def _pad_rope(a, axis):
    x1, x2 = jnp.split(a, 2, axis=axis)
    z = jnp.zeros_like(x1)
    return jnp.concatenate([x1, z, x2, z], axis=axis)


def _prepare(ln1_g, w_in, conv_w, q_a_norm_g, w_uq, kv_a_norm_g, w_ukv, q_norm_g, k_norm_g,
             w_conv_out, w_attn_out, w_out, ln2_g, w_router_group, b_router_group,
             w_router_exp, b_router_exp, w_gate, w_up, w_down):
    w_in0 = w_in[0]
    w_main = jnp.concatenate([w_in0[:, :KPE_OFF], w_in0[:, KPE_OFF + QK_ROPE:]], axis=1).astype(BF16)
    w_pe = _pad_rope(w_in0[:, KPE_OFF:KPE_OFF + QK_ROPE], 1).astype(BF16)

    wq = w_uq[0].reshape(Q_LORA, N_HEADS, QK_DIM)
    wq = jnp.concatenate([wq[:, :, :QK_NOPE], _pad_rope(wq[:, :, QK_NOPE:], 2)], axis=2)
    wq_t = wq.reshape(Q_LORA, N_HEADS * QK_PAD).T.astype(BF16)
    wkv = w_ukv[0].reshape(KV_LORA, N_HEADS, QK_NOPE + V_DIM)
    wkn = wkv[:, :, :QK_NOPE].reshape(KV_LORA, N_HEADS * QK_NOPE).astype(BF16)
    wv_t = wkv[:, :, QK_NOPE:].reshape(KV_LORA, N_HEADS * V_DIM).T.astype(BF16)

    qg = q_norm_g[0]
    score_scale = QK_DIM ** -0.5 * math.log2(math.e)
    gq = (jnp.concatenate([qg[:QK_NOPE], _pad_rope(qg[QK_NOPE:], 0)]) * score_scale).reshape(QK_PAD, 1)
    kg = k_norm_g[0]
    gkn = kg[:QK_NOPE].reshape(1, LANES)
    gkr = _pad_rope(kg[QK_NOPE:], 0).reshape(1, LANES)

    wr = jnp.concatenate([w_router_group[0], w_router_exp[0],
                          jnp.zeros((D_MODEL, LANES - N_GROUPS - N_EXPERTS), F32)], axis=1)
    wr_hi = wr.astype(BF16)
    wr_lo = (wr - wr_hi.astype(F32)).astype(BF16)
    br = jnp.concatenate([b_router_group[0], b_router_exp[0],
                          jnp.zeros((LANES - N_GROUPS - N_EXPERTS,), F32)]).reshape(1, LANES)
    return dict(
        ln1_g=ln1_g, w_main=w_main, w_pe=w_pe, conv_w=conv_w[0],
        qkv=(q_a_norm_g, kv_a_norm_g, gq, gkn, gkr, wq_t, wkn, wv_t),
        wc=w_conv_out[0].astype(BF16), wa=w_attn_out[0].astype(BF16), wo=w_out[0].astype(BF16),
        ln2_g=ln2_g, wr_hi=wr_hi, wr_lo=wr_lo, br=br,
        wg=w_gate[0].astype(BF16), wu=w_up[0].astype(BF16), wd=w_down[0].astype(BF16),
    )


def _rope_tables(s):
    inv = ROPE_THETA ** (-jnp.arange(0, QK_ROPE, 2, dtype=F32) / QK_ROPE)
    ang = jnp.arange(s, dtype=F32)[:, None] * inv[None, :]
    cos, sin = jnp.cos(ang), jnp.sin(ang)
    z = jnp.zeros_like(cos)
    cos_k = jnp.concatenate([cos, z, cos, z], axis=1)
    sin_k = jnp.concatenate([-sin, z, sin, z], axis=1)
    return cos_k, sin_k, cos.T, sin.T


def _tiles(b, s):
    t = b * s
    pick = lambda n, pref: pref if n % pref == 0 else n
    return dict(
        in_tm=pick(t, 512), in_tn=1024,
        qkv_tm=pick(s, 256),
        attn_tq=pick(s, 512), attn_kc=pick(s, 256),
        merge_tm=pick(s, 256), halo=16,
        route_tm=pick(t, 512),
        disp_tm=pick(t, 512),
        comb_tm=pick(t, 256),
    )


def _layer(x, p):
    b, s, _ = x.shape
    t = b * s
    tl = _tiles(b, s)
    x2d = x.reshape(t, D_MODEL)
    proj, kpe = _in_proj(x2d, p["ln1_g"], p["w_main"], p["w_pe"], tl["in_tm"], tl["in_tn"])
    qt, k, vt = _qkv(proj, kpe, _rope_tables(s), p["qkv"], b, s, tl["qkv_tm"])
    o = _attention(qt, k, vt, tl["attn_tq"], tl["attn_kc"])
    merged = _merge(proj, o.reshape(t, D_MODEL), p["conv_w"], p["wc"], p["wa"], b, s,
                    tl["merge_tm"], tl["halo"])
    h, xn2, route, route_t, counts = _out_route(x2d, merged, p["wo"], p["ln2_g"], p["wr_hi"],
                                                p["wr_lo"], p["br"], tl["route_tm"])
    n_blocks = 2 * t // ROW_BLOCK + N_EXPERTS
    counts_i32 = counts.reshape(LANES).astype(jnp.int32)
    pos_t, starts, blk_exp, blk_src, nvalid = _plan(counts_i32, route_t, n_blocks)
    xs = _dispatch(pos_t, counts_i32, starts, nvalid, xn2, n_blocks * ROW_BLOCK, tl["disp_tm"])
    ys = _experts(blk_exp, blk_src, nvalid, xs, p["wg"], p["wu"], p["wd"])
    y = _combine(pos_t, ys, h, route, tl["comb_tm"])
    return y.reshape(b, s, D_MODEL)


def kernel(x_prompt, x_sample, ln1_g, w_in, conv_w, q_a_norm_g, w_uq, kv_a_norm_g, w_ukv, q_norm_g,
           k_norm_g, w_conv_out, w_attn_out, w_out, ln2_g, w_router_group, b_router_group,
           w_router_exp, b_router_exp, w_gate, w_up, w_down):
    p = _prepare(ln1_g, w_in, conv_w, q_a_norm_g, w_uq, kv_a_norm_g, w_ukv, q_norm_g, k_norm_g,
                 w_conv_out, w_attn_out, w_out, ln2_g, w_router_group, b_router_group,
                 w_router_exp, b_router_exp, w_gate, w_up, w_down)
    return (_layer(x_prompt, p), _layer(x_sample, p))
```

```python
import functools
import math

import jax
import jax.numpy as jnp
from jax import lax
from jax.experimental import pallas as pl
from jax.experimental.pallas import tpu as pltpu

F32 = jnp.float32
BF16 = jnp.bfloat16

D_MODEL = 2048
CONV_CH = 1024
N_HEADS = 16
QK_NOPE = 128
QK_ROPE = 64
HALF_ROPE = QK_ROPE // 2
QK_DIM = QK_NOPE + QK_ROPE
QK_PAD = 256
V_DIM = 128
Q_LORA = 512
KV_LORA = 512
ROPE_THETA = 10000.0
N_GROUPS = 8
EXP_PER_GROUP = 8
N_EXPERTS = 64
D_EXPERT = 512
EPS = 1e-6
PROJ_W = 8192
KPE_OFF = 4096
LANES = 128
EXP_LANE0 = N_GROUPS
ROW_BLOCK = 256
VMEM_LIMIT = 56 * 1024 * 1024

_NT = (((1,), (1,)), ((), ()))


def _cparams(sem):
    return pltpu.CompilerParams(dimension_semantics=sem, vmem_limit_bytes=VMEM_LIMIT)


def _inproj_body(x_ref, g_ref, w_ref, wpe_ref, out_ref, kpe_ref, xn_ref):
    @pl.when(pl.program_id(1) == 0)
    def _():
        x = x_ref[...]
        ms = jnp.mean(x * x, axis=-1, keepdims=True)
        xn = (x * lax.rsqrt(ms + EPS) * g_ref[...]).astype(BF16)
        xn_ref[...] = xn
        kpe_ref[...] = jnp.dot(xn, wpe_ref[...], preferred_element_type=F32)

    out_ref[...] = jnp.dot(xn_ref[...], w_ref[...], preferred_element_type=F32).astype(BF16)


def _in_proj(x2d, ln1_g, w_main, w_pe, tm, tn):
    t = x2d.shape[0]
    return pl.pallas_call(
        _inproj_body,
        grid=(t // tm, PROJ_W // tn),
        in_specs=[
            pl.BlockSpec((tm, D_MODEL), lambda i, j: (i, 0)),
            pl.BlockSpec((1, D_MODEL), lambda i, j: (0, 0)),
            pl.BlockSpec((D_MODEL, tn), lambda i, j: (0, j)),
            pl.BlockSpec((D_MODEL, LANES), lambda i, j: (0, 0)),
        ],
        out_specs=[
            pl.BlockSpec((tm, tn), lambda i, j: (i, j)),
            pl.BlockSpec((tm, LANES), lambda i, j: (i, 0)),
        ],
        out_shape=[
            jax.ShapeDtypeStruct((t, PROJ_W), BF16),
            jax.ShapeDtypeStruct((t, LANES), F32),
        ],
        scratch_shapes=[pltpu.VMEM((tm, D_MODEL), BF16)],
        compiler_params=_cparams(("arbitrary", "arbitrary")),
        name="in_proj",
    )(x2d, ln1_g, w_main, w_pe)


def _qkv_body(ql_ref, kvl_ref, kpe_ref, cosk_ref, sink_ref, cost_ref, sint_ref,
              gqa_ref, gkva_ref, gq_ref, gkn_ref, gkr_ref, wq_ref, wkn_ref, wv_ref,
              qt_ref, k_ref, vt_ref):
    def latent_norm(ref, g_ref):
        v = ref[...].astype(F32)
        ms = jnp.mean(v * v, axis=-1, keepdims=True)
        return (v * lax.rsqrt(ms + EPS) * g_ref[...]).astype(BF16)

    qn = latent_norm(ql_ref, gqa_ref)
    kvn = latent_norm(kvl_ref, gkva_ref)
    tm = qn.shape[0]

    kn = jnp.dot(kvn, wkn_ref[...], preferred_element_type=F32)
    kpe = kpe_ref[...]
    ss_pe = jnp.sum(kpe * kpe, axis=-1, keepdims=True)
    kr = kpe * gkr_ref[...]
    kr = kr * cosk_ref[...] + pltpu.roll(kr, 2 * HALF_ROPE, axis=1) * sink_ref[...]
    gkn = gkn_ref[...]
    for h in range(N_HEADS):
        kh = kn[:, h * QK_NOPE:(h + 1) * QK_NOPE]
        ss = jnp.sum(kh * kh, axis=-1, keepdims=True) + ss_pe
        r = lax.rsqrt(ss * (1.0 / QK_DIM) + EPS)
        k_ref[0, h, :, 0:QK_NOPE] = (kh * gkn * r).astype(BF16)
        k_ref[0, h, :, QK_NOPE:QK_PAD] = (kr * r).astype(BF16)

    vt = lax.dot_general(wv_ref[...], kvn, _NT, preferred_element_type=F32)
    for h in range(N_HEADS):
        vt_ref[0, h] = vt[h * V_DIM:(h + 1) * V_DIM, :].astype(BF16)

    cost = cost_ref[...]
    sint = sint_ref[...]
    gq = gq_ref[...]
    zeros = jnp.zeros((HALF_ROPE, tm), BF16)
    for h in range(N_HEADS):
        qt = lax.dot_general(wq_ref[h * QK_PAD:(h + 1) * QK_PAD, :], qn, _NT,
                             preferred_element_type=F32)
        ss = jnp.sum(qt * qt, axis=0, keepdims=True)
        r = lax.rsqrt(ss * (1.0 / QK_DIM) + EPS)
        qs = qt * gq * r
        x1 = qs[QK_NOPE:QK_NOPE + HALF_ROPE]
        x2 = qs[QK_NOPE + 2 * HALF_ROPE:QK_NOPE + 3 * HALF_ROPE]
        qt_ref[0, h, 0:QK_NOPE, :] = qs[0:QK_NOPE].astype(BF16)
        qt_ref[0, h, QK_NOPE:QK_NOPE + HALF_ROPE, :] = (x1 * cost - x2 * sint).astype(BF16)
        qt_ref[0, h, QK_NOPE + HALF_ROPE:QK_NOPE + 2 * HALF_ROPE, :] = zeros
        qt_ref[0, h, QK_NOPE + 2 * HALF_ROPE:QK_NOPE + 3 * HALF_ROPE, :] = (
            x1 * sint + x2 * cost).astype(BF16)
        qt_ref[0, h, QK_NOPE + 3 * HALF_ROPE:QK_PAD, :] = zeros


def _qkv(proj, kpe, tabs, wts, b, s, tm):
    ns = s // tm
    cos_k, sin_k, cos_t, sin_t = tabs
    gqa, gkva, gq, gkn, gkr, wq_t, wkn, wv_t = wts
    const = lambda shape: pl.BlockSpec(shape, lambda bi, i: (0,) * len(shape))
    return pl.pallas_call(
        _qkv_body,
        grid=(b, ns),
        in_specs=[
            pl.BlockSpec((tm, Q_LORA), lambda bi, i: (bi * ns + i, 3 * CONV_CH // Q_LORA)),
            pl.BlockSpec((tm, KV_LORA), lambda bi, i: (bi * ns + i, 3 * CONV_CH // KV_LORA + 1)),
            pl.BlockSpec((tm, LANES), lambda bi, i: (bi * ns + i, 0)),
            pl.BlockSpec((tm, LANES), lambda bi, i: (i, 0)),
            pl.BlockSpec((tm, LANES), lambda bi, i: (i, 0)),
            pl.BlockSpec((HALF_ROPE, tm), lambda bi, i: (0, i)),
            pl.BlockSpec((HALF_ROPE, tm), lambda bi, i: (0, i)),
            const((1, Q_LORA)), const((1, KV_LORA)), const((QK_PAD, 1)),
            const((1, LANES)), const((1, LANES)),
            const((N_HEADS * QK_PAD, Q_LORA)),
            const((KV_LORA, N_HEADS * QK_NOPE)),
            const((N_HEADS * V_DIM, KV_LORA)),
        ],
        out_specs=[
            pl.BlockSpec((1, N_HEADS, QK_PAD, tm), lambda bi, i: (bi, 0, 0, i)),
            pl.BlockSpec((1, N_HEADS, tm, QK_PAD), lambda bi, i: (bi, 0, i, 0)),
            pl.BlockSpec((1, N_HEADS, V_DIM, tm), lambda bi, i: (bi, 0, 0, i)),
        ],
        out_shape=[
            jax.ShapeDtypeStruct((b, N_HEADS, QK_PAD, s), BF16),
            jax.ShapeDtypeStruct((b, N_HEADS, s, QK_PAD), BF16),
            jax.ShapeDtypeStruct((b, N_HEADS, V_DIM, s), BF16),
        ],
        compiler_params=_cparams(("arbitrary", "arbitrary")),
        name="qkv",
    )(proj, proj, kpe, cos_k, sin_k, cos_t, sin_t, gqa, gkva, gq, gkn, gkr, wq_t, wkn, wv_t)


def _attn_body(qt_ref, k_ref, vt_ref, o_ref, s_a, m_a, s_b, m_b, *, kc):
    n = pl.program_id(0)
    s_len = k_ref.shape[2]
    chunks = [(c * kc, (c + 1) * kc) for c in range(s_len // kc)]

    @pl.when(n == 0)
    def _():
        s_b[...] = jnp.zeros_like(s_b)
        m_b[...] = jnp.zeros_like(m_b)

    def step(s_w, m_w, s_r, m_r):
        m_prev = m_r[...]
        qt = qt_ref[0, 0]
        l = None
        acc = None
        m = None
        for lo, hi in chunks:
            p = jnp.exp2(s_r[lo:hi, :] - m_prev)
            lc = jnp.sum(p, axis=0, keepdims=True)
            l = lc if l is None else l + lc
            pv = jnp.dot(vt_ref[0, 0, :, lo:hi], p.astype(BF16), preferred_element_type=F32)
            acc = pv if acc is None else acc + pv

            sc = jnp.dot(k_ref[0, 0, lo:hi, :], qt, preferred_element_type=F32)
            s_w[lo:hi, :] = sc
            mc = jnp.max(sc, axis=0, keepdims=True)
            m = mc if m is None else jnp.maximum(m, mc)
        o_ref[0] = (acc * (1.0 / l)).T.astype(BF16)
        m_w[...] = m

    @pl.when(n % 2 == 0)
    def _():
        step(s_a, m_a, s_b, m_b)

    @pl.when(n % 2 == 1)
    def _():
        step(s_b, m_b, s_a, m_a)


def _attention(qt, k, vt, tq, kc):
    b, _, _, s = qt.shape
    nq = s // tq
    n_tiles = b * N_HEADS * nq

    def bhi(tile):
        return tile // (N_HEADS * nq), (tile // nq) % N_HEADS, tile % nq

    def score_tile(n):
        return bhi(jnp.minimum(n, n_tiles - 1))

    def value_tile(n):
        return bhi(jnp.maximum(n - 1, 0))

    def qt_map(n):
        bi, h, i = score_tile(n)
        return bi, h, 0, i

    def k_map(n):
        bi, h, _ = score_tile(n)
        return bi, h, 0, 0

    def vt_map(n):
        bi, h, _ = value_tile(n)
        return bi, h, 0, 0

    def o_map(n):
        bi, h, i = value_tile(n)
        return bi, i, h

    return pl.pallas_call(
        functools.partial(_attn_body, kc=kc),
        grid=(n_tiles + 1,),
        in_specs=[
            pl.BlockSpec((1, 1, QK_PAD, tq), qt_map),
            pl.BlockSpec((1, 1, s, QK_PAD), k_map),
            pl.BlockSpec((1, 1, V_DIM, s), vt_map),
        ],
        out_specs=pl.BlockSpec((1, tq, V_DIM), o_map),
        out_shape=jax.ShapeDtypeStruct((b, s, N_HEADS * V_DIM), BF16),
        scratch_shapes=[pltpu.VMEM((s, tq), F32), pltpu.VMEM((1, tq), F32),
                        pltpu.VMEM((s, tq), F32), pltpu.VMEM((1, tq), F32)],
        compiler_params=_cparams(("arbitrary",)),
        name="attention",
    )(qt, k, vt)


def _merge_body(u_ref, gb_ref, gc_ref, up_ref, gcp_ref, un_ref, gcn_ref, o_ref, gtc_ref, gta_ref,
                cw_ref, wc_ref, wa_ref, out_ref):
    i = pl.program_id(1)
    last = pl.num_programs(1) - 1
    halo = up_ref.shape[0]
    v = gc_ref[...].astype(F32) * u_ref[...].astype(F32)
    tm = v.shape[0]
    v_before = gcp_ref[halo - 1:halo, :].astype(F32) * up_ref[halo - 1:halo, :].astype(F32)
    v_after = gcn_ref[0:1, :].astype(F32) * un_ref[0:1, :].astype(F32)
    v_before = jnp.where(i == 0, 0.0, v_before)
    v_after = jnp.where(i == last, 0.0, v_after)
    row = lax.broadcasted_iota(jnp.int32, (tm, 1), 0)
    v_prev = jnp.where(row == 0, v_before, pltpu.roll(v, 1, axis=0))
    v_next = jnp.where(row == tm - 1, v_after, pltpu.roll(v, tm - 1, axis=0))
    cw = cw_ref[...]
    conv = v_prev * cw[0:1, :] + v * cw[1:2, :] + v_next * cw[2:3, :]
    z = (gb_ref[...].astype(F32) * conv).astype(BF16)
    conv_out = jnp.dot(z, wc_ref[...], preferred_element_type=F32)
    attn_out = jnp.dot(o_ref[...], wa_ref[...], preferred_element_type=F32)
    merged = (jax.nn.sigmoid(gtc_ref[...].astype(F32)) * conv_out
              + jax.nn.sigmoid(gta_ref[...].astype(F32)) * attn_out)
    out_ref[...] = merged.astype(BF16)


def _merge(proj, o2d, conv_w, wc, wa, b, s, tm, halo):
    ns = s // tm
    t = b * s
    hb = tm // halo
    nh = t // halo
    row = lambda bi, i: bi * ns + i
    prev = lambda bi, i: (jnp.maximum(row(bi, i) * hb - 1, 0))
    nxt = lambda bi, i: (jnp.minimum((row(bi, i) + 1) * hb, nh - 1))
    gate0 = KPE_OFF // D_MODEL
    return pl.pallas_call(
        _merge_body,
        grid=(b, ns),
        in_specs=[
            pl.BlockSpec((tm, CONV_CH), lambda bi, i: (row(bi, i), 0)),
            pl.BlockSpec((tm, CONV_CH), lambda bi, i: (row(bi, i), 1)),
            pl.BlockSpec((tm, CONV_CH), lambda bi, i: (row(bi, i), 2)),
            pl.BlockSpec((halo, CONV_CH), lambda bi, i: (prev(bi, i), 0)),
            pl.BlockSpec((halo, CONV_CH), lambda bi, i: (prev(bi, i), 2)),
            pl.BlockSpec((halo, CONV_CH), lambda bi, i: (nxt(bi, i), 0)),
            pl.BlockSpec((halo, CONV_CH), lambda bi, i: (nxt(bi, i), 2)),
            pl.BlockSpec((tm, D_MODEL), lambda bi, i: (row(bi, i), 0)),
            pl.BlockSpec((tm, D_MODEL), lambda bi, i: (row(bi, i), gate0)),
            pl.BlockSpec((tm, D_MODEL), lambda bi, i: (row(bi, i), gate0 + 1)),
            pl.BlockSpec((3, CONV_CH), lambda bi, i: (0, 0)),
            pl.BlockSpec((CONV_CH, D_MODEL), lambda bi, i: (0, 0)),
            pl.BlockSpec((D_MODEL, D_MODEL), lambda bi, i: (0, 0)),
        ],
        out_specs=pl.BlockSpec((tm, D_MODEL), lambda bi, i: (row(bi, i), 0)),
        out_shape=jax.ShapeDtypeStruct((t, D_MODEL), BF16),
        compiler_params=_cparams(("arbitrary", "arbitrary")),
        name="merge",
    )(proj, proj, proj, proj, proj, proj, proj, o2d, proj, proj, conv_w, wc, wa)


def _out_route_body(xa_ref, mga_ref, xb_ref, mgb_ref, wo_ref, g2_ref, wrh_ref, wrl_ref, br_ref,
                    h_ref, xn_ref, route_ref, routet_ref, cnt_ref, base_ref, *, steps_a):
    step = pl.program_id(0)

    @pl.when(step == 0)
    def _():
        base_ref[...] = jnp.zeros_like(base_ref)

    in_a = step < steps_a
    x = jnp.where(in_a, xa_ref[...], xb_ref[...])
    mg = jnp.where(in_a, mga_ref[...], mgb_ref[...])
    h = x + jnp.dot(mg, wo_ref[...], preferred_element_type=F32)
    h_ref[...] = h
    ms = jnp.mean(h * h, axis=-1, keepdims=True)
    xn = h * lax.rsqrt(ms + EPS) * g2_ref[...]
    xn_ref[...] = xn
    tm = xn.shape[0]

    hi = xn.astype(BF16)
    lo = (xn - hi.astype(F32)).astype(BF16)
    wrh = wrh_ref[...]
    logits = (jnp.dot(hi, wrh, preferred_element_type=F32)
              + jnp.dot(lo, wrh, preferred_element_type=F32)
              + jnp.dot(hi, wrl_ref[...], preferred_element_type=F32)) + br_ref[...]

    lane = lax.broadcasted_iota(jnp.int32, (tm, LANES), 1)
    neg = -jnp.inf
    lg = jnp.where(lane < N_GROUPS, logits, neg)
    gmax = jnp.max(lg, axis=-1, keepdims=True)
    g_p = 1.0 / jnp.sum(jnp.exp(lg - gmax), axis=-1, keepdims=True)
    g_sel = jnp.min(jnp.where(lg == gmax, lane, LANES), axis=-1, keepdims=True)
    lo_lane = EXP_LANE0 + g_sel * EXP_PER_GROUP
    le = jnp.where((lane >= lo_lane) & (lane < lo_lane + EXP_PER_GROUP), logits, neg)
    m1 = jnp.max(le, axis=-1, keepdims=True)
    i1 = jnp.min(jnp.where(le == m1, lane, LANES), axis=-1, keepdims=True)
    le2 = jnp.where(lane == i1, neg, le)
    m2 = jnp.max(le2, axis=-1, keepdims=True)
    i2 = jnp.min(jnp.where(le2 == m2, lane, LANES), axis=-1, keepdims=True)
    e2 = jnp.exp(m2 - m1)
    gate1 = g_p / (1.0 + e2)
    gate2 = g_p * e2 / (1.0 + e2)

    sel1 = lane == i1
    sel2 = lane == i2
    onehot = jnp.where(sel1 | sel2, 1.0, 0.0)
    r_i = lax.broadcasted_iota(jnp.int32, (tm, tm), 0)
    c_i = lax.broadcasted_iota(jnp.int32, (tm, tm), 1)
    lower = jnp.where(r_i > c_i, 1.0, 0.0).astype(BF16)
    before = jnp.dot(lower, onehot.astype(BF16), preferred_element_type=F32) + base_ref[...]
    rank1 = jnp.sum(jnp.where(sel1, before, 0.0), axis=-1, keepdims=True)
    rank2 = jnp.sum(jnp.where(sel2, before, 0.0), axis=-1, keepdims=True)
    base_ref[...] = base_ref[...] + jnp.sum(onehot, axis=0, keepdims=True)
    cnt_ref[...] = base_ref[...]

    e1f = (i1 - EXP_LANE0).astype(F32)
    e2f = (i2 - EXP_LANE0).astype(F32)
    cols = (e1f, e2f, gate1, gate2, rank1, rank2)
    route = jnp.zeros((tm, LANES), F32)
    for c, val in enumerate(cols):
        route = jnp.where(lane == c, val, route)
    route_ref[...] = route
    routet_ref[...] = route.T[0:8, :]


def _out_route(xa, mga, xb, mgb, wo, ln2_g, wr_hi, wr_lo, br, tm):
    steps_a, steps_b = xa.shape[0] // tm, xb.shape[0] // tm
    t = xa.shape[0] + xb.shape[0]
    const = lambda shape: pl.BlockSpec(shape, lambda i: (0,) * len(shape))
    seg_a = pl.BlockSpec((tm, D_MODEL), lambda i: (jnp.minimum(i, steps_a - 1), 0))
    seg_b = pl.BlockSpec((tm, D_MODEL), lambda i: (jnp.maximum(i - steps_a, 0), 0))
    return pl.pallas_call(
        functools.partial(_out_route_body, steps_a=steps_a),
        grid=(steps_a + steps_b,),
        in_specs=[
            seg_a, seg_a, seg_b, seg_b,
            const((D_MODEL, D_MODEL)), const((1, D_MODEL)),
            const((D_MODEL, LANES)), const((D_MODEL, LANES)), const((1, LANES)),
        ],
        out_specs=[
            pl.BlockSpec((tm, D_MODEL), lambda i: (i, 0)),
            pl.BlockSpec((tm, D_MODEL), lambda i: (i, 0)),
            pl.BlockSpec((tm, LANES), lambda i: (i, 0)),
            pl.BlockSpec((8, tm), lambda i: (0, i)),
            const((1, LANES)),
        ],
        out_shape=[
            jax.ShapeDtypeStruct((t, D_MODEL), F32),
            jax.ShapeDtypeStruct((t, D_MODEL), F32),
            jax.ShapeDtypeStruct((t, LANES), F32),
            jax.ShapeDtypeStruct((8, t), F32),
            jax.ShapeDtypeStruct((1, LANES), F32),
        ],
        scratch_shapes=[pltpu.VMEM((1, LANES), F32)],
        compiler_params=_cparams(("arbitrary",)),
        name="out_route",
    )(xa, mga, xb, mgb, wo, ln2_g, wr_hi, wr_lo, br)


def _plan_body(cnt_ref, routet_ref, pos_ref, start_ref, blk_exp_ref, blk_src_ref, nvalid_ref):
    n_blocks = blk_exp_ref.shape[0]

    def per_expert(e, acc):
        c = cnt_ref[EXP_LANE0 + e]
        nb = (c + ROW_BLOCK - 1) // ROW_BLOCK
        start_ref[e] = acc * ROW_BLOCK

        def fill(j, carry):
            blk_exp_ref[acc + j] = e
            blk_src_ref[acc + j] = acc + j
            return carry

        lax.fori_loop(0, nb, fill, 0)
        return acc + nb

    nvalid = lax.fori_loop(0, N_EXPERTS, per_expert, 0)
    nvalid_ref[0] = nvalid
    last_exp = blk_exp_ref[nvalid - 1]

    def tail(j, carry):
        blk_exp_ref[j] = last_exp
        blk_src_ref[j] = nvalid - 1
        return carry

    lax.fori_loop(nvalid, n_blocks, tail, 0)

    ef = routet_ref[0:2, :]
    off = jnp.zeros_like(ef)
    for e in range(N_EXPERTS):
        off = jnp.where(ef == float(e), start_ref[e].astype(F32), off)
    pos_ref[...] = (off + routet_ref[4:6, :]).astype(jnp.int32)


def _plan(counts_i32, routet, n_blocks):
    t = routet.shape[1]
    smem = lambda: pl.BlockSpec(memory_space=pltpu.SMEM)
    return pl.pallas_call(
        _plan_body,
        in_specs=[smem(), pl.BlockSpec(memory_space=pltpu.VMEM)],
        out_specs=[pl.BlockSpec(memory_space=pltpu.VMEM), smem(), smem(), smem(), smem()],
        out_shape=[
            jax.ShapeDtypeStruct((2, t), jnp.int32),
            jax.ShapeDtypeStruct((N_EXPERTS,), jnp.int32),
            jax.ShapeDtypeStruct((n_blocks,), jnp.int32),
            jax.ShapeDtypeStruct((n_blocks,), jnp.int32),
            jax.ShapeDtypeStruct((1,), jnp.int32),
        ],
        compiler_params=pltpu.CompilerParams(vmem_limit_bytes=VMEM_LIMIT),
        name="plan",
    )(counts_i32, routet)


def _dispatch_body(pos_ref, cnt_ref, start_ref, nvalid_ref, xn_ref, xs_hbm, zero_ref, sem):
    tm = xn_ref.shape[0]
    n_blocks = xs_hbm.shape[0] // ROW_BLOCK

    def put(src_ref, src_row, dst_row):
        return pltpu.make_async_copy(src_ref.at[pl.ds(src_row, 1)], xs_hbm.at[pl.ds(dst_row, 1)], sem)

    def put_block(blk):
        return pltpu.make_async_copy(zero_ref, xs_hbm.at[pl.ds(blk * ROW_BLOCK, ROW_BLOCK)], sem)

    @pl.when(pl.program_id(0) == 0)
    def _():
        zero_ref[...] = jnp.zeros_like(zero_ref)

        def tail(blk, carry):
            put_block(blk).start()
            return carry

        def tail_done(blk, carry):
            put_block(0).wait()
            return carry

        lax.fori_loop(nvalid_ref[0], n_blocks, tail, 0)

        def pad_rows(wait):
            def per_expert(e, carry):
                c = cnt_ref[EXP_LANE0 + e]
                end = (c + ROW_BLOCK - 1) // ROW_BLOCK * ROW_BLOCK
                base = start_ref[e]

                def fill(r, inner):
                    if wait:
                        put(zero_ref, 0, 0).wait()
                    else:
                        put(zero_ref, 0, base + r).start()
                    return inner

                lax.fori_loop(c, end, fill, 0)
                return carry

            lax.fori_loop(0, N_EXPERTS, per_expert, 0)

        pad_rows(wait=False)
        lax.fori_loop(nvalid_ref[0], n_blocks, tail_done, 0)
        pad_rows(wait=True)

    def issue(t, carry):
        put(xn_ref, t, pos_ref[0, t]).start()
        put(xn_ref, t, pos_ref[1, t]).start()
        return carry

    lax.fori_loop(0, tm, issue, 0, unroll=8)

    def drain(t, carry):
        put(xn_ref, 0, 0).wait()
        put(xn_ref, 0, 0).wait()
        return carry

    lax.fori_loop(0, tm, drain, 0, unroll=8)


def _dispatch(pos_t, counts_i32, starts, nvalid, xn2, n_rows, tm):
    t = xn2.shape[0]
    smem = lambda: pl.BlockSpec(memory_space=pltpu.SMEM)
    return pl.pallas_call(
        _dispatch_body,
        grid=(t // tm,),
        in_specs=[
            pl.BlockSpec((2, tm), lambda i: (0, i), memory_space=pltpu.SMEM),
            smem(), smem(), smem(),
            pl.BlockSpec((tm, D_MODEL), lambda i: (i, 0)),
        ],
        out_specs=pl.BlockSpec(memory_space=pl.ANY),
        out_shape=jax.ShapeDtypeStruct((n_rows, D_MODEL), F32),
        scratch_shapes=[pltpu.VMEM((ROW_BLOCK, D_MODEL), F32), pltpu.SemaphoreType.DMA(())],
        compiler_params=pltpu.CompilerParams(dimension_semantics=("arbitrary",),
                                             vmem_limit_bytes=VMEM_LIMIT, has_side_effects=True),
        name="dispatch",
    )(pos_t, counts_i32, starts, nvalid, xn2)


def _expert_body(blk_exp_ref, blk_src_ref, nvalid_ref, xs_ref, wg_ref, wu_ref, wd_ref, ys_ref):
    valid = pl.program_id(0) < nvalid_ref[0]

    @pl.when(valid)
    def _():
        xb = xs_ref[...].astype(BF16)
        g = jnp.dot(xb, wg_ref[0].astype(BF16), preferred_element_type=F32)
        u = jnp.dot(xb, wu_ref[0].astype(BF16), preferred_element_type=F32)
        hmid = (g * jax.nn.sigmoid(g) * u).astype(BF16)
        ys_ref[...] = jnp.dot(hmid, wd_ref[0].astype(BF16), preferred_element_type=F32)

    @pl.when(jnp.logical_not(valid))
    def _():
        ys_ref[...] = jnp.zeros_like(ys_ref)


def _experts(blk_exp, blk_src, nvalid, xs, wg, wu, wd):
    n_rows = xs.shape[0]
    n_blocks = n_rows // ROW_BLOCK
    grid_spec = pltpu.PrefetchScalarGridSpec(
        num_scalar_prefetch=3,
        grid=(n_blocks,),
        in_specs=[
            pl.BlockSpec((ROW_BLOCK, D_MODEL), lambda i, be, bs, nv: (bs[i], 0)),
            pl.BlockSpec((1, D_MODEL, D_EXPERT), lambda i, be, bs, nv: (be[i], 0, 0)),
            pl.BlockSpec((1, D_MODEL, D_EXPERT), lambda i, be, bs, nv: (be[i], 0, 0)),
            pl.BlockSpec((1, D_EXPERT, D_MODEL), lambda i, be, bs, nv: (be[i], 0, 0)),
        ],
        out_specs=pl.BlockSpec((ROW_BLOCK, D_MODEL), lambda i, be, bs, nv: (i, 0)),
    )
    return pl.pallas_call(
        _expert_body,
        grid_spec=grid_spec,
        out_shape=jax.ShapeDtypeStruct((n_rows, D_MODEL), F32),
        compiler_params=_cparams(("arbitrary",)),
        name="experts",
    )(blk_exp, blk_src, nvalid, xs, wg, wu, wd)


def _combine_body(pos_ref, pos_next_ref, ys_hbm, h_ref, route_ref, ya_ref, yb_ref, buf_ref, sems, *,
                  steps_a):
    n = pl.program_id(0)
    tm = h_ref.shape[0]
    slot = n % 2

    def row_copy(p_ref, k, t, to_slot):
        return pltpu.make_async_copy(ys_hbm.at[pl.ds(p_ref[k, t], 1)],
                                     buf_ref.at[to_slot, k, pl.ds(t, 1)], sems.at[to_slot])

    def gather(p_ref, to_slot):
        def issue(t, carry):
            row_copy(p_ref, 0, t, to_slot).start()
            row_copy(p_ref, 1, t, to_slot).start()
            return carry

        lax.fori_loop(0, tm, issue, 0, unroll=8)

    @pl.when(n == 0)
    def _():
        gather(pos_ref, 0)

    @pl.when(n + 1 < pl.num_programs(0))
    def _():
        gather(pos_next_ref, 1 - slot)

    def drain(t, carry):
        row_copy(pos_ref, 0, t, slot).wait()
        row_copy(pos_ref, 1, t, slot).wait()
        return carry

    lax.fori_loop(0, tm, drain, 0, unroll=8)
    route = route_ref[...]
    y = h_ref[...] + route[:, 2:3] * buf_ref[slot, 0] + route[:, 3:4] * buf_ref[slot, 1]

    @pl.when(n < steps_a)
    def _():
        ya_ref[...] = y

    @pl.when(n >= steps_a)
    def _():
        yb_ref[...] = y


def _combine(pos_t, ys, h, route, t_a, tm):
    t = h.shape[0]
    nt = t // tm
    steps_a = t_a // tm
    return pl.pallas_call(
        functools.partial(_combine_body, steps_a=steps_a),
        grid=(nt,),
        in_specs=[
            pl.BlockSpec((2, tm), lambda i: (0, i), memory_space=pltpu.SMEM),
            pl.BlockSpec((2, tm), lambda i: (0, jnp.minimum(i + 1, nt - 1)), memory_space=pltpu.SMEM),
            pl.BlockSpec(memory_space=pl.ANY),
            pl.BlockSpec((tm, D_MODEL), lambda i: (i, 0)),
            pl.BlockSpec((tm, LANES), lambda i: (i, 0)),
        ],
        out_specs=[
            pl.BlockSpec((tm, D_MODEL), lambda i: (jnp.minimum(i, steps_a - 1), 0)),
            pl.BlockSpec((tm, D_MODEL), lambda i: (jnp.maximum(i - steps_a, 0), 0)),
        ],
        out_shape=[
            jax.ShapeDtypeStruct((t_a, D_MODEL), F32),
            jax.ShapeDtypeStruct((t - t_a, D_MODEL), F32),
        ],
        scratch_shapes=[pltpu.VMEM((2, 2, tm, D_MODEL), F32), pltpu.SemaphoreType.DMA((2,))],
        compiler_params=_cparams(("arbitrary",)),
        name="combine",
    )(pos_t, pos_t, ys, h, route)


def _pad_rope(a, axis):
    x1, x2 = jnp.split(a, 2, axis=axis)
    z = jnp.zeros_like(x1)
    return jnp.concatenate([x1, z, x2, z], axis=axis)


def _prepare(ln1_g, w_in, conv_w, q_a_norm_g, w_uq, kv_a_norm_g, w_ukv, q_norm_g, k_norm_g,
             w_conv_out, w_attn_out, w_out, ln2_g, w_router_group, b_router_group,
             w_router_exp, b_router_exp, w_gate, w_up, w_down):
    w_in0 = w_in[0]
    w_main = jnp.concatenate([w_in0[:, :KPE_OFF], w_in0[:, KPE_OFF + QK_ROPE:]], axis=1).astype(BF16)
    w_pe = _pad_rope(w_in0[:, KPE_OFF:KPE_OFF + QK_ROPE], 1).astype(BF16)

    wq = w_uq[0].reshape(Q_LORA, N_HEADS, QK_DIM)
    wq = jnp.concatenate([wq[:, :, :QK_NOPE], _pad_rope(wq[:, :, QK_NOPE:], 2)], axis=2)
    wq_t = wq.reshape(Q_LORA, N_HEADS * QK_PAD).T.astype(BF16)
    wkv = w_ukv[0].reshape(KV_LORA, N_HEADS, QK_NOPE + V_DIM)
    wkn = wkv[:, :, :QK_NOPE].reshape(KV_LORA, N_HEADS * QK_NOPE).astype(BF16)
    wv_t = wkv[:, :, QK_NOPE:].reshape(KV_LORA, N_HEADS * V_DIM).T.astype(BF16)

    qg = q_norm_g[0]
    score_scale = QK_DIM ** -0.5 * math.log2(math.e)
    gq = (jnp.concatenate([qg[:QK_NOPE], _pad_rope(qg[QK_NOPE:], 0)]) * score_scale).reshape(QK_PAD, 1)
    kg = k_norm_g[0]
    gkn = kg[:QK_NOPE].reshape(1, LANES)
    gkr = _pad_rope(kg[QK_NOPE:], 0).reshape(1, LANES)

    wr = jnp.concatenate([w_router_group[0], w_router_exp[0],
                          jnp.zeros((D_MODEL, LANES - N_GROUPS - N_EXPERTS), F32)], axis=1)
    wr_hi = wr.astype(BF16)
    wr_lo = (wr - wr_hi.astype(F32)).astype(BF16)
    br = jnp.concatenate([b_router_group[0], b_router_exp[0],
                          jnp.zeros((LANES - N_GROUPS - N_EXPERTS,), F32)]).reshape(1, LANES)
    return dict(
        ln1_g=ln1_g, w_main=w_main, w_pe=w_pe, conv_w=conv_w[0],
        qkv=(q_a_norm_g, kv_a_norm_g, gq, gkn, gkr, wq_t, wkn, wv_t),
        wc=w_conv_out[0].astype(BF16), wa=w_attn_out[0].astype(BF16), wo=w_out[0].astype(BF16),
        ln2_g=ln2_g, wr_hi=wr_hi, wr_lo=wr_lo, br=br,
        wg=w_gate[0], wu=w_up[0], wd=w_down[0],
    )


def _rope_tables(s):
    inv = ROPE_THETA ** (-jnp.arange(0, QK_ROPE, 2, dtype=F32) / QK_ROPE)
    ang = jnp.arange(s, dtype=F32)[:, None] * inv[None, :]
    cos, sin = jnp.cos(ang), jnp.sin(ang)
    z = jnp.zeros_like(cos)
    cos_k = jnp.concatenate([cos, z, cos, z], axis=1)
    sin_k = jnp.concatenate([-sin, z, sin, z], axis=1)
    return cos_k, sin_k, cos.T, sin.T


def _tiles(b, s):
    t = b * s
    pick = lambda n, pref: pref if n % pref == 0 else n
    return dict(
        in_tm=pick(t, 512), in_tn=2048,
        qkv_tm=pick(s, 256),
        attn_tq=pick(s, 512), attn_kc=pick(s, 256),
        merge_tm=pick(s, 256), halo=16,
    )


def _moe_tiles(t_a, t_b):
    both = math.gcd(t_a, t_b)
    pick = lambda pref: pref if both % pref == 0 else both
    return dict(route_tm=pick(512), disp_tm=pick(512), comb_tm=pick(256))


def _mixer(x, p):
    b, s, _ = x.shape
    t = b * s
    tl = _tiles(b, s)
    x2d = x.reshape(t, D_MODEL)
    proj, kpe = _in_proj(x2d, p["ln1_g"], p["w_main"], p["w_pe"], tl["in_tm"], tl["in_tn"])
    qt, k, vt = _qkv(proj, kpe, _rope_tables(s), p["qkv"], b, s, tl["qkv_tm"])
    o = _attention(qt, k, vt, tl["attn_tq"], tl["attn_kc"])
    merged = _merge(proj, o.reshape(t, D_MODEL), p["conv_w"], p["wc"], p["wa"], b, s,
                    tl["merge_tm"], tl["halo"])
    return x2d, merged


def _forward(x_a, x_b, p):
    xa, mga = _mixer(x_a, p)
    xb, mgb = _mixer(x_b, p)
    t_a, t_b = xa.shape[0], xb.shape[0]
    t = t_a + t_b
    tl = _moe_tiles(t_a, t_b)
    h, xn2, route, route_t, counts = _out_route(xa, mga, xb, mgb, p["wo"], p["ln2_g"], p["wr_hi"],
                                                p["wr_lo"], p["br"], tl["route_tm"])
    n_blocks = 2 * t // ROW_BLOCK + N_EXPERTS
    counts_i32 = counts.reshape(LANES).astype(jnp.int32)
    pos_t, starts, blk_exp, blk_src, nvalid = _plan(counts_i32, route_t, n_blocks)
    xs = _dispatch(pos_t, counts_i32, starts, nvalid, xn2, n_blocks * ROW_BLOCK, tl["disp_tm"])
    ys = _experts(blk_exp, blk_src, nvalid, xs, p["wg"], p["wu"], p["wd"])
    y_a, y_b = _combine(pos_t, ys, h, route, t_a, tl["comb_tm"])
    return y_a.reshape(x_a.shape), y_b.reshape(x_b.shape)


def kernel(x_prompt, x_sample, ln1_g, w_in, conv_w, q_a_norm_g, w_uq, kv_a_norm_g, w_ukv, q_norm_g,
           k_norm_g, w_conv_out, w_attn_out, w_out, ln2_g, w_router_group, b_router_group,
           w_router_exp, b_router_exp, w_gate, w_up, w_down):
    p = _prepare(ln1_g, w_in, conv_w, q_a_norm_g, w_uq, kv_a_norm_g, w_ukv, q_norm_g, k_norm_g,
                 w_conv_out, w_attn_out, w_out, ln2_g, w_router_group, b_router_group,
                 w_router_exp, b_router_exp, w_gate, w_up, w_down)
    return _forward(x_prompt, x_sample, p)
```

```python
import functools
import math

import jax
import jax.numpy as jnp
from jax import lax
from jax.experimental import pallas as pl
from jax.experimental.pallas import tpu as pltpu

F32 = jnp.float32
BF16 = jnp.bfloat16

D_MODEL = 2048
CONV_CH = 1024
N_HEADS = 16
QK_NOPE = 128
QK_ROPE = 64
HALF_ROPE = QK_ROPE // 2
QK_DIM = QK_NOPE + QK_ROPE
QK_PAD = 256
V_DIM = 128
Q_LORA = 512
KV_LORA = 512
ROPE_THETA = 10000.0
N_GROUPS = 8
EXP_PER_GROUP = 8
N_EXPERTS = 64
D_EXPERT = 512
EPS = 1e-6
PROJ_W = 8192
KPE_OFF = 4096
LANES = 128
EXP_LANE0 = N_GROUPS
ROW_BLOCK = 256
VMEM_LIMIT = 56 * 1024 * 1024

_NT = (((1,), (1,)), ((), ()))


def _cparams(sem):
    return pltpu.CompilerParams(dimension_semantics=sem, vmem_limit_bytes=VMEM_LIMIT)


D_HALF = D_MODEL // 2
U32 = jnp.uint32


def _pack_halves(x):
    lo = lax.bitcast_convert_type(x[:, :D_HALF].astype(BF16).astype(F32), U32)
    hi = lax.bitcast_convert_type(x[:, D_HALF:].astype(BF16).astype(F32), U32)
    return (lo >> 16) | hi


def _unpack_halves(w):
    lo = lax.bitcast_convert_type(w << 16, F32)
    hi = lax.bitcast_convert_type(w & jnp.uint32(0xFFFF0000), F32)
    return lo, hi


def _inproj_body(x_ref, g_ref, w_ref, wpe_ref, out_ref, kpe_ref, xn_ref):
    @pl.when(pl.program_id(1) == 0)
    def _():
        x = x_ref[...]
        ms = jnp.mean(x * x, axis=-1, keepdims=True)
        xn = (x * lax.rsqrt(ms + EPS) * g_ref[...]).astype(BF16)
        xn_ref[...] = xn
        kpe_ref[...] = jnp.dot(xn, wpe_ref[...], preferred_element_type=F32)

    out_ref[...] = jnp.dot(xn_ref[...], w_ref[...], preferred_element_type=F32).astype(BF16)


def _in_proj(x2d, ln1_g, w_main, w_pe, tm, tn):
    t = x2d.shape[0]
    return pl.pallas_call(
        _inproj_body,
        grid=(t // tm, PROJ_W // tn),
        in_specs=[
            pl.BlockSpec((tm, D_MODEL), lambda i, j: (i, 0)),
            pl.BlockSpec((1, D_MODEL), lambda i, j: (0, 0)),
            pl.BlockSpec((D_MODEL, tn), lambda i, j: (0, j)),
            pl.BlockSpec((D_MODEL, LANES), lambda i, j: (0, 0)),
        ],
        out_specs=[
            pl.BlockSpec((tm, tn), lambda i, j: (i, j)),
            pl.BlockSpec((tm, LANES), lambda i, j: (i, 0)),
        ],
        out_shape=[
            jax.ShapeDtypeStruct((t, PROJ_W), BF16),
            jax.ShapeDtypeStruct((t, LANES), F32),
        ],
        scratch_shapes=[pltpu.VMEM((tm, D_MODEL), BF16)],
        compiler_params=_cparams(("arbitrary", "arbitrary")),
        name="in_proj",
    )(x2d, ln1_g, w_main, w_pe)


def _qkv_body(ql_ref, kvl_ref, kpe_ref, cosk_ref, sink_ref, cost_ref, sint_ref,
              gqa_ref, gkva_ref, gq_ref, gkn_ref, gkr_ref, wq_ref, wkn_ref, wv_ref,
              qt_ref, k_ref, vt_ref):
    def latent_norm(ref, g_ref):
        v = ref[...].astype(F32)
        ms = jnp.mean(v * v, axis=-1, keepdims=True)
        return (v * lax.rsqrt(ms + EPS) * g_ref[...]).astype(BF16)

    qn = latent_norm(ql_ref, gqa_ref)
    kvn = latent_norm(kvl_ref, gkva_ref)
    tm = qn.shape[0]

    kn = jnp.dot(kvn, wkn_ref[...], preferred_element_type=F32)
    kpe = kpe_ref[...]
    ss_pe = jnp.sum(kpe * kpe, axis=-1, keepdims=True)
    kr = kpe * gkr_ref[...]
    kr = kr * cosk_ref[...] + pltpu.roll(kr, 2 * HALF_ROPE, axis=1) * sink_ref[...]
    gkn = gkn_ref[...]
    for h in range(N_HEADS):
        kh = kn[:, h * QK_NOPE:(h + 1) * QK_NOPE]
        ss = jnp.sum(kh * kh, axis=-1, keepdims=True) + ss_pe
        r = lax.rsqrt(ss * (1.0 / QK_DIM) + EPS)
        k_ref[0, h, :, 0:QK_NOPE] = (kh * gkn * r).astype(BF16)
        k_ref[0, h, :, QK_NOPE:QK_PAD] = (kr * r).astype(BF16)

    vt = lax.dot_general(wv_ref[...], kvn, _NT, preferred_element_type=F32)
    for h in range(N_HEADS):
        vt_ref[0, h] = vt[h * V_DIM:(h + 1) * V_DIM, :].astype(BF16)

    cost = cost_ref[...]
    sint = sint_ref[...]
    gq = gq_ref[...]
    zeros = jnp.zeros((HALF_ROPE, tm), BF16)
    for h in range(N_HEADS):
        qt = lax.dot_general(wq_ref[h * QK_PAD:(h + 1) * QK_PAD, :], qn, _NT,
                             preferred_element_type=F32)
        ss = jnp.sum(qt * qt, axis=0, keepdims=True)
        r = lax.rsqrt(ss * (1.0 / QK_DIM) + EPS)
        qs = qt * gq * r
        x1 = qs[QK_NOPE:QK_NOPE + HALF_ROPE]
        x2 = qs[QK_NOPE + 2 * HALF_ROPE:QK_NOPE + 3 * HALF_ROPE]
        qt_ref[0, h, 0:QK_NOPE, :] = qs[0:QK_NOPE].astype(BF16)
        qt_ref[0, h, QK_NOPE:QK_NOPE + HALF_ROPE, :] = (x1 * cost - x2 * sint).astype(BF16)
        qt_ref[0, h, QK_NOPE + HALF_ROPE:QK_NOPE + 2 * HALF_ROPE, :] = zeros
        qt_ref[0, h, QK_NOPE + 2 * HALF_ROPE:QK_NOPE + 3 * HALF_ROPE, :] = (
            x1 * sint + x2 * cost).astype(BF16)
        qt_ref[0, h, QK_NOPE + 3 * HALF_ROPE:QK_PAD, :] = zeros


def _qkv(proj, kpe, tabs, wts, b, s, tm):
    ns = s // tm
    cos_k, sin_k, cos_t, sin_t = tabs
    gqa, gkva, gq, gkn, gkr, wq_t, wkn, wv_t = wts
    const = lambda shape: pl.BlockSpec(shape, lambda bi, i: (0,) * len(shape))
    return pl.pallas_call(
        _qkv_body,
        grid=(b, ns),
        in_specs=[
            pl.BlockSpec((tm, Q_LORA), lambda bi, i: (bi * ns + i, 3 * CONV_CH // Q_LORA)),
            pl.BlockSpec((tm, KV_LORA), lambda bi, i: (bi * ns + i, 3 * CONV_CH // KV_LORA + 1)),
            pl.BlockSpec((tm, LANES), lambda bi, i: (bi * ns + i, 0)),
            pl.BlockSpec((tm, LANES), lambda bi, i: (i, 0)),
            pl.BlockSpec((tm, LANES), lambda bi, i: (i, 0)),
            pl.BlockSpec((HALF_ROPE, tm), lambda bi, i: (0, i)),
            pl.BlockSpec((HALF_ROPE, tm), lambda bi, i: (0, i)),
            const((1, Q_LORA)), const((1, KV_LORA)), const((QK_PAD, 1)),
            const((1, LANES)), const((1, LANES)),
            const((N_HEADS * QK_PAD, Q_LORA)),
            const((KV_LORA, N_HEADS * QK_NOPE)),
            const((N_HEADS * V_DIM, KV_LORA)),
        ],
        out_specs=[
            pl.BlockSpec((1, N_HEADS, QK_PAD, tm), lambda bi, i: (bi, 0, 0, i)),
            pl.BlockSpec((1, N_HEADS, tm, QK_PAD), lambda bi, i: (bi, 0, i, 0)),
            pl.BlockSpec((1, N_HEADS, V_DIM, tm), lambda bi, i: (bi, 0, 0, i)),
        ],
        out_shape=[
            jax.ShapeDtypeStruct((b, N_HEADS, QK_PAD, s), BF16),
            jax.ShapeDtypeStruct((b, N_HEADS, s, QK_PAD), BF16),
            jax.ShapeDtypeStruct((b, N_HEADS, V_DIM, s), BF16),
        ],
        compiler_params=_cparams(("arbitrary", "arbitrary")),
        name="qkv",
    )(proj, proj, kpe, cos_k, sin_k, cos_t, sin_t, gqa, gkva, gq, gkn, gkr, wq_t, wkn, wv_t)


def _attn_body(qt_ref, k_ref, vt_ref, o_ref, s_a, m_a, s_b, m_b, *, kc):
    n = pl.program_id(0)
    s_len = k_ref.shape[2]
    chunks = [(c * kc, (c + 1) * kc) for c in range(s_len // kc)]

    @pl.when(n == 0)
    def _():
        s_b[...] = jnp.zeros_like(s_b)
        m_b[...] = jnp.zeros_like(m_b)

    def step(s_w, m_w, s_r, m_r):
        m_prev = m_r[...]
        qt = qt_ref[0, 0]
        l = None
        acc = None
        m = None
        for lo, hi in chunks:
            p = jnp.exp2(s_r[lo:hi, :] - m_prev)
            lc = jnp.sum(p, axis=0, keepdims=True)
            l = lc if l is None else l + lc
            pv = jnp.dot(vt_ref[0, 0, :, lo:hi], p.astype(BF16), preferred_element_type=F32)
            acc = pv if acc is None else acc + pv

            sc = jnp.dot(k_ref[0, 0, lo:hi, :], qt, preferred_element_type=F32)
            s_w[lo:hi, :] = sc
            mc = jnp.max(sc, axis=0, keepdims=True)
            m = mc if m is None else jnp.maximum(m, mc)
        o_ref[0] = (acc * (1.0 / l)).T.astype(BF16)
        m_w[...] = m

    @pl.when(n % 2 == 0)
    def _():
        step(s_a, m_a, s_b, m_b)

    @pl.when(n % 2 == 1)
    def _():
        step(s_b, m_b, s_a, m_a)


def _attention(qt, k, vt, tq, kc):
    b, _, _, s = qt.shape
    nq = s // tq
    n_tiles = b * N_HEADS * nq

    def bhi(tile):
        return tile // (N_HEADS * nq), (tile // nq) % N_HEADS, tile % nq

    def score_tile(n):
        return bhi(jnp.minimum(n, n_tiles - 1))

    def value_tile(n):
        return bhi(jnp.maximum(n - 1, 0))

    def qt_map(n):
        bi, h, i = score_tile(n)
        return bi, h, 0, i

    def k_map(n):
        bi, h, _ = score_tile(n)
        return bi, h, 0, 0

    def vt_map(n):
        bi, h, _ = value_tile(n)
        return bi, h, 0, 0

    def o_map(n):
        bi, h, i = value_tile(n)
        return bi, i, h

    return pl.pallas_call(
        functools.partial(_attn_body, kc=kc),
        grid=(n_tiles + 1,),
        in_specs=[
            pl.BlockSpec((1, 1, QK_PAD, tq), qt_map),
            pl.BlockSpec((1, 1, s, QK_PAD), k_map),
            pl.BlockSpec((1, 1, V_DIM, s), vt_map),
        ],
        out_specs=pl.BlockSpec((1, tq, V_DIM), o_map),
        out_shape=jax.ShapeDtypeStruct((b, s, N_HEADS * V_DIM), BF16),
        scratch_shapes=[pltpu.VMEM((s, tq), F32), pltpu.VMEM((1, tq), F32),
                        pltpu.VMEM((s, tq), F32), pltpu.VMEM((1, tq), F32)],
        compiler_params=_cparams(("arbitrary",)),
        name="attention",
    )(qt, k, vt)


def _merge_body(u_ref, gb_ref, gc_ref, up_ref, gcp_ref, un_ref, gcn_ref, o_ref, gtc_ref, gta_ref,
                cw_ref, wc_ref, wa_ref, out_ref):
    i = pl.program_id(1)
    last = pl.num_programs(1) - 1
    halo = up_ref.shape[0]
    v = gc_ref[...].astype(F32) * u_ref[...].astype(F32)
    tm = v.shape[0]
    v_before = gcp_ref[halo - 1:halo, :].astype(F32) * up_ref[halo - 1:halo, :].astype(F32)
    v_after = gcn_ref[0:1, :].astype(F32) * un_ref[0:1, :].astype(F32)
    v_before = jnp.where(i == 0, 0.0, v_before)
    v_after = jnp.where(i == last, 0.0, v_after)
    row = lax.broadcasted_iota(jnp.int32, (tm, 1), 0)
    v_prev = jnp.where(row == 0, v_before, pltpu.roll(v, 1, axis=0))
    v_next = jnp.where(row == tm - 1, v_after, pltpu.roll(v, tm - 1, axis=0))
    cw = cw_ref[...]
    conv = v_prev * cw[0:1, :] + v * cw[1:2, :] + v_next * cw[2:3, :]
    z = (gb_ref[...].astype(F32) * conv).astype(BF16)
    conv_out = jnp.dot(z, wc_ref[...], preferred_element_type=F32)
    attn_out = jnp.dot(o_ref[...], wa_ref[...], preferred_element_type=F32)
    merged = (jax.nn.sigmoid(gtc_ref[...].astype(F32)) * conv_out
              + jax.nn.sigmoid(gta_ref[...].astype(F32)) * attn_out)
    out_ref[...] = merged.astype(BF16)


def _merge(proj, o2d, conv_w, wc, wa, b, s, tm, halo):
    ns = s // tm
    t = b * s
    hb = tm // halo
    nh = t // halo
    row = lambda bi, i: bi * ns + i
    prev = lambda bi, i: (jnp.maximum(row(bi, i) * hb - 1, 0))
    nxt = lambda bi, i: (jnp.minimum((row(bi, i) + 1) * hb, nh - 1))
    gate0 = KPE_OFF // D_MODEL
    return pl.pallas_call(
        _merge_body,
        grid=(b, ns),
        in_specs=[
            pl.BlockSpec((tm, CONV_CH), lambda bi, i: (row(bi, i), 0)),
            pl.BlockSpec((tm, CONV_CH), lambda bi, i: (row(bi, i), 1)),
            pl.BlockSpec((tm, CONV_CH), lambda bi, i: (row(bi, i), 2)),
            pl.BlockSpec((halo, CONV_CH), lambda bi, i: (prev(bi, i), 0)),
            pl.BlockSpec((halo, CONV_CH), lambda bi, i: (prev(bi, i), 2)),
            pl.BlockSpec((halo, CONV_CH), lambda bi, i: (nxt(bi, i), 0)),
            pl.BlockSpec((halo, CONV_CH), lambda bi, i: (nxt(bi, i), 2)),
            pl.BlockSpec((tm, D_MODEL), lambda bi, i: (row(bi, i), 0)),
            pl.BlockSpec((tm, D_MODEL), lambda bi, i: (row(bi, i), gate0)),
            pl.BlockSpec((tm, D_MODEL), lambda bi, i: (row(bi, i), gate0 + 1)),
            pl.BlockSpec((3, CONV_CH), lambda bi, i: (0, 0)),
            pl.BlockSpec((CONV_CH, D_MODEL), lambda bi, i: (0, 0)),
            pl.BlockSpec((D_MODEL, D_MODEL), lambda bi, i: (0, 0)),
        ],
        out_specs=pl.BlockSpec((tm, D_MODEL), lambda bi, i: (row(bi, i), 0)),
        out_shape=jax.ShapeDtypeStruct((t, D_MODEL), BF16),
        compiler_params=_cparams(("arbitrary", "arbitrary")),
        name="merge",
    )(proj, proj, proj, proj, proj, proj, proj, o2d, proj, proj, conv_w, wc, wa)


def _out_route_body(xa_ref, mga_ref, xb_ref, mgb_ref, wo_ref, g2_ref, wrh_ref, wrl_ref, br_ref,
                    h_ref, xn_ref, route_ref, routet_ref, cnt_ref, base_ref, *, steps_a):
    step = pl.program_id(0)

    @pl.when(step == 0)
    def _():
        base_ref[...] = jnp.zeros_like(base_ref)

    in_a = step < steps_a
    x = jnp.where(in_a, xa_ref[...], xb_ref[...])
    mg = jnp.where(in_a, mga_ref[...], mgb_ref[...])
    h = x + jnp.dot(mg, wo_ref[...], preferred_element_type=F32)
    h_ref[...] = h
    ms = jnp.mean(h * h, axis=-1, keepdims=True)
    xn = h * lax.rsqrt(ms + EPS) * g2_ref[...]
    xn_ref[...] = _pack_halves(xn)
    tm = xn.shape[0]

    hi = xn.astype(BF16)
    lo = (xn - hi.astype(F32)).astype(BF16)
    wrh = wrh_ref[...]
    logits = (jnp.dot(hi, wrh, preferred_element_type=F32)
              + jnp.dot(lo, wrh, preferred_element_type=F32)
              + jnp.dot(hi, wrl_ref[...], preferred_element_type=F32)) + br_ref[...]

    lane = lax.broadcasted_iota(jnp.int32, (tm, LANES), 1)
    neg = -jnp.inf
    lg = jnp.where(lane < N_GROUPS, logits, neg)
    gmax = jnp.max(lg, axis=-1, keepdims=True)
    g_p = 1.0 / jnp.sum(jnp.exp(lg - gmax), axis=-1, keepdims=True)
    g_sel = jnp.min(jnp.where(lg == gmax, lane, LANES), axis=-1, keepdims=True)
    lo_lane = EXP_LANE0 + g_sel * EXP_PER_GROUP
    le = jnp.where((lane >= lo_lane) & (lane < lo_lane + EXP_PER_GROUP), logits, neg)
    m1 = jnp.max(le, axis=-1, keepdims=True)
    i1 = jnp.min(jnp.where(le == m1, lane, LANES), axis=-1, keepdims=True)
    le2 = jnp.where(lane == i1, neg, le)
    m2 = jnp.max(le2, axis=-1, keepdims=True)
    i2 = jnp.min(jnp.where(le2 == m2, lane, LANES), axis=-1, keepdims=True)
    e2 = jnp.exp(m2 - m1)
    gate1 = g_p / (1.0 + e2)
    gate2 = g_p * e2 / (1.0 + e2)

    sel1 = lane == i1
    sel2 = lane == i2
    onehot = jnp.where(sel1 | sel2, 1.0, 0.0)
    r_i = lax.broadcasted_iota(jnp.int32, (tm, tm), 0)
    c_i = lax.broadcasted_iota(jnp.int32, (tm, tm), 1)
    lower = jnp.where(r_i > c_i, 1.0, 0.0).astype(BF16)
    before = jnp.dot(lower, onehot.astype(BF16), preferred_element_type=F32) + base_ref[...]
    rank1 = jnp.sum(jnp.where(sel1, before, 0.0), axis=-1, keepdims=True)
    rank2 = jnp.sum(jnp.where(sel2, before, 0.0), axis=-1, keepdims=True)
    base_ref[...] = base_ref[...] + jnp.sum(onehot, axis=0, keepdims=True)
    cnt_ref[...] = base_ref[...]

    e1f = (i1 - EXP_LANE0).astype(F32)
    e2f = (i2 - EXP_LANE0).astype(F32)
    cols = (e1f, e2f, gate1, gate2, rank1, rank2)
    route = jnp.zeros((tm, LANES), F32)
    for c, val in enumerate(cols):
        route = jnp.where(lane == c, val, route)
    route_ref[...] = route
    routet_ref[...] = route.T[0:8, :]


def _out_route(xa, mga, xb, mgb, wo, ln2_g, wr_hi, wr_lo, br, tm):
    steps_a, steps_b = xa.shape[0] // tm, xb.shape[0] // tm
    t = xa.shape[0] + xb.shape[0]
    const = lambda shape: pl.BlockSpec(shape, lambda i: (0,) * len(shape))
    seg_a = pl.BlockSpec((tm, D_MODEL), lambda i: (jnp.minimum(i, steps_a - 1), 0))
    seg_b = pl.BlockSpec((tm, D_MODEL), lambda i: (jnp.maximum(i - steps_a, 0), 0))
    return pl.pallas_call(
        functools.partial(_out_route_body, steps_a=steps_a),
        grid=(steps_a + steps_b,),
        in_specs=[
            seg_a, seg_a, seg_b, seg_b,
            const((D_MODEL, D_MODEL)), const((1, D_MODEL)),
            const((D_MODEL, LANES)), const((D_MODEL, LANES)), const((1, LANES)),
        ],
        out_specs=[
            pl.BlockSpec((tm, D_MODEL), lambda i: (i, 0)),
            pl.BlockSpec((tm, D_HALF), lambda i: (i, 0)),
            pl.BlockSpec((tm, LANES), lambda i: (i, 0)),
            pl.BlockSpec((8, tm), lambda i: (0, i)),
            const((1, LANES)),
        ],
        out_shape=[
            jax.ShapeDtypeStruct((t, D_MODEL), F32),
            jax.ShapeDtypeStruct((t, D_HALF), U32),
            jax.ShapeDtypeStruct((t, LANES), F32),
            jax.ShapeDtypeStruct((8, t), F32),
            jax.ShapeDtypeStruct((1, LANES), F32),
        ],
        scratch_shapes=[pltpu.VMEM((1, LANES), F32)],
        compiler_params=_cparams(("arbitrary",)),
        name="out_route",
    )(xa, mga, xb, mgb, wo, ln2_g, wr_hi, wr_lo, br)


def _plan_body(cnt_ref, routet_ref, pos_ref, start_ref, blk_exp_ref, blk_src_ref, nvalid_ref):
    n_blocks = blk_exp_ref.shape[0]

    def per_expert(e, acc):
        c = cnt_ref[EXP_LANE0 + e]
        nb = (c + ROW_BLOCK - 1) // ROW_BLOCK
        start_ref[e] = acc * ROW_BLOCK

        def fill(j, carry):
            blk_exp_ref[acc + j] = e
            blk_src_ref[acc + j] = acc + j
            return carry

        lax.fori_loop(0, nb, fill, 0)
        return acc + nb

    nvalid = lax.fori_loop(0, N_EXPERTS, per_expert, 0)
    nvalid_ref[0] = nvalid
    last_exp = blk_exp_ref[nvalid - 1]

    def tail(j, carry):
        blk_exp_ref[j] = last_exp
        blk_src_ref[j] = nvalid - 1
        return carry

    lax.fori_loop(nvalid, n_blocks, tail, 0)

    ef = routet_ref[0:2, :]
    off = jnp.zeros_like(ef)
    for e in range(N_EXPERTS):
        off = jnp.where(ef == float(e), start_ref[e].astype(F32), off)
    pos_ref[...] = (off + routet_ref[4:6, :]).astype(jnp.int32)


def _plan(counts_i32, routet, n_blocks):
    t = routet.shape[1]
    smem = lambda: pl.BlockSpec(memory_space=pltpu.SMEM)
    return pl.pallas_call(
        _plan_body,
        in_specs=[smem(), pl.BlockSpec(memory_space=pltpu.VMEM)],
        out_specs=[pl.BlockSpec(memory_space=pltpu.VMEM), smem(), smem(), smem(), smem()],
        out_shape=[
            jax.ShapeDtypeStruct((2, t), jnp.int32),
            jax.ShapeDtypeStruct((N_EXPERTS,), jnp.int32),
            jax.ShapeDtypeStruct((n_blocks,), jnp.int32),
            jax.ShapeDtypeStruct((n_blocks,), jnp.int32),
            jax.ShapeDtypeStruct((1,), jnp.int32),
        ],
        compiler_params=pltpu.CompilerParams(vmem_limit_bytes=VMEM_LIMIT),
        name="plan",
    )(counts_i32, routet)


def _dispatch_body(pos_ref, cnt_ref, start_ref, nvalid_ref, xn_ref, xs_hbm, zero_ref, sem):
    tm = xn_ref.shape[0]
    n_blocks = xs_hbm.shape[0] // ROW_BLOCK

    def put(src_ref, src_row, dst_row):
        return pltpu.make_async_copy(src_ref.at[pl.ds(src_row, 1)], xs_hbm.at[pl.ds(dst_row, 1)], sem)

    def put_block(blk):
        return pltpu.make_async_copy(zero_ref, xs_hbm.at[pl.ds(blk * ROW_BLOCK, ROW_BLOCK)], sem)

    @pl.when(pl.program_id(0) == 0)
    def _():
        zero_ref[...] = jnp.zeros_like(zero_ref)

        def tail(blk, carry):
            put_block(blk).start()
            return carry

        def tail_done(blk, carry):
            put_block(0).wait()
            return carry

        lax.fori_loop(nvalid_ref[0], n_blocks, tail, 0)

        def pad_rows(wait):
            def per_expert(e, carry):
                c = cnt_ref[EXP_LANE0 + e]
                end = (c + ROW_BLOCK - 1) // ROW_BLOCK * ROW_BLOCK
                base = start_ref[e]

                def fill(r, inner):
                    if wait:
                        put(zero_ref, 0, 0).wait()
                    else:
                        put(zero_ref, 0, base + r).start()
                    return inner

                lax.fori_loop(c, end, fill, 0)
                return carry

            lax.fori_loop(0, N_EXPERTS, per_expert, 0)

        pad_rows(wait=False)
        lax.fori_loop(nvalid_ref[0], n_blocks, tail_done, 0)
        pad_rows(wait=True)

    def issue(t, carry):
        put(xn_ref, t, pos_ref[0, t]).start()
        put(xn_ref, t, pos_ref[1, t]).start()
        return carry

    lax.fori_loop(0, tm, issue, 0, unroll=8)

    def drain(t, carry):
        put(xn_ref, 0, 0).wait()
        put(xn_ref, 0, 0).wait()
        return carry

    lax.fori_loop(0, tm, drain, 0, unroll=8)


def _dispatch(pos_t, counts_i32, starts, nvalid, xn2, n_rows, tm):
    t = xn2.shape[0]
    smem = lambda: pl.BlockSpec(memory_space=pltpu.SMEM)
    return pl.pallas_call(
        _dispatch_body,
        grid=(t // tm,),
        in_specs=[
            pl.BlockSpec((2, tm), lambda i: (0, i), memory_space=pltpu.SMEM),
            smem(), smem(), smem(),
            pl.BlockSpec((tm, D_HALF), lambda i: (i, 0)),
        ],
        out_specs=pl.BlockSpec(memory_space=pl.ANY),
        out_shape=jax.ShapeDtypeStruct((n_rows, D_HALF), U32),
        scratch_shapes=[pltpu.VMEM((ROW_BLOCK, D_HALF), U32), pltpu.SemaphoreType.DMA(())],
        compiler_params=pltpu.CompilerParams(dimension_semantics=("arbitrary",),
                                             vmem_limit_bytes=VMEM_LIMIT, has_side_effects=True),
        name="dispatch",
    )(pos_t, counts_i32, starts, nvalid, xn2)


def _expert_body(blk_exp_ref, blk_src_ref, nvalid_ref, xs_ref, wg_ref, wu_ref, wd_ref, ys_ref):
    valid = pl.program_id(0) < nvalid_ref[0]

    @pl.when(valid)
    def _():
        lo, hi = _unpack_halves(xs_ref[...])
        xb = jnp.concatenate([lo.astype(BF16), hi.astype(BF16)], axis=1)
        g = jnp.dot(xb, wg_ref[0].astype(BF16), preferred_element_type=F32)
        u = jnp.dot(xb, wu_ref[0].astype(BF16), preferred_element_type=F32)
        hmid = (g * jax.nn.sigmoid(g) * u).astype(BF16)
        ys_ref[...] = _pack_halves(jnp.dot(hmid, wd_ref[0].astype(BF16), preferred_element_type=F32))

    @pl.when(jnp.logical_not(valid))
    def _():
        ys_ref[...] = jnp.zeros_like(ys_ref)


def _experts(blk_exp, blk_src, nvalid, xs, wg, wu, wd):
    n_rows = xs.shape[0]
    n_blocks = n_rows // ROW_BLOCK
    grid_spec = pltpu.PrefetchScalarGridSpec(
        num_scalar_prefetch=3,
        grid=(n_blocks,),
        in_specs=[
            pl.BlockSpec((ROW_BLOCK, D_HALF), lambda i, be, bs, nv: (bs[i], 0)),
            pl.BlockSpec((1, D_MODEL, D_EXPERT), lambda i, be, bs, nv: (be[i], 0, 0)),
            pl.BlockSpec((1, D_MODEL, D_EXPERT), lambda i, be, bs, nv: (be[i], 0, 0)),
            pl.BlockSpec((1, D_EXPERT, D_MODEL), lambda i, be, bs, nv: (be[i], 0, 0)),
        ],
        out_specs=pl.BlockSpec((ROW_BLOCK, D_HALF), lambda i, be, bs, nv: (i, 0)),
    )
    return pl.pallas_call(
        _expert_body,
        grid_spec=grid_spec,
        out_shape=jax.ShapeDtypeStruct((n_rows, D_HALF), U32),
        compiler_params=_cparams(("arbitrary",)),
        name="experts",
    )(blk_exp, blk_src, nvalid, xs, wg, wu, wd)


def _combine_body(pos_ref, pos_next_ref, ys_hbm, h_ref, route_ref, ya_ref, yb_ref, buf_ref, sems, *,
                  steps_a):
    n = pl.program_id(0)
    tm = h_ref.shape[0]
    slot = n % 2

    def row_copy(p_ref, k, t, to_slot):
        return pltpu.make_async_copy(ys_hbm.at[pl.ds(p_ref[k, t], 1)],
                                     buf_ref.at[to_slot, k, pl.ds(t, 1)], sems.at[to_slot])

    def gather(p_ref, to_slot):
        def issue(t, carry):
            row_copy(p_ref, 0, t, to_slot).start()
            row_copy(p_ref, 1, t, to_slot).start()
            return carry

        lax.fori_loop(0, tm, issue, 0, unroll=8)

    @pl.when(n == 0)
    def _():
        gather(pos_ref, 0)

    @pl.when(n + 1 < pl.num_programs(0))
    def _():
        gather(pos_next_ref, 1 - slot)

    def drain(t, carry):
        row_copy(pos_ref, 0, t, slot).wait()
        row_copy(pos_ref, 1, t, slot).wait()
        return carry

    lax.fori_loop(0, tm, drain, 0, unroll=8)
    route = route_ref[...]
    lo1, hi1 = _unpack_halves(buf_ref[slot, 0])
    lo2, hi2 = _unpack_halves(buf_ref[slot, 1])
    g1, g2 = route[:, 2:3], route[:, 3:4]
    y_lo = h_ref[:, :D_HALF] + g1 * lo1 + g2 * lo2
    y_hi = h_ref[:, D_HALF:] + g1 * hi1 + g2 * hi2

    @pl.when(n < steps_a)
    def _():
        ya_ref[:, :D_HALF] = y_lo
        ya_ref[:, D_HALF:] = y_hi

    @pl.when(n >= steps_a)
    def _():
        yb_ref[:, :D_HALF] = y_lo
        yb_ref[:, D_HALF:] = y_hi


def _combine(pos_t, ys, h, route, t_a, tm):
    t = h.shape[0]
    nt = t // tm
    steps_a = t_a // tm
    return pl.pallas_call(
        functools.partial(_combine_body, steps_a=steps_a),
        grid=(nt,),
        in_specs=[
            pl.BlockSpec((2, tm), lambda i: (0, i), memory_space=pltpu.SMEM),
            pl.BlockSpec((2, tm), lambda i: (0, jnp.minimum(i + 1, nt - 1)), memory_space=pltpu.SMEM),
            pl.BlockSpec(memory_space=pl.ANY),
            pl.BlockSpec((tm, D_MODEL), lambda i: (i, 0)),
            pl.BlockSpec((tm, LANES), lambda i: (i, 0)),
        ],
        out_specs=[
            pl.BlockSpec((tm, D_MODEL), lambda i: (jnp.minimum(i, steps_a - 1), 0)),
            pl.BlockSpec((tm, D_MODEL), lambda i: (jnp.maximum(i - steps_a, 0), 0)),
        ],
        out_shape=[
            jax.ShapeDtypeStruct((t_a, D_MODEL), F32),
            jax.ShapeDtypeStruct((t - t_a, D_MODEL), F32),
        ],
        scratch_shapes=[pltpu.VMEM((2, 2, tm, D_HALF), U32), pltpu.SemaphoreType.DMA((2,))],
        compiler_params=_cparams(("arbitrary",)),
        name="combine",
    )(pos_t, pos_t, ys, h, route)


def _pad_rope(a, axis):
    x1, x2 = jnp.split(a, 2, axis=axis)
    z = jnp.zeros_like(x1)
    return jnp.concatenate([x1, z, x2, z], axis=axis)


def _prepare(ln1_g, w_in, conv_w, q_a_norm_g, w_uq, kv_a_norm_g, w_ukv, q_norm_g, k_norm_g,
             w_conv_out, w_attn_out, w_out, ln2_g, w_router_group, b_router_group,
             w_router_exp, b_router_exp, w_gate, w_up, w_down):
    w_in0 = w_in[0]
    w_main = jnp.concatenate([w_in0[:, :KPE_OFF], w_in0[:, KPE_OFF + QK_ROPE:]], axis=1).astype(BF16)
    w_pe = _pad_rope(w_in0[:, KPE_OFF:KPE_OFF + QK_ROPE], 1).astype(BF16)

    wq = w_uq[0].reshape(Q_LORA, N_HEADS, QK_DIM)
    wq = jnp.concatenate([wq[:, :, :QK_NOPE], _pad_rope(wq[:, :, QK_NOPE:], 2)], axis=2)
    wq_t = wq.reshape(Q_LORA, N_HEADS * QK_PAD).T.astype(BF16)
    wkv = w_ukv[0].reshape(KV_LORA, N_HEADS, QK_NOPE + V_DIM)
    wkn = wkv[:, :, :QK_NOPE].reshape(KV_LORA, N_HEADS * QK_NOPE).astype(BF16)
    wv_t = wkv[:, :, QK_NOPE:].reshape(KV_LORA, N_HEADS * V_DIM).T.astype(BF16)

    qg = q_norm_g[0]
    score_scale = QK_DIM ** -0.5 * math.log2(math.e)
    gq = (jnp.concatenate([qg[:QK_NOPE], _pad_rope(qg[QK_NOPE:], 0)]) * score_scale).reshape(QK_PAD, 1)
    kg = k_norm_g[0]
    gkn = kg[:QK_NOPE].reshape(1, LANES)
    gkr = _pad_rope(kg[QK_NOPE:], 0).reshape(1, LANES)

    wr = jnp.concatenate([w_router_group[0], w_router_exp[0],
                          jnp.zeros((D_MODEL, LANES - N_GROUPS - N_EXPERTS), F32)], axis=1)
    wr_hi = wr.astype(BF16)
    wr_lo = (wr - wr_hi.astype(F32)).astype(BF16)
    br = jnp.concatenate([b_router_group[0], b_router_exp[0],
                          jnp.zeros((LANES - N_GROUPS - N_EXPERTS,), F32)]).reshape(1, LANES)
    return dict(
        ln1_g=ln1_g, w_main=w_main, w_pe=w_pe, conv_w=conv_w[0],
        qkv=(q_a_norm_g, kv_a_norm_g, gq, gkn, gkr, wq_t, wkn, wv_t),
        wc=w_conv_out[0].astype(BF16), wa=w_attn_out[0].astype(BF16), wo=w_out[0].astype(BF16),
        ln2_g=ln2_g, wr_hi=wr_hi, wr_lo=wr_lo, br=br,
        wg=w_gate[0], wu=w_up[0], wd=w_down[0],
    )


def _rope_tables(s):
    inv = ROPE_THETA ** (-jnp.arange(0, QK_ROPE, 2, dtype=F32) / QK_ROPE)
    ang = jnp.arange(s, dtype=F32)[:, None] * inv[None, :]
    cos, sin = jnp.cos(ang), jnp.sin(ang)
    z = jnp.zeros_like(cos)
    cos_k = jnp.concatenate([cos, z, cos, z], axis=1)
    sin_k = jnp.concatenate([-sin, z, sin, z], axis=1)
    return cos_k, sin_k, cos.T, sin.T


def _tiles(b, s):
    t = b * s
    pick = lambda n, pref: pref if n % pref == 0 else n
    return dict(
        in_tm=pick(t, 512), in_tn=2048,
        qkv_tm=pick(s, 256),
        attn_tq=pick(s, 512), attn_kc=pick(s, 256),
        merge_tm=pick(s, 256), halo=16,
    )


def _moe_tiles(t_a, t_b):
    both = math.gcd(t_a, t_b)
    pick = lambda pref: pref if both % pref == 0 else both
    return dict(route_tm=pick(512), disp_tm=pick(512), comb_tm=pick(256))


def _mixer(x, p):
    b, s, _ = x.shape
    t = b * s
    tl = _tiles(b, s)
    x2d = x.reshape(t, D_MODEL)
    proj, kpe = _in_proj(x2d, p["ln1_g"], p["w_main"], p["w_pe"], tl["in_tm"], tl["in_tn"])
    qt, k, vt = _qkv(proj, kpe, _rope_tables(s), p["qkv"], b, s, tl["qkv_tm"])
    o = _attention(qt, k, vt, tl["attn_tq"], tl["attn_kc"])
    merged = _merge(proj, o.reshape(t, D_MODEL), p["conv_w"], p["wc"], p["wa"], b, s,
                    tl["merge_tm"], tl["halo"])
    return x2d, merged


def _forward(x_a, x_b, p):
    xa, mga = _mixer(x_a, p)
    xb, mgb = _mixer(x_b, p)
    t_a, t_b = xa.shape[0], xb.shape[0]
    t = t_a + t_b
    tl = _moe_tiles(t_a, t_b)
    h, xn2, route, route_t, counts = _out_route(xa, mga, xb, mgb, p["wo"], p["ln2_g"], p["wr_hi"],
                                                p["wr_lo"], p["br"], tl["route_tm"])
    n_blocks = 2 * t // ROW_BLOCK + N_EXPERTS
    counts_i32 = counts.reshape(LANES).astype(jnp.int32)
    pos_t, starts, blk_exp, blk_src, nvalid = _plan(counts_i32, route_t, n_blocks)
    xs = _dispatch(pos_t, counts_i32, starts, nvalid, xn2, n_blocks * ROW_BLOCK, tl["disp_tm"])
    ys = _experts(blk_exp, blk_src, nvalid, xs, p["wg"], p["wu"], p["wd"])
    y_a, y_b = _combine(pos_t, ys, h, route, t_a, tl["comb_tm"])
    return y_a.reshape(x_a.shape), y_b.reshape(x_b.shape)


def kernel(x_prompt, x_sample, ln1_g, w_in, conv_w, q_a_norm_g, w_uq, kv_a_norm_g, w_ukv, q_norm_g,
           k_norm_g, w_conv_out, w_attn_out, w_out, ln2_g, w_router_group, b_router_group,
           w_router_exp, b_router_exp, w_gate, w_up, w_down):
    p = _prepare(ln1_g, w_in, conv_w, q_a_norm_g, w_uq, kv_a_norm_g, w_ukv, q_norm_g, k_norm_g,
                 w_conv_out, w_attn_out, w_out, ln2_g, w_router_group, b_router_group,
                 w_router_exp, b_router_exp, w_gate, w_up, w_down)
    return _forward(x_prompt, x_sample, p)
```

```python
import functools
import math

import jax
import jax.numpy as jnp
from jax import lax
from jax.experimental import pallas as pl
from jax.experimental.pallas import tpu as pltpu

F32 = jnp.float32
BF16 = jnp.bfloat16

D_MODEL = 2048
CONV_CH = 1024
N_HEADS = 16
QK_NOPE = 128
QK_ROPE = 64
HALF_ROPE = QK_ROPE // 2
QK_DIM = QK_NOPE + QK_ROPE
QK_PAD = 256
V_DIM = 128
Q_LORA = 512
KV_LORA = 512
ROPE_THETA = 10000.0
N_GROUPS = 8
EXP_PER_GROUP = 8
N_EXPERTS = 64
D_EXPERT = 512
EPS = 1e-6
PROJ_W = 8192
KPE_OFF = 4096
LANES = 128
EXP_LANE0 = N_GROUPS
ROW_BLOCK = 256
VMEM_LIMIT = 56 * 1024 * 1024

_NT = (((1,), (1,)), ((), ()))


def _cparams(sem):
    return pltpu.CompilerParams(dimension_semantics=sem, vmem_limit_bytes=VMEM_LIMIT)


D_HALF = D_MODEL // 2
U32 = jnp.uint32


def _pack_halves(x):
    lo = lax.bitcast_convert_type(x[:, :D_HALF].astype(BF16).astype(F32), U32)
    hi = lax.bitcast_convert_type(x[:, D_HALF:].astype(BF16).astype(F32), U32)
    return (lo >> 16) | hi


def _unpack_halves(w):
    lo = lax.bitcast_convert_type(w << 16, F32)
    hi = lax.bitcast_convert_type(w & jnp.uint32(0xFFFF0000), F32)
    return lo, hi


def _inproj_body(x_ref, g_ref, w_ref, wpe_ref, out_ref, kpe_ref, xn_ref):
    @pl.when(pl.program_id(1) == 0)
    def _():
        x = x_ref[...]
        ms = jnp.mean(x * x, axis=-1, keepdims=True)
        xn = (x * lax.rsqrt(ms + EPS) * g_ref[...]).astype(BF16)
        xn_ref[...] = xn
        kpe_ref[...] = jnp.dot(xn, wpe_ref[...], preferred_element_type=F32)

    out_ref[...] = jnp.dot(xn_ref[...], w_ref[...], preferred_element_type=F32).astype(BF16)


def _in_proj(x2d, ln1_g, w_main, w_pe, tm, tn):
    t = x2d.shape[0]
    return pl.pallas_call(
        _inproj_body,
        grid=(t // tm, PROJ_W // tn),
        in_specs=[
            pl.BlockSpec((tm, D_MODEL), lambda i, j: (i, 0)),
            pl.BlockSpec((1, D_MODEL), lambda i, j: (0, 0)),
            pl.BlockSpec((D_MODEL, tn), lambda i, j: (0, j)),
            pl.BlockSpec((D_MODEL, LANES), lambda i, j: (0, 0)),
        ],
        out_specs=[
            pl.BlockSpec((tm, tn), lambda i, j: (i, j)),
            pl.BlockSpec((tm, LANES), lambda i, j: (i, 0)),
        ],
        out_shape=[
            jax.ShapeDtypeStruct((t, PROJ_W), BF16),
            jax.ShapeDtypeStruct((t, LANES), F32),
        ],
        scratch_shapes=[pltpu.VMEM((tm, D_MODEL), BF16)],
        compiler_params=_cparams(("arbitrary", "arbitrary")),
        name="in_proj",
    )(x2d, ln1_g, w_main, w_pe)


def _qkv_body(ql_ref, kvl_ref, kpe_ref, cosk_ref, sink_ref, cost_ref, sint_ref,
              gqa_ref, gkva_ref, gq_ref, gkn_ref, gkr_ref, wq_ref, wkn_ref, wv_ref,
              qt_ref, k_ref, vt_ref):
    def latent_norm(ref, g_ref):
        v = ref[...].astype(F32)
        ms = jnp.mean(v * v, axis=-1, keepdims=True)
        return (v * lax.rsqrt(ms + EPS) * g_ref[...]).astype(BF16)

    qn = latent_norm(ql_ref, gqa_ref)
    kvn = latent_norm(kvl_ref, gkva_ref)
    tm = qn.shape[0]

    kn = jnp.dot(kvn, wkn_ref[...], preferred_element_type=F32)
    kpe = kpe_ref[...]
    ss_pe = jnp.sum(kpe * kpe, axis=-1, keepdims=True)
    kr = kpe * gkr_ref[...]
    kr = kr * cosk_ref[...] + pltpu.roll(kr, 2 * HALF_ROPE, axis=1) * sink_ref[...]
    gkn = gkn_ref[...]
    for h in range(N_HEADS):
        kh = kn[:, h * QK_NOPE:(h + 1) * QK_NOPE]
        ss = jnp.sum(kh * kh, axis=-1, keepdims=True) + ss_pe
        r = lax.rsqrt(ss * (1.0 / QK_DIM) + EPS)
        k_ref[0, h, :, 0:QK_NOPE] = (kh * gkn * r).astype(BF16)
        k_ref[0, h, :, QK_NOPE:QK_PAD] = (kr * r).astype(BF16)

    vt = lax.dot_general(wv_ref[...], kvn, _NT, preferred_element_type=F32)
    for h in range(N_HEADS):
        vt_ref[0, h] = vt[h * V_DIM:(h + 1) * V_DIM, :].astype(BF16)

    cost = cost_ref[...]
    sint = sint_ref[...]
    gq = gq_ref[...]
    zeros = jnp.zeros((HALF_ROPE, tm), BF16)
    for h in range(N_HEADS):
        qt = lax.dot_general(wq_ref[h * QK_PAD:(h + 1) * QK_PAD, :], qn, _NT,
                             preferred_element_type=F32)
        ss = jnp.sum(qt * qt, axis=0, keepdims=True)
        r = lax.rsqrt(ss * (1.0 / QK_DIM) + EPS)
        qs = qt * gq * r
        x1 = qs[QK_NOPE:QK_NOPE + HALF_ROPE]
        x2 = qs[QK_NOPE + 2 * HALF_ROPE:QK_NOPE + 3 * HALF_ROPE]
        qt_ref[0, h, 0:QK_NOPE, :] = qs[0:QK_NOPE].astype(BF16)
        qt_ref[0, h, QK_NOPE:QK_NOPE + HALF_ROPE, :] = (x1 * cost - x2 * sint).astype(BF16)
        qt_ref[0, h, QK_NOPE + HALF_ROPE:QK_NOPE + 2 * HALF_ROPE, :] = zeros
        qt_ref[0, h, QK_NOPE + 2 * HALF_ROPE:QK_NOPE + 3 * HALF_ROPE, :] = (
            x1 * sint + x2 * cost).astype(BF16)
        qt_ref[0, h, QK_NOPE + 3 * HALF_ROPE:QK_PAD, :] = zeros


def _qkv(proj, kpe, tabs, wts, b, s, tm):
    ns = s // tm
    cos_k, sin_k, cos_t, sin_t = tabs
    gqa, gkva, gq, gkn, gkr, wq_t, wkn, wv_t = wts
    const = lambda shape: pl.BlockSpec(shape, lambda bi, i: (0,) * len(shape))
    return pl.pallas_call(
        _qkv_body,
        grid=(b, ns),
        in_specs=[
            pl.BlockSpec((tm, Q_LORA), lambda bi, i: (bi * ns + i, 3 * CONV_CH // Q_LORA)),
            pl.BlockSpec((tm, KV_LORA), lambda bi, i: (bi * ns + i, 3 * CONV_CH // KV_LORA + 1)),
            pl.BlockSpec((tm, LANES), lambda bi, i: (bi * ns + i, 0)),
            pl.BlockSpec((tm, LANES), lambda bi, i: (i, 0)),
            pl.BlockSpec((tm, LANES), lambda bi, i: (i, 0)),
            pl.BlockSpec((HALF_ROPE, tm), lambda bi, i: (0, i)),
            pl.BlockSpec((HALF_ROPE, tm), lambda bi, i: (0, i)),
            const((1, Q_LORA)), const((1, KV_LORA)), const((QK_PAD, 1)),
            const((1, LANES)), const((1, LANES)),
            const((N_HEADS * QK_PAD, Q_LORA)),
            const((KV_LORA, N_HEADS * QK_NOPE)),
            const((N_HEADS * V_DIM, KV_LORA)),
        ],
        out_specs=[
            pl.BlockSpec((1, N_HEADS, QK_PAD, tm), lambda bi, i: (bi, 0, 0, i)),
            pl.BlockSpec((1, N_HEADS, tm, QK_PAD), lambda bi, i: (bi, 0, i, 0)),
            pl.BlockSpec((1, N_HEADS, V_DIM, tm), lambda bi, i: (bi, 0, 0, i)),
        ],
        out_shape=[
            jax.ShapeDtypeStruct((b, N_HEADS, QK_PAD, s), BF16),
            jax.ShapeDtypeStruct((b, N_HEADS, s, QK_PAD), BF16),
            jax.ShapeDtypeStruct((b, N_HEADS, V_DIM, s), BF16),
        ],
        compiler_params=_cparams(("arbitrary", "arbitrary")),
        name="qkv",
    )(proj, proj, kpe, cos_k, sin_k, cos_t, sin_t, gqa, gkva, gq, gkn, gkr, wq_t, wkn, wv_t)


def _attn_body(qt_ref, k_ref, vt_ref, o_ref, s_a, m_a, s_b, m_b, *, kc):
    n = pl.program_id(0)
    s_len = k_ref.shape[2]
    chunks = [(c * kc, (c + 1) * kc) for c in range(s_len // kc)]

    @pl.when(n == 0)
    def _():
        s_b[...] = jnp.zeros_like(s_b)
        m_b[...] = jnp.zeros_like(m_b)

    def step(s_w, m_w, s_r, m_r):
        m_prev = m_r[...]
        qt = qt_ref[0, 0]
        l = None
        acc = None
        m = None
        for lo, hi in chunks:
            p = jnp.exp2(s_r[lo:hi, :] - m_prev)
            lc = jnp.sum(p, axis=0, keepdims=True)
            l = lc if l is None else l + lc
            pv = jnp.dot(vt_ref[0, 0, :, lo:hi], p.astype(BF16), preferred_element_type=F32)
            acc = pv if acc is None else acc + pv

            sc = jnp.dot(k_ref[0, 0, lo:hi, :], qt, preferred_element_type=F32)
            s_w[lo:hi, :] = sc
            mc = jnp.max(sc, axis=0, keepdims=True)
            m = mc if m is None else jnp.maximum(m, mc)
        o_ref[0] = (acc * (1.0 / l)).T.astype(BF16)
        m_w[...] = m

    @pl.when(n % 2 == 0)
    def _():
        step(s_a, m_a, s_b, m_b)

    @pl.when(n % 2 == 1)
    def _():
        step(s_b, m_b, s_a, m_a)


def _attention(qt, k, vt, tq, kc):
    b, _, _, s = qt.shape
    nq = s // tq
    n_tiles = b * N_HEADS * nq

    def bhi(tile):
        return tile // (N_HEADS * nq), (tile // nq) % N_HEADS, tile % nq

    def score_tile(n):
        return bhi(jnp.minimum(n, n_tiles - 1))

    def value_tile(n):
        return bhi(jnp.maximum(n - 1, 0))

    def qt_map(n):
        bi, h, i = score_tile(n)
        return bi, h, 0, i

    def k_map(n):
        bi, h, _ = score_tile(n)
        return bi, h, 0, 0

    def vt_map(n):
        bi, h, _ = value_tile(n)
        return bi, h, 0, 0

    def o_map(n):
        bi, h, i = value_tile(n)
        return bi, i, h

    return pl.pallas_call(
        functools.partial(_attn_body, kc=kc),
        grid=(n_tiles + 1,),
        in_specs=[
            pl.BlockSpec((1, 1, QK_PAD, tq), qt_map),
            pl.BlockSpec((1, 1, s, QK_PAD), k_map),
            pl.BlockSpec((1, 1, V_DIM, s), vt_map),
        ],
        out_specs=pl.BlockSpec((1, tq, V_DIM), o_map),
        out_shape=jax.ShapeDtypeStruct((b, s, N_HEADS * V_DIM), BF16),
        scratch_shapes=[pltpu.VMEM((s, tq), F32), pltpu.VMEM((1, tq), F32),
                        pltpu.VMEM((s, tq), F32), pltpu.VMEM((1, tq), F32)],
        compiler_params=_cparams(("arbitrary",)),
        name="attention",
    )(qt, k, vt)


def _merge_body(u_ref, gb_ref, gc_ref, up_ref, gcp_ref, un_ref, gcn_ref, o_ref, gtc_ref, gta_ref,
                cw_ref, wc_ref, wa_ref, out_ref):
    i = pl.program_id(1)
    last = pl.num_programs(1) - 1
    halo = up_ref.shape[0]
    v = gc_ref[...].astype(F32) * u_ref[...].astype(F32)
    tm = v.shape[0]
    v_before = gcp_ref[halo - 1:halo, :].astype(F32) * up_ref[halo - 1:halo, :].astype(F32)
    v_after = gcn_ref[0:1, :].astype(F32) * un_ref[0:1, :].astype(F32)
    v_before = jnp.where(i == 0, 0.0, v_before)
    v_after = jnp.where(i == last, 0.0, v_after)
    row = lax.broadcasted_iota(jnp.int32, (tm, 1), 0)
    v_prev = jnp.where(row == 0, v_before, pltpu.roll(v, 1, axis=0))
    v_next = jnp.where(row == tm - 1, v_after, pltpu.roll(v, tm - 1, axis=0))
    cw = cw_ref[...]
    conv = v_prev * cw[0:1, :] + v * cw[1:2, :] + v_next * cw[2:3, :]
    z = (gb_ref[...].astype(F32) * conv).astype(BF16)
    conv_out = jnp.dot(z, wc_ref[...], preferred_element_type=F32)
    attn_out = jnp.dot(o_ref[...], wa_ref[...], preferred_element_type=F32)
    merged = (jax.nn.sigmoid(gtc_ref[...].astype(F32)) * conv_out
              + jax.nn.sigmoid(gta_ref[...].astype(F32)) * attn_out)
    out_ref[...] = merged.astype(BF16)


def _merge(proj, o2d, conv_w, wc, wa, b, s, tm, halo):
    ns = s // tm
    t = b * s
    hb = tm // halo
    nh = t // halo
    row = lambda bi, i: bi * ns + i
    prev = lambda bi, i: (jnp.maximum(row(bi, i) * hb - 1, 0))
    nxt = lambda bi, i: (jnp.minimum((row(bi, i) + 1) * hb, nh - 1))
    gate0 = KPE_OFF // D_MODEL
    return pl.pallas_call(
        _merge_body,
        grid=(b, ns),
        in_specs=[
            pl.BlockSpec((tm, CONV_CH), lambda bi, i: (row(bi, i), 0)),
            pl.BlockSpec((tm, CONV_CH), lambda bi, i: (row(bi, i), 1)),
            pl.BlockSpec((tm, CONV_CH), lambda bi, i: (row(bi, i), 2)),
            pl.BlockSpec((halo, CONV_CH), lambda bi, i: (prev(bi, i), 0)),
            pl.BlockSpec((halo, CONV_CH), lambda bi, i: (prev(bi, i), 2)),
            pl.BlockSpec((halo, CONV_CH), lambda bi, i: (nxt(bi, i), 0)),
            pl.BlockSpec((halo, CONV_CH), lambda bi, i: (nxt(bi, i), 2)),
            pl.BlockSpec((tm, D_MODEL), lambda bi, i: (row(bi, i), 0)),
            pl.BlockSpec((tm, D_MODEL), lambda bi, i: (row(bi, i), gate0)),
            pl.BlockSpec((tm, D_MODEL), lambda bi, i: (row(bi, i), gate0 + 1)),
            pl.BlockSpec((3, CONV_CH), lambda bi, i: (0, 0)),
            pl.BlockSpec((CONV_CH, D_MODEL), lambda bi, i: (0, 0)),
            pl.BlockSpec((D_MODEL, D_MODEL), lambda bi, i: (0, 0)),
        ],
        out_specs=pl.BlockSpec((tm, D_MODEL), lambda bi, i: (row(bi, i), 0)),
        out_shape=jax.ShapeDtypeStruct((t, D_MODEL), BF16),
        compiler_params=_cparams(("arbitrary", "arbitrary")),
        name="merge",
    )(proj, proj, proj, proj, proj, proj, proj, o2d, proj, proj, conv_w, wc, wa)


def _out_route_body(xa_ref, mga_ref, xb_ref, mgb_ref, wo_ref, g2_ref, wrh_ref, wrl_ref, br_ref,
                    h_ref, xn_ref, route_ref, routet_ref, cnt_ref, base_ref, *, steps_a):
    step = pl.program_id(0)

    @pl.when(step == 0)
    def _():
        base_ref[...] = jnp.zeros_like(base_ref)

    in_a = step < steps_a
    x = jnp.where(in_a, xa_ref[...], xb_ref[...])
    mg = jnp.where(in_a, mga_ref[...], mgb_ref[...])
    h = x + jnp.dot(mg, wo_ref[...], preferred_element_type=F32)
    h_ref[...] = h
    ms = jnp.mean(h * h, axis=-1, keepdims=True)
    xn = h * lax.rsqrt(ms + EPS) * g2_ref[...]
    xn_ref[...] = _pack_halves(xn)
    tm = xn.shape[0]

    hi = xn.astype(BF16)
    lo = (xn - hi.astype(F32)).astype(BF16)
    wrh = wrh_ref[...]
    logits = (jnp.dot(hi, wrh, preferred_element_type=F32)
              + jnp.dot(lo, wrh, preferred_element_type=F32)
              + jnp.dot(hi, wrl_ref[...], preferred_element_type=F32)) + br_ref[...]

    lane = lax.broadcasted_iota(jnp.int32, (tm, LANES), 1)
    neg = -jnp.inf
    lg = jnp.where(lane < N_GROUPS, logits, neg)
    gmax = jnp.max(lg, axis=-1, keepdims=True)
    g_p = 1.0 / jnp.sum(jnp.exp(lg - gmax), axis=-1, keepdims=True)
    g_sel = jnp.min(jnp.where(lg == gmax, lane, LANES), axis=-1, keepdims=True)
    lo_lane = EXP_LANE0 + g_sel * EXP_PER_GROUP
    le = jnp.where((lane >= lo_lane) & (lane < lo_lane + EXP_PER_GROUP), logits, neg)
    m1 = jnp.max(le, axis=-1, keepdims=True)
    i1 = jnp.min(jnp.where(le == m1, lane, LANES), axis=-1, keepdims=True)
    le2 = jnp.where(lane == i1, neg, le)
    m2 = jnp.max(le2, axis=-1, keepdims=True)
    i2 = jnp.min(jnp.where(le2 == m2, lane, LANES), axis=-1, keepdims=True)
    e2 = jnp.exp(m2 - m1)
    gate1 = g_p / (1.0 + e2)
    gate2 = g_p * e2 / (1.0 + e2)

    sel1 = lane == i1
    sel2 = lane == i2
    onehot = jnp.where(sel1 | sel2, 1.0, 0.0)
    r_i = lax.broadcasted_iota(jnp.int32, (tm, tm), 0)
    c_i = lax.broadcasted_iota(jnp.int32, (tm, tm), 1)
    lower = jnp.where(r_i > c_i, 1.0, 0.0).astype(BF16)
    before = jnp.dot(lower, onehot.astype(BF16), preferred_element_type=F32) + base_ref[...]
    rank1 = jnp.sum(jnp.where(sel1, before, 0.0), axis=-1, keepdims=True)
    rank2 = jnp.sum(jnp.where(sel2, before, 0.0), axis=-1, keepdims=True)
    base_ref[...] = base_ref[...] + jnp.sum(onehot, axis=0, keepdims=True)
    cnt_ref[...] = base_ref[...]

    e1f = (i1 - EXP_LANE0).astype(F32)
    e2f = (i2 - EXP_LANE0).astype(F32)
    cols = (e1f, e2f, gate1, gate2, rank1, rank2)
    route = jnp.zeros((tm, LANES), F32)
    for c, val in enumerate(cols):
        route = jnp.where(lane == c, val, route)
    route_ref[...] = route
    routet_ref[...] = route.T[0:8, :]


def _out_route(xa, mga, xb, mgb, wo, ln2_g, wr_hi, wr_lo, br, tm):
    steps_a, steps_b = xa.shape[0] // tm, xb.shape[0] // tm
    t = xa.shape[0] + xb.shape[0]
    const = lambda shape: pl.BlockSpec(shape, lambda i: (0,) * len(shape))
    seg_a = pl.BlockSpec((tm, D_MODEL), lambda i: (jnp.minimum(i, steps_a - 1), 0))
    seg_b = pl.BlockSpec((tm, D_MODEL), lambda i: (jnp.maximum(i - steps_a, 0), 0))
    return pl.pallas_call(
        functools.partial(_out_route_body, steps_a=steps_a),
        grid=(steps_a + steps_b,),
        in_specs=[
            seg_a, seg_a, seg_b, seg_b,
            const((D_MODEL, D_MODEL)), const((1, D_MODEL)),
            const((D_MODEL, LANES)), const((D_MODEL, LANES)), const((1, LANES)),
        ],
        out_specs=[
            pl.BlockSpec((tm, D_MODEL), lambda i: (i, 0)),
            pl.BlockSpec((tm, D_HALF), lambda i: (i, 0)),
            pl.BlockSpec((tm, LANES), lambda i: (i, 0)),
            pl.BlockSpec((8, tm), lambda i: (0, i)),
            const((1, LANES)),
        ],
        out_shape=[
            jax.ShapeDtypeStruct((t, D_MODEL), F32),
            jax.ShapeDtypeStruct((t, D_HALF), U32),
            jax.ShapeDtypeStruct((t, LANES), F32),
            jax.ShapeDtypeStruct((8, t), F32),
            jax.ShapeDtypeStruct((1, LANES), F32),
        ],
        scratch_shapes=[pltpu.VMEM((1, LANES), F32)],
        compiler_params=_cparams(("arbitrary",)),
        name="out_route",
    )(xa, mga, xb, mgb, wo, ln2_g, wr_hi, wr_lo, br)


def _plan_body(cnt_ref, routet_ref, pos_ref, start_ref, blk_exp_ref, blk_src_ref, nvalid_ref):
    n_blocks = blk_exp_ref.shape[0]

    def per_expert(e, acc):
        c = cnt_ref[EXP_LANE0 + e]
        nb = (c + ROW_BLOCK - 1) // ROW_BLOCK
        start_ref[e] = acc * ROW_BLOCK

        def fill(j, carry):
            blk_exp_ref[acc + j] = e
            blk_src_ref[acc + j] = acc + j
            return carry

        lax.fori_loop(0, nb, fill, 0)
        return acc + nb

    nvalid = lax.fori_loop(0, N_EXPERTS, per_expert, 0)
    nvalid_ref[0] = nvalid
    last_exp = blk_exp_ref[nvalid - 1]

    def tail(j, carry):
        blk_exp_ref[j] = last_exp
        blk_src_ref[j] = nvalid - 1
        return carry

    lax.fori_loop(nvalid, n_blocks, tail, 0)

    ef = routet_ref[0:2, :]
    off = jnp.zeros_like(ef)
    for e in range(N_EXPERTS):
        off = jnp.where(ef == float(e), start_ref[e].astype(F32), off)
    pos_ref[...] = (off + routet_ref[4:6, :]).astype(jnp.int32)


def _plan(counts_i32, routet, n_blocks):
    t = routet.shape[1]
    smem = lambda: pl.BlockSpec(memory_space=pltpu.SMEM)
    return pl.pallas_call(
        _plan_body,
        in_specs=[smem(), pl.BlockSpec(memory_space=pltpu.VMEM)],
        out_specs=[pl.BlockSpec(memory_space=pltpu.VMEM), smem(), smem(), smem(), smem()],
        out_shape=[
            jax.ShapeDtypeStruct((2, t), jnp.int32),
            jax.ShapeDtypeStruct((N_EXPERTS,), jnp.int32),
            jax.ShapeDtypeStruct((n_blocks,), jnp.int32),
            jax.ShapeDtypeStruct((n_blocks,), jnp.int32),
            jax.ShapeDtypeStruct((1,), jnp.int32),
        ],
        compiler_params=pltpu.CompilerParams(vmem_limit_bytes=VMEM_LIMIT),
        name="plan",
    )(counts_i32, routet)


def _dispatch_body(pos0_ref, pos1_ref, cnt_ref, start_ref, nvalid_ref, xn_ref, xs_hbm, zero_ref, sem):
    tm = xn_ref.shape[0]
    n_blocks = xs_hbm.shape[0] // ROW_BLOCK

    def put(src_ref, src_row, dst_row):
        return pltpu.make_async_copy(src_ref.at[pl.ds(src_row, 1)], xs_hbm.at[pl.ds(dst_row, 1)], sem)

    def put_block(blk):
        return pltpu.make_async_copy(zero_ref, xs_hbm.at[pl.ds(blk * ROW_BLOCK, ROW_BLOCK)], sem)

    @pl.when(pl.program_id(0) == 0)
    def _():
        zero_ref[...] = jnp.zeros_like(zero_ref)

        def tail(blk, carry):
            put_block(blk).start()
            return carry

        def tail_done(blk, carry):
            put_block(0).wait()
            return carry

        lax.fori_loop(nvalid_ref[0], n_blocks, tail, 0)

        def pad_rows(wait):
            def per_expert(e, carry):
                c = cnt_ref[EXP_LANE0 + e]
                end = (c + ROW_BLOCK - 1) // ROW_BLOCK * ROW_BLOCK
                base = start_ref[e]

                def fill(r, inner):
                    if wait:
                        put(zero_ref, 0, 0).wait()
                    else:
                        put(zero_ref, 0, base + r).start()
                    return inner

                lax.fori_loop(c, end, fill, 0)
                return carry

            lax.fori_loop(0, N_EXPERTS, per_expert, 0)

        pad_rows(wait=False)
        lax.fori_loop(nvalid_ref[0], n_blocks, tail_done, 0)
        pad_rows(wait=True)

    for t in range(tm):
        put(xn_ref, t, pos0_ref[t]).start()
        put(xn_ref, t, pos1_ref[t]).start()

    for t in range(2 * tm):
        put(xn_ref, 0, 0).wait()


def _dispatch(pos0, pos1, counts_i32, starts, nvalid, xn2, n_rows, tm):
    t = xn2.shape[0]
    smem = lambda: pl.BlockSpec(memory_space=pltpu.SMEM)
    pos_spec = lambda: pl.BlockSpec((tm,), lambda i: (i,), memory_space=pltpu.SMEM)
    return pl.pallas_call(
        _dispatch_body,
        grid=(t // tm,),
        in_specs=[
            pos_spec(), pos_spec(),
            smem(), smem(), smem(),
            pl.BlockSpec((tm, D_HALF), lambda i: (i, 0)),
        ],
        out_specs=pl.BlockSpec(memory_space=pl.ANY),
        out_shape=jax.ShapeDtypeStruct((n_rows, D_HALF), U32),
        scratch_shapes=[pltpu.VMEM((ROW_BLOCK, D_HALF), U32), pltpu.SemaphoreType.DMA(())],
        compiler_params=pltpu.CompilerParams(dimension_semantics=("arbitrary",),
                                             vmem_limit_bytes=VMEM_LIMIT, has_side_effects=True),
        name="dispatch",
    )(pos0, pos1, counts_i32, starts, nvalid, xn2)


def _expert_body(blk_exp_ref, blk_src_ref, nvalid_ref, xs_ref, wg_ref, wu_ref, wd_ref, ys_ref):
    valid = pl.program_id(0) < nvalid_ref[0]

    @pl.when(valid)
    def _():
        lo, hi = _unpack_halves(xs_ref[...])
        xb = jnp.concatenate([lo.astype(BF16), hi.astype(BF16)], axis=1)
        g = jnp.dot(xb, wg_ref[0].astype(BF16), preferred_element_type=F32)
        u = jnp.dot(xb, wu_ref[0].astype(BF16), preferred_element_type=F32)
        hmid = (g * jax.nn.sigmoid(g) * u).astype(BF16)
        ys_ref[...] = _pack_halves(jnp.dot(hmid, wd_ref[0].astype(BF16), preferred_element_type=F32))

    @pl.when(jnp.logical_not(valid))
    def _():
        ys_ref[...] = jnp.zeros_like(ys_ref)


def _experts(blk_exp, blk_src, nvalid, xs, wg, wu, wd):
    n_rows = xs.shape[0]
    n_blocks = n_rows // ROW_BLOCK
    grid_spec = pltpu.PrefetchScalarGridSpec(
        num_scalar_prefetch=3,
        grid=(n_blocks,),
        in_specs=[
            pl.BlockSpec((ROW_BLOCK, D_HALF), lambda i, be, bs, nv: (bs[i], 0)),
            pl.BlockSpec((1, D_MODEL, D_EXPERT), lambda i, be, bs, nv: (be[i], 0, 0)),
            pl.BlockSpec((1, D_MODEL, D_EXPERT), lambda i, be, bs, nv: (be[i], 0, 0)),
            pl.BlockSpec((1, D_EXPERT, D_MODEL), lambda i, be, bs, nv: (be[i], 0, 0)),
        ],
        out_specs=pl.BlockSpec((ROW_BLOCK, D_HALF), lambda i, be, bs, nv: (i, 0)),
    )
    return pl.pallas_call(
        _expert_body,
        grid_spec=grid_spec,
        out_shape=jax.ShapeDtypeStruct((n_rows, D_HALF), U32),
        compiler_params=_cparams(("arbitrary",)),
        name="experts",
    )(blk_exp, blk_src, nvalid, xs, wg, wu, wd)


def _combine_body(pos0_ref, pos1_ref, pos0_next_ref, pos1_next_ref, ys_hbm, h_ref, route_ref,
                  ya_ref, yb_ref, buf_ref, sems, *, steps_a):
    n = pl.program_id(0)
    tm = h_ref.shape[0]
    slot = n % 2

    def row_copy(row, k, t, to_slot):
        return pltpu.make_async_copy(ys_hbm.at[pl.ds(row, 1)],
                                     buf_ref.at[to_slot, k, pl.ds(t, 1)], sems.at[to_slot])

    def gather(p0_ref, p1_ref, to_slot):
        for t in range(tm):
            row_copy(p0_ref[t], 0, t, to_slot).start()
            row_copy(p1_ref[t], 1, t, to_slot).start()

    @pl.when(n == 0)
    def _():
        gather(pos0_ref, pos1_ref, 0)

    @pl.when(n + 1 < pl.num_programs(0))
    def _():
        gather(pos0_next_ref, pos1_next_ref, 1 - slot)

    for t in range(tm):
        row_copy(0, 0, t, slot).wait()
        row_copy(0, 1, t, slot).wait()
    route = route_ref[...]
    lo1, hi1 = _unpack_halves(buf_ref[slot, 0])
    lo2, hi2 = _unpack_halves(buf_ref[slot, 1])
    g1, g2 = route[:, 2:3], route[:, 3:4]
    y_lo = h_ref[:, :D_HALF] + g1 * lo1 + g2 * lo2
    y_hi = h_ref[:, D_HALF:] + g1 * hi1 + g2 * hi2

    @pl.when(n < steps_a)
    def _():
        ya_ref[:, :D_HALF] = y_lo
        ya_ref[:, D_HALF:] = y_hi

    @pl.when(n >= steps_a)
    def _():
        yb_ref[:, :D_HALF] = y_lo
        yb_ref[:, D_HALF:] = y_hi


def _combine(pos0, pos1, ys, h, route, t_a, tm):
    t = h.shape[0]
    nt = t // tm
    steps_a = t_a // tm
    cur = lambda: pl.BlockSpec((tm,), lambda i: (i,), memory_space=pltpu.SMEM)
    nxt = lambda: pl.BlockSpec((tm,), lambda i: (jnp.minimum(i + 1, nt - 1),), memory_space=pltpu.SMEM)
    return pl.pallas_call(
        functools.partial(_combine_body, steps_a=steps_a),
        grid=(nt,),
        in_specs=[
            cur(), cur(), nxt(), nxt(),
            pl.BlockSpec(memory_space=pl.ANY),
            pl.BlockSpec((tm, D_MODEL), lambda i: (i, 0)),
            pl.BlockSpec((tm, LANES), lambda i: (i, 0)),
        ],
        out_specs=[
            pl.BlockSpec((tm, D_MODEL), lambda i: (jnp.minimum(i, steps_a - 1), 0)),
            pl.BlockSpec((tm, D_MODEL), lambda i: (jnp.maximum(i - steps_a, 0), 0)),
        ],
        out_shape=[
            jax.ShapeDtypeStruct((t_a, D_MODEL), F32),
            jax.ShapeDtypeStruct((t - t_a, D_MODEL), F32),
        ],
        scratch_shapes=[pltpu.VMEM((2, 2, tm, D_HALF), U32), pltpu.SemaphoreType.DMA((2,))],
        compiler_params=_cparams(("arbitrary",)),
        name="combine",
    )(pos0, pos1, pos0, pos1, ys, h, route)


def _pad_rope(a, axis):
    x1, x2 = jnp.split(a, 2, axis=axis)
    z = jnp.zeros_like(x1)
    return jnp.concatenate([x1, z, x2, z], axis=axis)


def _prepare(ln1_g, w_in, conv_w, q_a_norm_g, w_uq, kv_a_norm_g, w_ukv, q_norm_g, k_norm_g,
             w_conv_out, w_attn_out, w_out, ln2_g, w_router_group, b_router_group,
             w_router_exp, b_router_exp, w_gate, w_up, w_down):
    w_in0 = w_in[0]
    w_main = jnp.concatenate([w_in0[:, :KPE_OFF], w_in0[:, KPE_OFF + QK_ROPE:]], axis=1).astype(BF16)
    w_pe = _pad_rope(w_in0[:, KPE_OFF:KPE_OFF + QK_ROPE], 1).astype(BF16)

    wq = w_uq[0].reshape(Q_LORA, N_HEADS, QK_DIM)
    wq = jnp.concatenate([wq[:, :, :QK_NOPE], _pad_rope(wq[:, :, QK_NOPE:], 2)], axis=2)
    wq_t = wq.reshape(Q_LORA, N_HEADS * QK_PAD).T.astype(BF16)
    wkv = w_ukv[0].reshape(KV_LORA, N_HEADS, QK_NOPE + V_DIM)
    wkn = wkv[:, :, :QK_NOPE].reshape(KV_LORA, N_HEADS * QK_NOPE).astype(BF16)
    wv_t = wkv[:, :, QK_NOPE:].reshape(KV_LORA, N_HEADS * V_DIM).T.astype(BF16)

    qg = q_norm_g[0]
    score_scale = QK_DIM ** -0.5 * math.log2(math.e)
    gq = (jnp.concatenate([qg[:QK_NOPE], _pad_rope(qg[QK_NOPE:], 0)]) * score_scale).reshape(QK_PAD, 1)
    kg = k_norm_g[0]
    gkn = kg[:QK_NOPE].reshape(1, LANES)
    gkr = _pad_rope(kg[QK_NOPE:], 0).reshape(1, LANES)

    wr = jnp.concatenate([w_router_group[0], w_router_exp[0],
                          jnp.zeros((D_MODEL, LANES - N_GROUPS - N_EXPERTS), F32)], axis=1)
    wr_hi = wr.astype(BF16)
    wr_lo = (wr - wr_hi.astype(F32)).astype(BF16)
    br = jnp.concatenate([b_router_group[0], b_router_exp[0],
                          jnp.zeros((LANES - N_GROUPS - N_EXPERTS,), F32)]).reshape(1, LANES)
    return dict(
        ln1_g=ln1_g, w_main=w_main, w_pe=w_pe, conv_w=conv_w[0],
        qkv=(q_a_norm_g, kv_a_norm_g, gq, gkn, gkr, wq_t, wkn, wv_t),
        wc=w_conv_out[0].astype(BF16), wa=w_attn_out[0].astype(BF16), wo=w_out[0].astype(BF16),
        ln2_g=ln2_g, wr_hi=wr_hi, wr_lo=wr_lo, br=br,
        wg=w_gate[0], wu=w_up[0], wd=w_down[0],
    )


def _rope_tables(s):
    inv = ROPE_THETA ** (-jnp.arange(0, QK_ROPE, 2, dtype=F32) / QK_ROPE)
    ang = jnp.arange(s, dtype=F32)[:, None] * inv[None, :]
    cos, sin = jnp.cos(ang), jnp.sin(ang)
    z = jnp.zeros_like(cos)
    cos_k = jnp.concatenate([cos, z, cos, z], axis=1)
    sin_k = jnp.concatenate([-sin, z, sin, z], axis=1)
    return cos_k, sin_k, cos.T, sin.T


def _tiles(b, s):
    t = b * s
    pick = lambda n, pref: pref if n % pref == 0 else n
    return dict(
        in_tm=pick(t, 512), in_tn=2048,
        qkv_tm=pick(s, 256),
        attn_tq=pick(s, 1024), attn_kc=pick(s, 256),
        merge_tm=pick(s, 256), halo=16,
    )


def _moe_tiles(t_a, t_b):
    both = math.gcd(t_a, t_b)
    pick = lambda pref: pref if both % pref == 0 else both
    return dict(route_tm=pick(512), disp_tm=pick(512), comb_tm=pick(256))


def _mixer(x, p):
    b, s, _ = x.shape
    t = b * s
    tl = _tiles(b, s)
    x2d = x.reshape(t, D_MODEL)
    proj, kpe = _in_proj(x2d, p["ln1_g"], p["w_main"], p["w_pe"], tl["in_tm"], tl["in_tn"])
    qt, k, vt = _qkv(proj, kpe, _rope_tables(s), p["qkv"], b, s, tl["qkv_tm"])
    o = _attention(qt, k, vt, tl["attn_tq"], tl["attn_kc"])
    merged = _merge(proj, o.reshape(t, D_MODEL), p["conv_w"], p["wc"], p["wa"], b, s,
                    tl["merge_tm"], tl["halo"])
    return x2d, merged


def _forward(x_a, x_b, p):
    xa, mga = _mixer(x_a, p)
    xb, mgb = _mixer(x_b, p)
    t_a, t_b = xa.shape[0], xb.shape[0]
    t = t_a + t_b
    tl = _moe_tiles(t_a, t_b)
    h, xn2, route, route_t, counts = _out_route(xa, mga, xb, mgb, p["wo"], p["ln2_g"], p["wr_hi"],
                                                p["wr_lo"], p["br"], tl["route_tm"])
    n_blocks = 2 * t // ROW_BLOCK + N_EXPERTS
    counts_i32 = counts.reshape(LANES).astype(jnp.int32)
    pos_t, starts, blk_exp, blk_src, nvalid = _plan(counts_i32, route_t, n_blocks)
    pos0, pos1 = pos_t[0], pos_t[1]
    xs = _dispatch(pos0, pos1, counts_i32, starts, nvalid, xn2, n_blocks * ROW_BLOCK, tl["disp_tm"])
    ys = _experts(blk_exp, blk_src, nvalid, xs, p["wg"], p["wu"], p["wd"])
    y_a, y_b = _combine(pos0, pos1, ys, h, route, t_a, tl["comb_tm"])
    return y_a.reshape(x_a.shape), y_b.reshape(x_b.shape)


def kernel(x_prompt, x_sample, ln1_g, w_in, conv_w, q_a_norm_g, w_uq, kv_a_norm_g, w_ukv, q_norm_g,
           k_norm_g, w_conv_out, w_attn_out, w_out, ln2_g, w_router_group, b_router_group,
           w_router_exp, b_router_exp, w_gate, w_up, w_down):
    p = _prepare(ln1_g, w_in, conv_w, q_a_norm_g, w_uq, kv_a_norm_g, w_ukv, q_norm_g, k_norm_g,
                 w_conv_out, w_attn_out, w_out, ln2_g, w_router_group, b_router_group,
                 w_router_exp, b_router_exp, w_gate, w_up, w_down)
    return _forward(x_prompt, x_sample, p)
```

```python
import functools
import math

import jax
import jax.numpy as jnp
from jax import lax
from jax.experimental import pallas as pl
from jax.experimental.pallas import tpu as pltpu

F32 = jnp.float32
BF16 = jnp.bfloat16

D_MODEL = 2048
CONV_CH = 1024
N_HEADS = 16
QK_NOPE = 128
QK_ROPE = 64
HALF_ROPE = QK_ROPE // 2
QK_DIM = QK_NOPE + QK_ROPE
QK_PAD = 256
V_DIM = 128
Q_LORA = 512
KV_LORA = 512
ROPE_THETA = 10000.0
N_GROUPS = 8
EXP_PER_GROUP = 8
N_EXPERTS = 64
D_EXPERT = 512
EPS = 1e-6
PROJ_W = 8192
KPE_OFF = 4096
LANES = 128
EXP_LANE0 = N_GROUPS
ROW_BLOCK = 256
VMEM_LIMIT = 56 * 1024 * 1024

_NT = (((1,), (1,)), ((), ()))


def _cparams(sem):
    return pltpu.CompilerParams(dimension_semantics=sem, vmem_limit_bytes=VMEM_LIMIT)


D_HALF = D_MODEL // 2
U32 = jnp.uint32


def _pack_halves(x):
    lo = lax.bitcast_convert_type(x[:, :D_HALF].astype(BF16).astype(F32), U32)
    hi = lax.bitcast_convert_type(x[:, D_HALF:].astype(BF16).astype(F32), U32)
    return (lo >> 16) | hi


def _unpack_halves(w):
    lo = lax.bitcast_convert_type(w << 16, F32)
    hi = lax.bitcast_convert_type(w & jnp.uint32(0xFFFF0000), F32)
    return lo, hi


def _inproj_body(x_ref, g_ref, w_ref, wpe_ref, out_ref, kpe_ref, xn_ref):
    @pl.when(pl.program_id(1) == 0)
    def _():
        x = x_ref[...]
        ms = jnp.mean(x * x, axis=-1, keepdims=True)
        xn = (x * lax.rsqrt(ms + EPS) * g_ref[...]).astype(BF16)
        xn_ref[...] = xn
        kpe_ref[...] = jnp.dot(xn, wpe_ref[...], preferred_element_type=F32)

    out_ref[...] = jnp.dot(xn_ref[...], w_ref[...], preferred_element_type=F32).astype(BF16)


def _in_proj(x2d, ln1_g, w_main, w_pe, tm, tn):
    t = x2d.shape[0]
    return pl.pallas_call(
        _inproj_body,
        grid=(t // tm, PROJ_W // tn),
        in_specs=[
            pl.BlockSpec((tm, D_MODEL), lambda i, j: (i, 0)),
            pl.BlockSpec((1, D_MODEL), lambda i, j: (0, 0)),
            pl.BlockSpec((D_MODEL, tn), lambda i, j: (0, j)),
            pl.BlockSpec((D_MODEL, LANES), lambda i, j: (0, 0)),
        ],
        out_specs=[
            pl.BlockSpec((tm, tn), lambda i, j: (i, j)),
            pl.BlockSpec((tm, LANES), lambda i, j: (i, 0)),
        ],
        out_shape=[
            jax.ShapeDtypeStruct((t, PROJ_W), BF16),
            jax.ShapeDtypeStruct((t, LANES), F32),
        ],
        scratch_shapes=[pltpu.VMEM((tm, D_MODEL), BF16)],
        compiler_params=_cparams(("arbitrary", "arbitrary")),
        name="in_proj",
    )(x2d, ln1_g, w_main, w_pe)


def _qkv_body(ql_ref, kvl_ref, kpe_ref, cosk_ref, sink_ref, cost_ref, sint_ref,
              gqa_ref, gkva_ref, gq_ref, gkn_ref, gkr_ref, wq_ref, wkn_ref, wv_ref,
              qt_ref, k_ref, vt_ref):
    def latent_norm(ref, g_ref):
        v = ref[...].astype(F32)
        ms = jnp.mean(v * v, axis=-1, keepdims=True)
        return (v * lax.rsqrt(ms + EPS) * g_ref[...]).astype(BF16)

    qn = latent_norm(ql_ref, gqa_ref)
    kvn = latent_norm(kvl_ref, gkva_ref)
    tm = qn.shape[0]

    kn = jnp.dot(kvn, wkn_ref[...], preferred_element_type=F32)
    kpe = kpe_ref[...]
    ss_pe = jnp.sum(kpe * kpe, axis=-1, keepdims=True)
    kr = kpe * gkr_ref[...]
    kr = kr * cosk_ref[...] + pltpu.roll(kr, 2 * HALF_ROPE, axis=1) * sink_ref[...]
    gkn = gkn_ref[...]
    for h in range(N_HEADS):
        kh = kn[:, h * QK_NOPE:(h + 1) * QK_NOPE]
        ss = jnp.sum(kh * kh, axis=-1, keepdims=True) + ss_pe
        r = lax.rsqrt(ss * (1.0 / QK_DIM) + EPS)
        k_ref[0, h, :, 0:QK_NOPE] = (kh * gkn * r).astype(BF16)
        k_ref[0, h, :, QK_NOPE:QK_PAD] = (kr * r).astype(BF16)

    vt = lax.dot_general(wv_ref[...], kvn, _NT, preferred_element_type=F32)
    for h in range(N_HEADS):
        vt_ref[0, h] = vt[h * V_DIM:(h + 1) * V_DIM, :].astype(BF16)

    cost = cost_ref[...]
    sint = sint_ref[...]
    gq = gq_ref[...]
    zeros = jnp.zeros((HALF_ROPE, tm), BF16)
    for h in range(N_HEADS):
        qt = lax.dot_general(wq_ref[h * QK_PAD:(h + 1) * QK_PAD, :], qn, _NT,
                             preferred_element_type=F32)
        ss = jnp.sum(qt * qt, axis=0, keepdims=True)
        r = lax.rsqrt(ss * (1.0 / QK_DIM) + EPS)
        qs = qt * gq * r
        x1 = qs[QK_NOPE:QK_NOPE + HALF_ROPE]
        x2 = qs[QK_NOPE + 2 * HALF_ROPE:QK_NOPE + 3 * HALF_ROPE]
        qt_ref[0, h, 0:QK_NOPE, :] = qs[0:QK_NOPE].astype(BF16)
        qt_ref[0, h, QK_NOPE:QK_NOPE + HALF_ROPE, :] = (x1 * cost - x2 * sint).astype(BF16)
        qt_ref[0, h, QK_NOPE + HALF_ROPE:QK_NOPE + 2 * HALF_ROPE, :] = zeros
        qt_ref[0, h, QK_NOPE + 2 * HALF_ROPE:QK_NOPE + 3 * HALF_ROPE, :] = (
            x1 * sint + x2 * cost).astype(BF16)
        qt_ref[0, h, QK_NOPE + 3 * HALF_ROPE:QK_PAD, :] = zeros


def _qkv(proj, kpe, tabs, wts, b, s, tm):
    ns = s // tm
    cos_k, sin_k, cos_t, sin_t = tabs
    gqa, gkva, gq, gkn, gkr, wq_t, wkn, wv_t = wts
    const = lambda shape: pl.BlockSpec(shape, lambda bi, i: (0,) * len(shape))
    return pl.pallas_call(
        _qkv_body,
        grid=(b, ns),
        in_specs=[
            pl.BlockSpec((tm, Q_LORA), lambda bi, i: (bi * ns + i, 3 * CONV_CH // Q_LORA)),
            pl.BlockSpec((tm, KV_LORA), lambda bi, i: (bi * ns + i, 3 * CONV_CH // KV_LORA + 1)),
            pl.BlockSpec((tm, LANES), lambda bi, i: (bi * ns + i, 0)),
            pl.BlockSpec((tm, LANES), lambda bi, i: (i, 0)),
            pl.BlockSpec((tm, LANES), lambda bi, i: (i, 0)),
            pl.BlockSpec((HALF_ROPE, tm), lambda bi, i: (0, i)),
            pl.BlockSpec((HALF_ROPE, tm), lambda bi, i: (0, i)),
            const((1, Q_LORA)), const((1, KV_LORA)), const((QK_PAD, 1)),
            const((1, LANES)), const((1, LANES)),
            const((N_HEADS * QK_PAD, Q_LORA)),
            const((KV_LORA, N_HEADS * QK_NOPE)),
            const((N_HEADS * V_DIM, KV_LORA)),
        ],
        out_specs=[
            pl.BlockSpec((1, N_HEADS, QK_PAD, tm), lambda bi, i: (bi, 0, 0, i)),
            pl.BlockSpec((1, N_HEADS, tm, QK_PAD), lambda bi, i: (bi, 0, i, 0)),
            pl.BlockSpec((1, N_HEADS, V_DIM, tm), lambda bi, i: (bi, 0, 0, i)),
        ],
        out_shape=[
            jax.ShapeDtypeStruct((b, N_HEADS, QK_PAD, s), BF16),
            jax.ShapeDtypeStruct((b, N_HEADS, s, QK_PAD), BF16),
            jax.ShapeDtypeStruct((b, N_HEADS, V_DIM, s), BF16),
        ],
        compiler_params=_cparams(("arbitrary", "arbitrary")),
        name="qkv",
    )(proj, proj, kpe, cos_k, sin_k, cos_t, sin_t, gqa, gkva, gq, gkn, gkr, wq_t, wkn, wv_t)


def _attn_body(qt_ref, k_ref, vt_ref, o_ref, s_a, m_a, s_b, m_b, *, kc):
    n = pl.program_id(0)
    s_len = k_ref.shape[2]
    chunks = [(c * kc, (c + 1) * kc) for c in range(s_len // kc)]

    @pl.when(n == 0)
    def _():
        s_b[...] = jnp.zeros_like(s_b)
        m_b[...] = jnp.zeros_like(m_b)

    def step(s_w, m_w, s_r, m_r):
        m_prev = m_r[...]
        qt = qt_ref[0, 0]
        l = None
        acc = None
        m = None
        for lo, hi in chunks:
            p = jnp.exp2(s_r[lo:hi, :] - m_prev)
            lc = jnp.sum(p, axis=0, keepdims=True)
            l = lc if l is None else l + lc
            pv = jnp.dot(vt_ref[0, 0, :, lo:hi], p.astype(BF16), preferred_element_type=F32)
            acc = pv if acc is None else acc + pv

            sc = jnp.dot(k_ref[0, 0, lo:hi, :], qt, preferred_element_type=F32)
            s_w[lo:hi, :] = sc
            mc = jnp.max(sc, axis=0, keepdims=True)
            m = mc if m is None else jnp.maximum(m, mc)
        o_ref[0] = (acc * (1.0 / l)).T.astype(BF16)
        m_w[...] = m

    @pl.when(n % 2 == 0)
    def _():
        step(s_a, m_a, s_b, m_b)

    @pl.when(n % 2 == 1)
    def _():
        step(s_b, m_b, s_a, m_a)


def _attention(qt, k, vt, tq, kc):
    b, _, _, s = qt.shape
    nq = s // tq
    n_tiles = b * N_HEADS * nq

    def bhi(tile):
        return tile // (N_HEADS * nq), (tile // nq) % N_HEADS, tile % nq

    def score_tile(n):
        return bhi(jnp.minimum(n, n_tiles - 1))

    def value_tile(n):
        return bhi(jnp.maximum(n - 1, 0))

    def qt_map(n):
        bi, h, i = score_tile(n)
        return bi, h, 0, i

    def k_map(n):
        bi, h, _ = score_tile(n)
        return bi, h, 0, 0

    def vt_map(n):
        bi, h, _ = value_tile(n)
        return bi, h, 0, 0

    def o_map(n):
        bi, h, i = value_tile(n)
        return bi, i, h

    return pl.pallas_call(
        functools.partial(_attn_body, kc=kc),
        grid=(n_tiles + 1,),
        in_specs=[
            pl.BlockSpec((1, 1, QK_PAD, tq), qt_map),
            pl.BlockSpec((1, 1, s, QK_PAD), k_map),
            pl.BlockSpec((1, 1, V_DIM, s), vt_map),
        ],
        out_specs=pl.BlockSpec((1, tq, V_DIM), o_map),
        out_shape=jax.ShapeDtypeStruct((b, s, N_HEADS * V_DIM), BF16),
        scratch_shapes=[pltpu.VMEM((s, tq), F32), pltpu.VMEM((1, tq), F32),
                        pltpu.VMEM((s, tq), F32), pltpu.VMEM((1, tq), F32)],
        compiler_params=_cparams(("arbitrary",)),
        name="attention",
    )(qt, k, vt)


def _merge_body(u_ref, gb_ref, gc_ref, up_ref, gcp_ref, un_ref, gcn_ref, o_ref, gtc_ref, gta_ref,
                cw_ref, wc_ref, wa_ref, out_ref):
    i = pl.program_id(1)
    last = pl.num_programs(1) - 1
    halo = up_ref.shape[0]
    v = gc_ref[...].astype(F32) * u_ref[...].astype(F32)
    tm = v.shape[0]
    v_before = gcp_ref[halo - 1:halo, :].astype(F32) * up_ref[halo - 1:halo, :].astype(F32)
    v_after = gcn_ref[0:1, :].astype(F32) * un_ref[0:1, :].astype(F32)
    v_before = jnp.where(i == 0, 0.0, v_before)
    v_after = jnp.where(i == last, 0.0, v_after)
    row = lax.broadcasted_iota(jnp.int32, (tm, 1), 0)
    v_prev = jnp.where(row == 0, v_before, pltpu.roll(v, 1, axis=0))
    v_next = jnp.where(row == tm - 1, v_after, pltpu.roll(v, tm - 1, axis=0))
    cw = cw_ref[...]
    conv = v_prev * cw[0:1, :] + v * cw[1:2, :] + v_next * cw[2:3, :]
    z = (gb_ref[...].astype(F32) * conv).astype(BF16)
    conv_out = jnp.dot(z, wc_ref[...], preferred_element_type=F32)
    attn_out = jnp.dot(o_ref[...], wa_ref[...], preferred_element_type=F32)
    merged = (jax.nn.sigmoid(gtc_ref[...].astype(F32)) * conv_out
              + jax.nn.sigmoid(gta_ref[...].astype(F32)) * attn_out)
    out_ref[...] = merged.astype(BF16)


def _merge(proj, o2d, conv_w, wc, wa, b, s, tm, halo):
    ns = s // tm
    t = b * s
    hb = tm // halo
    nh = t // halo
    row = lambda bi, i: bi * ns + i
    prev = lambda bi, i: (jnp.maximum(row(bi, i) * hb - 1, 0))
    nxt = lambda bi, i: (jnp.minimum((row(bi, i) + 1) * hb, nh - 1))
    gate0 = KPE_OFF // D_MODEL
    return pl.pallas_call(
        _merge_body,
        grid=(b, ns),
        in_specs=[
            pl.BlockSpec((tm, CONV_CH), lambda bi, i: (row(bi, i), 0)),
            pl.BlockSpec((tm, CONV_CH), lambda bi, i: (row(bi, i), 1)),
            pl.BlockSpec((tm, CONV_CH), lambda bi, i: (row(bi, i), 2)),
            pl.BlockSpec((halo, CONV_CH), lambda bi, i: (prev(bi, i), 0)),
            pl.BlockSpec((halo, CONV_CH), lambda bi, i: (prev(bi, i), 2)),
            pl.BlockSpec((halo, CONV_CH), lambda bi, i: (nxt(bi, i), 0)),
            pl.BlockSpec((halo, CONV_CH), lambda bi, i: (nxt(bi, i), 2)),
            pl.BlockSpec((tm, D_MODEL), lambda bi, i: (row(bi, i), 0)),
            pl.BlockSpec((tm, D_MODEL), lambda bi, i: (row(bi, i), gate0)),
            pl.BlockSpec((tm, D_MODEL), lambda bi, i: (row(bi, i), gate0 + 1)),
            pl.BlockSpec((3, CONV_CH), lambda bi, i: (0, 0)),
            pl.BlockSpec((CONV_CH, D_MODEL), lambda bi, i: (0, 0)),
            pl.BlockSpec((D_MODEL, D_MODEL), lambda bi, i: (0, 0)),
        ],
        out_specs=pl.BlockSpec((tm, D_MODEL), lambda bi, i: (row(bi, i), 0)),
        out_shape=jax.ShapeDtypeStruct((t, D_MODEL), BF16),
        compiler_params=_cparams(("arbitrary", "arbitrary")),
        name="merge",
    )(proj, proj, proj, proj, proj, proj, proj, o2d, proj, proj, conv_w, wc, wa)


def _out_route_body(xa_ref, mga_ref, xb_ref, mgb_ref, wo_ref, g2_ref, wr_ref, br_ref,
                    h_ref, xn_ref, route_ref, routet_ref, cnt_ref, base_ref, *, steps_a):
    step = pl.program_id(0)

    @pl.when(step == 0)
    def _():
        base_ref[...] = jnp.zeros_like(base_ref)

    in_a = step < steps_a
    x = jnp.where(in_a, xa_ref[...], xb_ref[...])
    mg = jnp.where(in_a, mga_ref[...], mgb_ref[...])
    h = x + jnp.dot(mg, wo_ref[...], preferred_element_type=F32)
    h_ref[...] = h
    ms = jnp.mean(h * h, axis=-1, keepdims=True)
    xn = h * lax.rsqrt(ms + EPS) * g2_ref[...]
    xn_ref[...] = _pack_halves(xn)
    tm = xn.shape[0]

    hi = xn.astype(BF16)
    lo = (xn - hi.astype(F32)).astype(BF16)
    w_hl = wr_ref[...]
    both = jnp.dot(hi, w_hl, preferred_element_type=F32)
    logits = (both[:, :LANES] + both[:, LANES:]
              + jnp.dot(lo, w_hl[:, :LANES], preferred_element_type=F32)) + br_ref[...]

    lane = lax.broadcasted_iota(jnp.int32, (tm, LANES), 1)
    neg = -jnp.inf
    lg = jnp.where(lane < N_GROUPS, logits, neg)
    gmax = jnp.max(lg, axis=-1, keepdims=True)
    g_p = 1.0 / jnp.sum(jnp.exp(lg - gmax), axis=-1, keepdims=True)
    g_sel = jnp.min(jnp.where(lg == gmax, lane, LANES), axis=-1, keepdims=True)
    lo_lane = EXP_LANE0 + g_sel * EXP_PER_GROUP
    le = jnp.where((lane >= lo_lane) & (lane < lo_lane + EXP_PER_GROUP), logits, neg)
    m1 = jnp.max(le, axis=-1, keepdims=True)
    i1 = jnp.min(jnp.where(le == m1, lane, LANES), axis=-1, keepdims=True)
    le2 = jnp.where(lane == i1, neg, le)
    m2 = jnp.max(le2, axis=-1, keepdims=True)
    i2 = jnp.min(jnp.where(le2 == m2, lane, LANES), axis=-1, keepdims=True)
    e2 = jnp.exp(m2 - m1)
    gate1 = g_p / (1.0 + e2)
    gate2 = g_p * e2 / (1.0 + e2)

    sel1 = lane == i1
    sel2 = lane == i2
    onehot = jnp.where(sel1 | sel2, 1.0, 0.0)
    r_i = lax.broadcasted_iota(jnp.int32, (tm, tm), 0)
    c_i = lax.broadcasted_iota(jnp.int32, (tm, tm), 1)
    lower = jnp.where(r_i > c_i, 1.0, 0.0).astype(BF16)
    before = jnp.dot(lower, onehot.astype(BF16), preferred_element_type=F32) + base_ref[...]
    rank1 = jnp.sum(jnp.where(sel1, before, 0.0), axis=-1, keepdims=True)
    rank2 = jnp.sum(jnp.where(sel2, before, 0.0), axis=-1, keepdims=True)
    base_ref[...] = base_ref[...] + jnp.sum(onehot, axis=0, keepdims=True)
    cnt_ref[...] = base_ref[...]

    e1f = (i1 - EXP_LANE0).astype(F32)
    e2f = (i2 - EXP_LANE0).astype(F32)
    cols = (e1f, e2f, gate1, gate2, rank1, rank2)
    route = jnp.zeros((tm, LANES), F32)
    for c, val in enumerate(cols):
        route = jnp.where(lane == c, val, route)
    route_ref[...] = route
    routet_ref[...] = route.T[0:8, :]


def _out_route(xa, mga, xb, mgb, wo, ln2_g, wr_hl, br, tm):
    steps_a, steps_b = xa.shape[0] // tm, xb.shape[0] // tm
    t = xa.shape[0] + xb.shape[0]
    const = lambda shape: pl.BlockSpec(shape, lambda i: (0,) * len(shape))
    seg_a = pl.BlockSpec((tm, D_MODEL), lambda i: (jnp.minimum(i, steps_a - 1), 0))
    seg_b = pl.BlockSpec((tm, D_MODEL), lambda i: (jnp.maximum(i - steps_a, 0), 0))
    return pl.pallas_call(
        functools.partial(_out_route_body, steps_a=steps_a),
        grid=(steps_a + steps_b,),
        in_specs=[
            seg_a, seg_a, seg_b, seg_b,
            const((D_MODEL, D_MODEL)), const((1, D_MODEL)),
            const((D_MODEL, 2 * LANES)), const((1, LANES)),
        ],
        out_specs=[
            pl.BlockSpec((tm, D_MODEL), lambda i: (i, 0)),
            pl.BlockSpec((tm, D_HALF), lambda i: (i, 0)),
            pl.BlockSpec((tm, LANES), lambda i: (i, 0)),
            pl.BlockSpec((8, tm), lambda i: (0, i)),
            const((1, LANES)),
        ],
        out_shape=[
            jax.ShapeDtypeStruct((t, D_MODEL), F32),
            jax.ShapeDtypeStruct((t, D_HALF), U32),
            jax.ShapeDtypeStruct((t, LANES), F32),
            jax.ShapeDtypeStruct((8, t), F32),
            jax.ShapeDtypeStruct((1, LANES), F32),
        ],
        scratch_shapes=[pltpu.VMEM((1, LANES), F32)],
        compiler_params=_cparams(("arbitrary",)),
        name="out_route",
    )(xa, mga, xb, mgb, wo, ln2_g, wr_hl, br)


def _block_meta(next_expert, slot, first):
    return (next_expert + 1) * 4 + slot * 2 + first


def _plan_body(cnt_ref, routet_ref, pos_ref, start_ref, blk_exp_ref, blk_src_ref, blk_meta_ref,
               nvalid_ref, next_ref):
    n_blocks = blk_exp_ref.shape[0]

    def backwards(k, following):
        e = N_EXPERTS - 1 - k
        next_ref[e] = following
        return jnp.where(cnt_ref[EXP_LANE0 + e] > 0, e, following)

    lax.fori_loop(0, N_EXPERTS, backwards, -1)

    def per_expert(e, carry):
        acc, ordinal = carry
        c = cnt_ref[EXP_LANE0 + e]
        nb = (c + ROW_BLOCK - 1) // ROW_BLOCK
        start_ref[e] = acc * ROW_BLOCK

        def fill(j, inner):
            blk_exp_ref[acc + j] = e
            blk_src_ref[acc + j] = acc + j
            blk_meta_ref[acc + j] = _block_meta(next_ref[e], ordinal % 2, jnp.where(j == 0, 1, 0))
            return inner

        lax.fori_loop(0, nb, fill, 0)
        return acc + nb, ordinal + jnp.where(nb > 0, 1, 0)

    nvalid, _ = lax.fori_loop(0, N_EXPERTS, per_expert, (0, 0))
    nvalid_ref[0] = nvalid
    last_exp = blk_exp_ref[nvalid - 1]

    def tail(j, carry):
        blk_exp_ref[j] = last_exp
        blk_src_ref[j] = nvalid - 1
        blk_meta_ref[j] = 0
        return carry

    lax.fori_loop(nvalid, n_blocks, tail, 0)

    ef = routet_ref[0:2, :]
    off = jnp.zeros_like(ef)
    for e in range(N_EXPERTS):
        off = jnp.where(ef == float(e), start_ref[e].astype(F32), off)
    pos_ref[...] = (off + routet_ref[4:6, :]).astype(jnp.int32)


def _plan(counts_i32, routet, n_blocks):
    t = routet.shape[1]
    smem = lambda: pl.BlockSpec(memory_space=pltpu.SMEM)
    return pl.pallas_call(
        _plan_body,
        in_specs=[smem(), pl.BlockSpec(memory_space=pltpu.VMEM)],
        out_specs=[pl.BlockSpec(memory_space=pltpu.VMEM), smem(), smem(), smem(), smem(), smem()],
        out_shape=[
            jax.ShapeDtypeStruct((2, t), jnp.int32),
            jax.ShapeDtypeStruct((N_EXPERTS,), jnp.int32),
            jax.ShapeDtypeStruct((n_blocks,), jnp.int32),
            jax.ShapeDtypeStruct((n_blocks,), jnp.int32),
            jax.ShapeDtypeStruct((n_blocks,), jnp.int32),
            jax.ShapeDtypeStruct((1,), jnp.int32),
        ],
        scratch_shapes=[pltpu.SMEM((N_EXPERTS,), jnp.int32)],
        compiler_params=pltpu.CompilerParams(vmem_limit_bytes=VMEM_LIMIT),
        name="plan",
    )(counts_i32, routet)


def _dispatch_body(pos0_ref, pos1_ref, cnt_ref, start_ref, nvalid_ref, xn_ref, xs_hbm, zero_ref, sem):
    tm = xn_ref.shape[0]
    n_blocks = xs_hbm.shape[0] // ROW_BLOCK

    def put(src_ref, src_row, dst_row):
        return pltpu.make_async_copy(src_ref.at[pl.ds(src_row, 1)], xs_hbm.at[pl.ds(dst_row, 1)], sem)

    def put_block(blk):
        return pltpu.make_async_copy(zero_ref, xs_hbm.at[pl.ds(blk * ROW_BLOCK, ROW_BLOCK)], sem)

    @pl.when(pl.program_id(0) == 0)
    def _():
        zero_ref[...] = jnp.zeros_like(zero_ref)

        def tail(blk, carry):
            put_block(blk).start()
            return carry

        def tail_done(blk, carry):
            put_block(0).wait()
            return carry

        lax.fori_loop(nvalid_ref[0], n_blocks, tail, 0)

        def pad_rows(wait):
            def per_expert(e, carry):
                c = cnt_ref[EXP_LANE0 + e]
                end = (c + ROW_BLOCK - 1) // ROW_BLOCK * ROW_BLOCK
                base = start_ref[e]

                def fill(r, inner):
                    if wait:
                        put(zero_ref, 0, 0).wait()
                    else:
                        put(zero_ref, 0, base + r).start()
                    return inner

                lax.fori_loop(c, end, fill, 0)
                return carry

            lax.fori_loop(0, N_EXPERTS, per_expert, 0)

        pad_rows(wait=False)
        lax.fori_loop(nvalid_ref[0], n_blocks, tail_done, 0)
        pad_rows(wait=True)

    for t in range(tm):
        put(xn_ref, t, pos0_ref[t]).start()
        put(xn_ref, t, pos1_ref[t]).start()

    for t in range(2 * tm):
        put(xn_ref, 0, 0).wait()


def _dispatch(pos0, pos1, counts_i32, starts, nvalid, xn2, n_rows, tm):
    t = xn2.shape[0]
    smem = lambda: pl.BlockSpec(memory_space=pltpu.SMEM)
    pos_spec = lambda: pl.BlockSpec((tm,), lambda i: (i,), memory_space=pltpu.SMEM)
    return pl.pallas_call(
        _dispatch_body,
        grid=(t // tm,),
        in_specs=[
            pos_spec(), pos_spec(),
            smem(), smem(), smem(),
            pl.BlockSpec((tm, D_HALF), lambda i: (i, 0)),
        ],
        out_specs=pl.BlockSpec(memory_space=pl.ANY),
        out_shape=jax.ShapeDtypeStruct((n_rows, D_HALF), U32),
        scratch_shapes=[pltpu.VMEM((ROW_BLOCK, D_HALF), U32), pltpu.SemaphoreType.DMA(())],
        compiler_params=pltpu.CompilerParams(dimension_semantics=("arbitrary",),
                                             vmem_limit_bytes=VMEM_LIMIT, has_side_effects=True),
        name="dispatch",
    )(pos0, pos1, counts_i32, starts, nvalid, xn2)


def _expert_body(blk_exp_ref, blk_src_ref, blk_meta_ref, nvalid_ref, xs_ref, wg_hbm, wu_hbm, wd_hbm,
                 ys_ref, wg_buf, wu_buf, wd_buf, sems):
    i = pl.program_id(0)
    valid = i < nvalid_ref[0]
    meta = blk_meta_ref[i]
    first = meta % 2
    slot = (meta // 2) % 2
    next_expert = meta // 4 - 1

    def weight_copies(expert, to_slot):
        return [pltpu.make_async_copy(hbm.at[expert], buf.at[to_slot], sems.at[to_slot])
                for hbm, buf in ((wg_hbm, wg_buf), (wu_hbm, wu_buf), (wd_hbm, wd_buf))]

    @pl.when(i == 0)
    def _():
        for cp in weight_copies(blk_exp_ref[0], 0):
            cp.start()

    @pl.when(first == 1)
    def _():
        for cp in weight_copies(blk_exp_ref[i], slot):
            cp.wait()

        @pl.when(next_expert >= 0)
        def _():
            for cp in weight_copies(next_expert, 1 - slot):
                cp.start()

    @pl.when(valid)
    def _():
        lo, hi = _unpack_halves(xs_ref[...])
        xb = jnp.concatenate([lo.astype(BF16), hi.astype(BF16)], axis=1)
        g = jnp.dot(xb, wg_buf[slot].astype(BF16), preferred_element_type=F32)
        u = jnp.dot(xb, wu_buf[slot].astype(BF16), preferred_element_type=F32)
        hmid = (g * jax.nn.sigmoid(g) * u).astype(BF16)
        ys_ref[...] = _pack_halves(jnp.dot(hmid, wd_buf[slot].astype(BF16), preferred_element_type=F32))

    @pl.when(jnp.logical_not(valid))
    def _():
        ys_ref[...] = jnp.zeros_like(ys_ref)


def _experts(blk_exp, blk_src, blk_meta, nvalid, xs, wg, wu, wd):
    n_rows = xs.shape[0]
    n_blocks = n_rows // ROW_BLOCK
    hbm = lambda: pl.BlockSpec(memory_space=pl.ANY)
    grid_spec = pltpu.PrefetchScalarGridSpec(
        num_scalar_prefetch=4,
        grid=(n_blocks,),
        in_specs=[
            pl.BlockSpec((ROW_BLOCK, D_HALF), lambda i, be, bs, bm, nv: (bs[i], 0)),
            hbm(), hbm(), hbm(),
        ],
        out_specs=pl.BlockSpec((ROW_BLOCK, D_HALF), lambda i, be, bs, bm, nv: (i, 0)),
        scratch_shapes=[
            pltpu.VMEM((2, D_MODEL, D_EXPERT), F32),
            pltpu.VMEM((2, D_MODEL, D_EXPERT), F32),
            pltpu.VMEM((2, D_EXPERT, D_MODEL), F32),
            pltpu.SemaphoreType.DMA((2,)),
        ],
    )
    return pl.pallas_call(
        _expert_body,
        grid_spec=grid_spec,
        out_shape=jax.ShapeDtypeStruct((n_rows, D_HALF), U32),
        compiler_params=_cparams(("arbitrary",)),
        name="experts",
    )(blk_exp, blk_src, blk_meta, nvalid, xs, wg, wu, wd)


def _combine_body(pos0_ref, pos1_ref, pos0_next_ref, pos1_next_ref, ys_hbm, h_ref, route_ref,
                  ya_ref, yb_ref, buf_ref, sems, *, steps_a):
    n = pl.program_id(0)
    tm = h_ref.shape[0]
    slot = n % 2

    def row_copy(row, k, t, to_slot):
        return pltpu.make_async_copy(ys_hbm.at[pl.ds(row, 1)],
                                     buf_ref.at[to_slot, k, pl.ds(t, 1)], sems.at[to_slot])

    def gather(p0_ref, p1_ref, to_slot):
        for t in range(tm):
            row_copy(p0_ref[t], 0, t, to_slot).start()
            row_copy(p1_ref[t], 1, t, to_slot).start()

    @pl.when(n == 0)
    def _():
        gather(pos0_ref, pos1_ref, 0)

    @pl.when(n + 1 < pl.num_programs(0))
    def _():
        gather(pos0_next_ref, pos1_next_ref, 1 - slot)

    for t in range(tm):
        row_copy(0, 0, t, slot).wait()
        row_copy(0, 1, t, slot).wait()
    route = route_ref[...]
    lo1, hi1 = _unpack_halves(buf_ref[slot, 0])
    lo2, hi2 = _unpack_halves(buf_ref[slot, 1])
    g1, g2 = route[:, 2:3], route[:, 3:4]
    y_lo = h_ref[:, :D_HALF] + g1 * lo1 + g2 * lo2
    y_hi = h_ref[:, D_HALF:] + g1 * hi1 + g2 * hi2

    @pl.when(n < steps_a)
    def _():
        ya_ref[:, :D_HALF] = y_lo
        ya_ref[:, D_HALF:] = y_hi

    @pl.when(n >= steps_a)
    def _():
        yb_ref[:, :D_HALF] = y_lo
        yb_ref[:, D_HALF:] = y_hi


def _combine(pos0, pos1, ys, h, route, t_a, tm):
    t = h.shape[0]
    nt = t // tm
    steps_a = t_a // tm
    cur = lambda: pl.BlockSpec((tm,), lambda i: (i,), memory_space=pltpu.SMEM)
    nxt = lambda: pl.BlockSpec((tm,), lambda i: (jnp.minimum(i + 1, nt - 1),), memory_space=pltpu.SMEM)
    return pl.pallas_call(
        functools.partial(_combine_body, steps_a=steps_a),
        grid=(nt,),
        in_specs=[
            cur(), cur(), nxt(), nxt(),
            pl.BlockSpec(memory_space=pl.ANY),
            pl.BlockSpec((tm, D_MODEL), lambda i: (i, 0)),
            pl.BlockSpec((tm, LANES), lambda i: (i, 0)),
        ],
        out_specs=[
            pl.BlockSpec((tm, D_MODEL), lambda i: (jnp.minimum(i, steps_a - 1), 0)),
            pl.BlockSpec((tm, D_MODEL), lambda i: (jnp.maximum(i - steps_a, 0), 0)),
        ],
        out_shape=[
            jax.ShapeDtypeStruct((t_a, D_MODEL), F32),
            jax.ShapeDtypeStruct((t - t_a, D_MODEL), F32),
        ],
        scratch_shapes=[pltpu.VMEM((2, 2, tm, D_HALF), U32), pltpu.SemaphoreType.DMA((2,))],
        compiler_params=_cparams(("arbitrary",)),
        name="combine",
    )(pos0, pos1, pos0, pos1, ys, h, route)


def _pad_rope(a, axis):
    x1, x2 = jnp.split(a, 2, axis=axis)
    z = jnp.zeros_like(x1)
    return jnp.concatenate([x1, z, x2, z], axis=axis)


def _prepare(ln1_g, w_in, conv_w, q_a_norm_g, w_uq, kv_a_norm_g, w_ukv, q_norm_g, k_norm_g,
             w_conv_out, w_attn_out, w_out, ln2_g, w_router_group, b_router_group,
             w_router_exp, b_router_exp, w_gate, w_up, w_down):
    w_in0 = w_in[0]
    w_main = jnp.concatenate([w_in0[:, :KPE_OFF], w_in0[:, KPE_OFF + QK_ROPE:]], axis=1).astype(BF16)
    w_pe = _pad_rope(w_in0[:, KPE_OFF:KPE_OFF + QK_ROPE], 1).astype(BF16)

    wq = w_uq[0].reshape(Q_LORA, N_HEADS, QK_DIM)
    wq = jnp.concatenate([wq[:, :, :QK_NOPE], _pad_rope(wq[:, :, QK_NOPE:], 2)], axis=2)
    wq_t = wq.reshape(Q_LORA, N_HEADS * QK_PAD).T.astype(BF16)
    wkv = w_ukv[0].reshape(KV_LORA, N_HEADS, QK_NOPE + V_DIM)
    wkn = wkv[:, :, :QK_NOPE].reshape(KV_LORA, N_HEADS * QK_NOPE).astype(BF16)
    wv_t = wkv[:, :, QK_NOPE:].reshape(KV_LORA, N_HEADS * V_DIM).T.astype(BF16)

    qg = q_norm_g[0]
    score_scale = QK_DIM ** -0.5 * math.log2(math.e)
    gq = (jnp.concatenate([qg[:QK_NOPE], _pad_rope(qg[QK_NOPE:], 0)]) * score_scale).reshape(QK_PAD, 1)
    kg = k_norm_g[0]
    gkn = kg[:QK_NOPE].reshape(1, LANES)
    gkr = _pad_rope(kg[QK_NOPE:], 0).reshape(1, LANES)

    wr = jnp.concatenate([w_router_group[0], w_router_exp[0],
                          jnp.zeros((D_MODEL, LANES - N_GROUPS - N_EXPERTS), F32)], axis=1)
    wr_hi = wr.astype(BF16)
    wr_hl = jnp.concatenate([wr_hi, (wr - wr_hi.astype(F32)).astype(BF16)], axis=1)
    br = jnp.concatenate([b_router_group[0], b_router_exp[0],
                          jnp.zeros((LANES - N_GROUPS - N_EXPERTS,), F32)]).reshape(1, LANES)
    return dict(
        ln1_g=ln1_g, w_main=w_main, w_pe=w_pe, conv_w=conv_w[0],
        qkv=(q_a_norm_g, kv_a_norm_g, gq, gkn, gkr, wq_t, wkn, wv_t),
        wc=w_conv_out[0].astype(BF16), wa=w_attn_out[0].astype(BF16), wo=w_out[0].astype(BF16),
        ln2_g=ln2_g, wr_hl=wr_hl, br=br,
        wg=w_gate[0], wu=w_up[0], wd=w_down[0],
    )


def _rope_tables(s):
    inv = ROPE_THETA ** (-jnp.arange(0, QK_ROPE, 2, dtype=F32) / QK_ROPE)
    ang = jnp.arange(s, dtype=F32)[:, None] * inv[None, :]
    cos, sin = jnp.cos(ang), jnp.sin(ang)
    z = jnp.zeros_like(cos)
    cos_k = jnp.concatenate([cos, z, cos, z], axis=1)
    sin_k = jnp.concatenate([-sin, z, sin, z], axis=1)
    return cos_k, sin_k, cos.T, sin.T


def _tiles(b, s):
    t = b * s
    pick = lambda n, pref: pref if n % pref == 0 else n
    return dict(
        in_tm=pick(t, 512), in_tn=2048,
        qkv_tm=pick(s, 256),
        attn_tq=pick(s, 1024), attn_kc=pick(s, 256),
        merge_tm=pick(s, 256), halo=16,
    )


def _moe_tiles(t_a, t_b):
    both = math.gcd(t_a, t_b)
    pick = lambda pref: pref if both % pref == 0 else both
    return dict(route_tm=pick(512), disp_tm=pick(512), comb_tm=pick(256))


def _mixer(x, p):
    b, s, _ = x.shape
    t = b * s
    tl = _tiles(b, s)
    x2d = x.reshape(t, D_MODEL)
    proj, kpe = _in_proj(x2d, p["ln1_g"], p["w_main"], p["w_pe"], tl["in_tm"], tl["in_tn"])
    qt, k, vt = _qkv(proj, kpe, _rope_tables(s), p["qkv"], b, s, tl["qkv_tm"])
    o = _attention(qt, k, vt, tl["attn_tq"], tl["attn_kc"])
    merged = _merge(proj, o.reshape(t, D_MODEL), p["conv_w"], p["wc"], p["wa"], b, s,
                    tl["merge_tm"], tl["halo"])
    return x2d, merged


def _forward(x_a, x_b, p):
    xa, mga = _mixer(x_a, p)
    xb, mgb = _mixer(x_b, p)
    t_a, t_b = xa.shape[0], xb.shape[0]
    t = t_a + t_b
    tl = _moe_tiles(t_a, t_b)
    h, xn2, route, route_t, counts = _out_route(xa, mga, xb, mgb, p["wo"], p["ln2_g"], p["wr_hl"],
                                                p["br"], tl["route_tm"])
    n_blocks = 2 * t // ROW_BLOCK + N_EXPERTS
    counts_i32 = counts.reshape(LANES).astype(jnp.int32)
    pos_t, starts, blk_exp, blk_src, blk_meta, nvalid = _plan(counts_i32, route_t, n_blocks)
    pos0, pos1 = pos_t[0], pos_t[1]
    xs = _dispatch(pos0, pos1, counts_i32, starts, nvalid, xn2, n_blocks * ROW_BLOCK, tl["disp_tm"])
    ys = _experts(blk_exp, blk_src, blk_meta, nvalid, xs, p["wg"], p["wu"], p["wd"])
    y_a, y_b = _combine(pos0, pos1, ys, h, route, t_a, tl["comb_tm"])
    return y_a.reshape(x_a.shape), y_b.reshape(x_b.shape)


def kernel(x_prompt, x_sample, ln1_g, w_in, conv_w, q_a_norm_g, w_uq, kv_a_norm_g, w_ukv, q_norm_g,
           k_norm_g, w_conv_out, w_attn_out, w_out, ln2_g, w_router_group, b_router_group,
           w_router_exp, b_router_exp, w_gate, w_up, w_down):
    p = _prepare(ln1_g, w_in, conv_w, q_a_norm_g, w_uq, kv_a_norm_g, w_ukv, q_norm_g, k_norm_g,
                 w_conv_out, w_attn_out, w_out, ln2_g, w_router_group, b_router_group,
                 w_router_exp, b_router_exp, w_gate, w_up, w_down)
    return _forward(x_prompt, x_sample, p)
```

```python
import functools
import math

import jax
import jax.numpy as jnp
from jax import lax
from jax.experimental import pallas as pl
from jax.experimental.pallas import tpu as pltpu

F32 = jnp.float32
BF16 = jnp.bfloat16

D_MODEL = 2048
CONV_CH = 1024
N_HEADS = 16
QK_NOPE = 128
QK_ROPE = 64
HALF_ROPE = QK_ROPE // 2
QK_DIM = QK_NOPE + QK_ROPE
QK_PAD = 256
V_DIM = 128
Q_LORA = 512
KV_LORA = 512
ROPE_THETA = 10000.0
N_GROUPS = 8
EXP_PER_GROUP = 8
N_EXPERTS = 64
D_EXPERT = 512
EPS = 1e-6
PROJ_W = 8192
KPE_OFF = 4096
LANES = 128
EXP_LANE0 = N_GROUPS
ROW_BLOCK = 256
VMEM_LIMIT = 56 * 1024 * 1024

_NT = (((1,), (1,)), ((), ()))


def _cparams(sem):
    return pltpu.CompilerParams(dimension_semantics=sem, vmem_limit_bytes=VMEM_LIMIT)


D_HALF = D_MODEL // 2
U32 = jnp.uint32


def _pack_halves(x):
    lo = lax.bitcast_convert_type(x[:, :D_HALF].astype(BF16).astype(F32), U32)
    hi = lax.bitcast_convert_type(x[:, D_HALF:].astype(BF16).astype(F32), U32)
    return (lo >> 16) | hi


def _unpack_halves(w):
    lo = lax.bitcast_convert_type(w << 16, F32)
    hi = lax.bitcast_convert_type(w & jnp.uint32(0xFFFF0000), F32)
    return lo, hi


def _inproj_body(x_ref, g_ref, w_ref, wpe_ref, out_ref, kpe_ref, xn_ref):
    @pl.when(pl.program_id(1) == 0)
    def _():
        x = x_ref[...]
        ms = jnp.mean(x * x, axis=-1, keepdims=True)
        xn = (x * lax.rsqrt(ms + EPS) * g_ref[...]).astype(BF16)
        xn_ref[...] = xn
        kpe_ref[...] = jnp.dot(xn, wpe_ref[...], preferred_element_type=F32)

    out_ref[...] = jnp.dot(xn_ref[...], w_ref[...], preferred_element_type=F32).astype(BF16)


def _in_proj(x2d, ln1_g, w_main, w_pe, tm, tn):
    t = x2d.shape[0]
    return pl.pallas_call(
        _inproj_body,
        grid=(t // tm, PROJ_W // tn),
        in_specs=[
            pl.BlockSpec((tm, D_MODEL), lambda i, j: (i, 0)),
            pl.BlockSpec((1, D_MODEL), lambda i, j: (0, 0)),
            pl.BlockSpec((D_MODEL, tn), lambda i, j: (0, j)),
            pl.BlockSpec((D_MODEL, LANES), lambda i, j: (0, 0)),
        ],
        out_specs=[
            pl.BlockSpec((tm, tn), lambda i, j: (i, j)),
            pl.BlockSpec((tm, LANES), lambda i, j: (i, 0)),
        ],
        out_shape=[
            jax.ShapeDtypeStruct((t, PROJ_W), BF16),
            jax.ShapeDtypeStruct((t, LANES), F32),
        ],
        scratch_shapes=[pltpu.VMEM((tm, D_MODEL), BF16)],
        compiler_params=_cparams(("arbitrary", "arbitrary")),
        name="in_proj",
    )(x2d, ln1_g, w_main, w_pe)


def _qkv_body(ql_ref, kvl_ref, kpe_ref, cosk_ref, sink_ref, cost_ref, sint_ref,
              gqa_ref, gkva_ref, gq_ref, gkn_ref, gkr_ref, wq_ref, wkn_ref, wv_ref,
              qt_ref, k_ref, vt_ref):
    def latent_norm(ref, g_ref):
        v = ref[...].astype(F32)
        ms = jnp.mean(v * v, axis=-1, keepdims=True)
        return (v * lax.rsqrt(ms + EPS) * g_ref[...]).astype(BF16)

    qn = latent_norm(ql_ref, gqa_ref)
    kvn = latent_norm(kvl_ref, gkva_ref)
    tm = qn.shape[0]

    kn = jnp.dot(kvn, wkn_ref[...], preferred_element_type=F32)
    kpe = kpe_ref[...]
    ss_pe = jnp.sum(kpe * kpe, axis=-1, keepdims=True)
    kr = kpe * gkr_ref[...]
    kr = kr * cosk_ref[...] + pltpu.roll(kr, 2 * HALF_ROPE, axis=1) * sink_ref[...]
    gkn = gkn_ref[...]
    for h in range(N_HEADS):
        kh = kn[:, h * QK_NOPE:(h + 1) * QK_NOPE]
        ss = jnp.sum(kh * kh, axis=-1, keepdims=True) + ss_pe
        r = lax.rsqrt(ss * (1.0 / QK_DIM) + EPS)
        k_ref[0, h, :, 0:QK_NOPE] = (kh * gkn * r).astype(BF16)
        k_ref[0, h, :, QK_NOPE:QK_PAD] = (kr * r).astype(BF16)

    vt = lax.dot_general(wv_ref[...], kvn, _NT, preferred_element_type=F32)
    for h in range(N_HEADS):
        vt_ref[0, h] = vt[h * V_DIM:(h + 1) * V_DIM, :].astype(BF16)

    cost = cost_ref[...]
    sint = sint_ref[...]
    gq = gq_ref[...]
    zeros = jnp.zeros((HALF_ROPE, tm), BF16)
    for h in range(N_HEADS):
        qt = lax.dot_general(wq_ref[h * QK_PAD:(h + 1) * QK_PAD, :], qn, _NT,
                             preferred_element_type=F32)
        ss = jnp.sum(qt * qt, axis=0, keepdims=True)
        r = lax.rsqrt(ss * (1.0 / QK_DIM) + EPS)
        qs = qt * gq * r
        x1 = qs[QK_NOPE:QK_NOPE + HALF_ROPE]
        x2 = qs[QK_NOPE + 2 * HALF_ROPE:QK_NOPE + 3 * HALF_ROPE]
        qt_ref[0, h, 0:QK_NOPE, :] = qs[0:QK_NOPE].astype(BF16)
        qt_ref[0, h, QK_NOPE:QK_NOPE + HALF_ROPE, :] = (x1 * cost - x2 * sint).astype(BF16)
        qt_ref[0, h, QK_NOPE + HALF_ROPE:QK_NOPE + 2 * HALF_ROPE, :] = zeros
        qt_ref[0, h, QK_NOPE + 2 * HALF_ROPE:QK_NOPE + 3 * HALF_ROPE, :] = (
            x1 * sint + x2 * cost).astype(BF16)
        qt_ref[0, h, QK_NOPE + 3 * HALF_ROPE:QK_PAD, :] = zeros


def _qkv(proj, kpe, tabs, wts, b, s, tm):
    ns = s // tm
    cos_k, sin_k, cos_t, sin_t = tabs
    gqa, gkva, gq, gkn, gkr, wq_t, wkn, wv_t = wts
    const = lambda shape: pl.BlockSpec(shape, lambda bi, i: (0,) * len(shape))
    return pl.pallas_call(
        _qkv_body,
        grid=(b, ns),
        in_specs=[
            pl.BlockSpec((tm, Q_LORA), lambda bi, i: (bi * ns + i, 3 * CONV_CH // Q_LORA)),
            pl.BlockSpec((tm, KV_LORA), lambda bi, i: (bi * ns + i, 3 * CONV_CH // KV_LORA + 1)),
            pl.BlockSpec((tm, LANES), lambda bi, i: (bi * ns + i, 0)),
            pl.BlockSpec((tm, LANES), lambda bi, i: (i, 0)),
            pl.BlockSpec((tm, LANES), lambda bi, i: (i, 0)),
            pl.BlockSpec((HALF_ROPE, tm), lambda bi, i: (0, i)),
            pl.BlockSpec((HALF_ROPE, tm), lambda bi, i: (0, i)),
            const((1, Q_LORA)), const((1, KV_LORA)), const((QK_PAD, 1)),
            const((1, LANES)), const((1, LANES)),
            const((N_HEADS * QK_PAD, Q_LORA)),
            const((KV_LORA, N_HEADS * QK_NOPE)),
            const((N_HEADS * V_DIM, KV_LORA)),
        ],
        out_specs=[
            pl.BlockSpec((1, N_HEADS, QK_PAD, tm), lambda bi, i: (bi, 0, 0, i)),
            pl.BlockSpec((1, N_HEADS, tm, QK_PAD), lambda bi, i: (bi, 0, i, 0)),
            pl.BlockSpec((1, N_HEADS, V_DIM, tm), lambda bi, i: (bi, 0, 0, i)),
        ],
        out_shape=[
            jax.ShapeDtypeStruct((b, N_HEADS, QK_PAD, s), BF16),
            jax.ShapeDtypeStruct((b, N_HEADS, s, QK_PAD), BF16),
            jax.ShapeDtypeStruct((b, N_HEADS, V_DIM, s), BF16),
        ],
        compiler_params=_cparams(("arbitrary", "arbitrary")),
        name="qkv",
    )(proj, proj, kpe, cos_k, sin_k, cos_t, sin_t, gqa, gkva, gq, gkn, gkr, wq_t, wkn, wv_t)


def _attn_body(qt_ref, k_ref, vt_ref, o_ref, s_a, m_a, s_b, m_b, *, kc):
    n = pl.program_id(0)
    s_len = k_ref.shape[2]
    chunks = [(c * kc, (c + 1) * kc) for c in range(s_len // kc)]

    @pl.when(n == 0)
    def _():
        s_b[...] = jnp.zeros_like(s_b)
        m_b[...] = jnp.zeros_like(m_b)

    def step(s_w, m_w, s_r, m_r):
        m_prev = m_r[...]
        qt = qt_ref[0, 0]
        l = None
        acc = None
        m = None
        for lo, hi in chunks:
            p = jnp.exp2(s_r[lo:hi, :] - m_prev)
            lc = jnp.sum(p, axis=0, keepdims=True)
            l = lc if l is None else l + lc
            pv = jnp.dot(vt_ref[0, 0, :, lo:hi], p.astype(BF16), preferred_element_type=F32)
            acc = pv if acc is None else acc + pv

            sc = jnp.dot(k_ref[0, 0, lo:hi, :], qt, preferred_element_type=F32)
            s_w[lo:hi, :] = sc
            mc = jnp.max(sc, axis=0, keepdims=True)
            m = mc if m is None else jnp.maximum(m, mc)
        o_ref[0] = (acc * (1.0 / l)).T.astype(BF16)
        m_w[...] = m

    @pl.when(n % 2 == 0)
    def _():
        step(s_a, m_a, s_b, m_b)

    @pl.when(n % 2 == 1)
    def _():
        step(s_b, m_b, s_a, m_a)


def _attention(qt, k, vt, tq, kc):
    b, _, _, s = qt.shape
    nq = s // tq
    n_tiles = b * N_HEADS * nq

    def bhi(tile):
        return tile // (N_HEADS * nq), (tile // nq) % N_HEADS, tile % nq

    def score_tile(n):
        return bhi(jnp.minimum(n, n_tiles - 1))

    def value_tile(n):
        return bhi(jnp.maximum(n - 1, 0))

    def qt_map(n):
        bi, h, i = score_tile(n)
        return bi, h, 0, i

    def k_map(n):
        bi, h, _ = score_tile(n)
        return bi, h, 0, 0

    def vt_map(n):
        bi, h, _ = value_tile(n)
        return bi, h, 0, 0

    def o_map(n):
        bi, h, i = value_tile(n)
        return bi, i, h

    return pl.pallas_call(
        functools.partial(_attn_body, kc=kc),
        grid=(n_tiles + 1,),
        in_specs=[
            pl.BlockSpec((1, 1, QK_PAD, tq), qt_map),
            pl.BlockSpec((1, 1, s, QK_PAD), k_map),
            pl.BlockSpec((1, 1, V_DIM, s), vt_map),
        ],
        out_specs=pl.BlockSpec((1, tq, V_DIM), o_map),
        out_shape=jax.ShapeDtypeStruct((b, s, N_HEADS * V_DIM), BF16),
        scratch_shapes=[pltpu.VMEM((s, tq), F32), pltpu.VMEM((1, tq), F32),
                        pltpu.VMEM((s, tq), F32), pltpu.VMEM((1, tq), F32)],
        compiler_params=_cparams(("arbitrary",)),
        name="attention",
    )(qt, k, vt)


def _merge_body(u_ref, gb_ref, gc_ref, up_ref, gcp_ref, un_ref, gcn_ref, o_ref, gtc_ref, gta_ref,
                cw_ref, wc_ref, wa_ref, out_ref):
    i = pl.program_id(1)
    last = pl.num_programs(1) - 1
    halo = up_ref.shape[0]
    v = gc_ref[...].astype(F32) * u_ref[...].astype(F32)
    tm = v.shape[0]
    v_before = gcp_ref[halo - 1:halo, :].astype(F32) * up_ref[halo - 1:halo, :].astype(F32)
    v_after = gcn_ref[0:1, :].astype(F32) * un_ref[0:1, :].astype(F32)
    v_before = jnp.where(i == 0, 0.0, v_before)
    v_after = jnp.where(i == last, 0.0, v_after)
    row = lax.broadcasted_iota(jnp.int32, (tm, 1), 0)
    v_prev = jnp.where(row == 0, v_before, pltpu.roll(v, 1, axis=0))
    v_next = jnp.where(row == tm - 1, v_after, pltpu.roll(v, tm - 1, axis=0))
    cw = cw_ref[...]
    conv = v_prev * cw[0:1, :] + v * cw[1:2, :] + v_next * cw[2:3, :]
    z = (gb_ref[...].astype(F32) * conv).astype(BF16)
    conv_out = jnp.dot(z, wc_ref[...], preferred_element_type=F32)
    attn_out = jnp.dot(o_ref[...], wa_ref[...], preferred_element_type=F32)
    merged = (jax.nn.sigmoid(gtc_ref[...].astype(F32)) * conv_out
              + jax.nn.sigmoid(gta_ref[...].astype(F32)) * attn_out)
    out_ref[...] = merged.astype(BF16)


def _merge(proj, o2d, conv_w, wc, wa, b, s, tm, halo):
    ns = s // tm
    t = b * s
    hb = tm // halo
    nh = t // halo
    row = lambda bi, i: bi * ns + i
    prev = lambda bi, i: (jnp.maximum(row(bi, i) * hb - 1, 0))
    nxt = lambda bi, i: (jnp.minimum((row(bi, i) + 1) * hb, nh - 1))
    gate0 = KPE_OFF // D_MODEL
    return pl.pallas_call(
        _merge_body,
        grid=(b, ns),
        in_specs=[
            pl.BlockSpec((tm, CONV_CH), lambda bi, i: (row(bi, i), 0)),
            pl.BlockSpec((tm, CONV_CH), lambda bi, i: (row(bi, i), 1)),
            pl.BlockSpec((tm, CONV_CH), lambda bi, i: (row(bi, i), 2)),
            pl.BlockSpec((halo, CONV_CH), lambda bi, i: (prev(bi, i), 0)),
            pl.BlockSpec((halo, CONV_CH), lambda bi, i: (prev(bi, i), 2)),
            pl.BlockSpec((halo, CONV_CH), lambda bi, i: (nxt(bi, i), 0)),
            pl.BlockSpec((halo, CONV_CH), lambda bi, i: (nxt(bi, i), 2)),
            pl.BlockSpec((tm, D_MODEL), lambda bi, i: (row(bi, i), 0)),
            pl.BlockSpec((tm, D_MODEL), lambda bi, i: (row(bi, i), gate0)),
            pl.BlockSpec((tm, D_MODEL), lambda bi, i: (row(bi, i), gate0 + 1)),
            pl.BlockSpec((3, CONV_CH), lambda bi, i: (0, 0)),
            pl.BlockSpec((CONV_CH, D_MODEL), lambda bi, i: (0, 0)),
            pl.BlockSpec((D_MODEL, D_MODEL), lambda bi, i: (0, 0)),
        ],
        out_specs=pl.BlockSpec((tm, D_MODEL), lambda bi, i: (row(bi, i), 0)),
        out_shape=jax.ShapeDtypeStruct((t, D_MODEL), BF16),
        compiler_params=_cparams(("arbitrary", "arbitrary")),
        name="merge",
    )(proj, proj, proj, proj, proj, proj, proj, o2d, proj, proj, conv_w, wc, wa)


def _out_route_body(xa_ref, mga_ref, xb_ref, mgb_ref, wo_ref, g2_ref, wr_ref, br_ref,
                    h_ref, xn_ref, route_ref, routet_ref, cnt_ref, base_ref, logit_ref, *, steps_a):
    step = pl.program_id(0)
    tm = h_ref.shape[0]
    slot = step % 2

    @pl.when(step == 0)
    def _():
        base_ref[...] = jnp.zeros_like(base_ref)
        logit_ref[...] = jnp.zeros_like(logit_ref)

    in_a = step < steps_a
    mg = jnp.where(in_a, mga_ref[...], mgb_ref[...])
    n_chunks = 4
    cw = D_MODEL // n_chunks

    def h_chunk(c):
        x = jnp.where(in_a, xa_ref[:, c * cw:(c + 1) * cw], xb_ref[:, c * cw:(c + 1) * cw])
        return x + jnp.dot(mg, wo_ref[:, c * cw:(c + 1) * cw], preferred_element_type=F32)

    logits = logit_ref[1 - slot]
    lane = lax.broadcasted_iota(jnp.int32, (tm, LANES), 1)
    neg = -jnp.inf
    lg = jnp.where(lane < N_GROUPS, logits, neg)
    gmax = jnp.max(lg, axis=-1, keepdims=True)
    g_p = 1.0 / jnp.sum(jnp.exp(lg - gmax), axis=-1, keepdims=True)
    g_sel = jnp.min(jnp.where(lg == gmax, lane, LANES), axis=-1, keepdims=True)
    h_parts = [h_chunk(0)]

    lo_lane = EXP_LANE0 + g_sel * EXP_PER_GROUP
    le = jnp.where((lane >= lo_lane) & (lane < lo_lane + EXP_PER_GROUP), logits, neg)
    m1 = jnp.max(le, axis=-1, keepdims=True)
    i1 = jnp.min(jnp.where(le == m1, lane, LANES), axis=-1, keepdims=True)
    le2 = jnp.where(lane == i1, neg, le)
    m2 = jnp.max(le2, axis=-1, keepdims=True)
    i2 = jnp.min(jnp.where(le2 == m2, lane, LANES), axis=-1, keepdims=True)
    e2 = jnp.exp(m2 - m1)
    gate1 = g_p / (1.0 + e2)
    gate2 = g_p * e2 / (1.0 + e2)
    h_parts.append(h_chunk(1))

    sel1 = lane == i1
    sel2 = lane == i2
    onehot = jnp.where(sel1 | sel2, 1.0, 0.0)
    r_i = lax.broadcasted_iota(jnp.int32, (tm, tm), 0)
    c_i = lax.broadcasted_iota(jnp.int32, (tm, tm), 1)
    lower = jnp.where(r_i > c_i, 1.0, 0.0).astype(BF16)
    before = jnp.dot(lower, onehot.astype(BF16), preferred_element_type=F32) + base_ref[...]
    rank1 = jnp.sum(jnp.where(sel1, before, 0.0), axis=-1, keepdims=True)
    rank2 = jnp.sum(jnp.where(sel2, before, 0.0), axis=-1, keepdims=True)
    counted = jnp.where(step > 0, 1.0, 0.0)
    base_ref[...] = base_ref[...] + counted * jnp.sum(onehot, axis=0, keepdims=True)
    cnt_ref[...] = base_ref[...]
    h_parts.append(h_chunk(2))

    e1f = (i1 - EXP_LANE0).astype(F32)
    e2f = (i2 - EXP_LANE0).astype(F32)
    cols = (e1f, e2f, gate1, gate2, rank1, rank2)
    route = jnp.zeros((tm, LANES), F32)
    for c, val in enumerate(cols):
        route = jnp.where(lane == c, val, route)
    route_ref[...] = route
    routet_ref[...] = route.T[0:8, :]
    h_parts.append(h_chunk(3))

    h = jnp.concatenate(h_parts, axis=1)
    h_ref[...] = h
    ms = jnp.mean(h * h, axis=-1, keepdims=True)
    xn = h * lax.rsqrt(ms + EPS) * g2_ref[...]
    xn_ref[...] = _pack_halves(xn)

    hi = xn.astype(BF16)
    lo = (xn - hi.astype(F32)).astype(BF16)
    w_hl = wr_ref[...]
    both = jnp.dot(hi, w_hl, preferred_element_type=F32)
    logit_ref[slot] = (both[:, :LANES] + both[:, LANES:]
                       + jnp.dot(lo, w_hl[:, :LANES], preferred_element_type=F32)) + br_ref[...]


def _out_route(xa, mga, xb, mgb, wo, ln2_g, wr_hl, br, tm):
    steps_a, steps_b = xa.shape[0] // tm, xb.shape[0] // tm
    t = xa.shape[0] + xb.shape[0]
    const = lambda shape: pl.BlockSpec(shape, lambda i: (0,) * len(shape))
    n_tiles = steps_a + steps_b
    seg_a = pl.BlockSpec((tm, D_MODEL), lambda i: (jnp.minimum(i, steps_a - 1), 0))
    seg_b = pl.BlockSpec((tm, D_MODEL), lambda i: (jnp.clip(i - steps_a, 0, steps_b - 1), 0))
    this_tile = lambda i: jnp.minimum(i, n_tiles - 1)
    prev_tile = lambda i: jnp.maximum(i - 1, 0)
    return pl.pallas_call(
        functools.partial(_out_route_body, steps_a=steps_a),
        grid=(n_tiles + 1,),
        in_specs=[
            seg_a, seg_a, seg_b, seg_b,
            const((D_MODEL, D_MODEL)), const((1, D_MODEL)),
            const((D_MODEL, 2 * LANES)), const((1, LANES)),
        ],
        out_specs=[
            pl.BlockSpec((tm, D_MODEL), lambda i: (this_tile(i), 0)),
            pl.BlockSpec((tm, D_HALF), lambda i: (this_tile(i), 0)),
            pl.BlockSpec((tm, LANES), lambda i: (prev_tile(i), 0)),
            pl.BlockSpec((8, tm), lambda i: (0, prev_tile(i))),
            const((1, LANES)),
        ],
        out_shape=[
            jax.ShapeDtypeStruct((t, D_MODEL), F32),
            jax.ShapeDtypeStruct((t, D_HALF), U32),
            jax.ShapeDtypeStruct((t, LANES), F32),
            jax.ShapeDtypeStruct((8, t), F32),
            jax.ShapeDtypeStruct((1, LANES), F32),
        ],
        scratch_shapes=[pltpu.VMEM((1, LANES), F32), pltpu.VMEM((2, tm, LANES), F32)],
        compiler_params=_cparams(("arbitrary",)),
        name="out_route",
    )(xa, mga, xb, mgb, wo, ln2_g, wr_hl, br)


def _block_meta(next_expert, slot, first):
    return (next_expert + 1) * 4 + slot * 2 + first


def _plan_body(cnt_ref, routet_ref, pos_ref, start_ref, blk_exp_ref, blk_src_ref, blk_meta_ref,
               nvalid_ref, next_ref):
    n_blocks = blk_exp_ref.shape[0]

    def backwards(k, following):
        e = N_EXPERTS - 1 - k
        next_ref[e] = following
        return jnp.where(cnt_ref[EXP_LANE0 + e] > 0, e, following)

    lax.fori_loop(0, N_EXPERTS, backwards, -1)

    def per_expert(e, carry):
        acc, ordinal = carry
        c = cnt_ref[EXP_LANE0 + e]
        nb = (c + ROW_BLOCK - 1) // ROW_BLOCK
        start_ref[e] = acc * ROW_BLOCK

        def fill(j, inner):
            blk_exp_ref[acc + j] = e
            blk_src_ref[acc + j] = acc + j
            blk_meta_ref[acc + j] = _block_meta(next_ref[e], ordinal % 2, jnp.where(j == 0, 1, 0))
            return inner

        lax.fori_loop(0, nb, fill, 0)
        return acc + nb, ordinal + jnp.where(nb > 0, 1, 0)

    nvalid, _ = lax.fori_loop(0, N_EXPERTS, per_expert, (0, 0))
    nvalid_ref[0] = nvalid
    last_exp = blk_exp_ref[nvalid - 1]

    def tail(j, carry):
        blk_exp_ref[j] = last_exp
        blk_src_ref[j] = nvalid - 1
        blk_meta_ref[j] = 0
        return carry

    lax.fori_loop(nvalid, n_blocks, tail, 0)

    ef = routet_ref[0:2, :]
    off = jnp.zeros_like(ef)
    for e in range(N_EXPERTS):
        off = jnp.where(ef == float(e), start_ref[e].astype(F32), off)
    pos_ref[...] = (off + routet_ref[4:6, :]).astype(jnp.int32)


def _plan(counts_i32, routet, n_blocks):
    t = routet.shape[1]
    smem = lambda: pl.BlockSpec(memory_space=pltpu.SMEM)
    return pl.pallas_call(
        _plan_body,
        in_specs=[smem(), pl.BlockSpec(memory_space=pltpu.VMEM)],
        out_specs=[pl.BlockSpec(memory_space=pltpu.VMEM), smem(), smem(), smem(), smem(), smem()],
        out_shape=[
            jax.ShapeDtypeStruct((2, t), jnp.int32),
            jax.ShapeDtypeStruct((N_EXPERTS,), jnp.int32),
            jax.ShapeDtypeStruct((n_blocks,), jnp.int32),
            jax.ShapeDtypeStruct((n_blocks,), jnp.int32),
            jax.ShapeDtypeStruct((n_blocks,), jnp.int32),
            jax.ShapeDtypeStruct((1,), jnp.int32),
        ],
        scratch_shapes=[pltpu.SMEM((N_EXPERTS,), jnp.int32)],
        compiler_params=pltpu.CompilerParams(vmem_limit_bytes=VMEM_LIMIT),
        name="plan",
    )(counts_i32, routet)


def _dispatch_body(pos0_ref, pos1_ref, cnt_ref, start_ref, nvalid_ref, xn_ref, xs_hbm, zero_ref, sem):
    tm = xn_ref.shape[0]
    n_blocks = xs_hbm.shape[0] // ROW_BLOCK

    def put(src_ref, src_row, dst_row):
        return pltpu.make_async_copy(src_ref.at[pl.ds(src_row, 1)], xs_hbm.at[pl.ds(dst_row, 1)], sem)

    def put_block(blk):
        return pltpu.make_async_copy(zero_ref, xs_hbm.at[pl.ds(blk * ROW_BLOCK, ROW_BLOCK)], sem)

    @pl.when(pl.program_id(0) == 0)
    def _():
        zero_ref[...] = jnp.zeros_like(zero_ref)

        def tail(blk, carry):
            put_block(blk).start()
            return carry

        def tail_done(blk, carry):
            put_block(0).wait()
            return carry

        lax.fori_loop(nvalid_ref[0], n_blocks, tail, 0)

        def pad_rows(wait):
            def per_expert(e, carry):
                c = cnt_ref[EXP_LANE0 + e]
                end = (c + ROW_BLOCK - 1) // ROW_BLOCK * ROW_BLOCK
                base = start_ref[e]

                def fill(r, inner):
                    if wait:
                        put(zero_ref, 0, 0).wait()
                    else:
                        put(zero_ref, 0, base + r).start()
                    return inner

                lax.fori_loop(c, end, fill, 0)
                return carry

            lax.fori_loop(0, N_EXPERTS, per_expert, 0)

        pad_rows(wait=False)
        lax.fori_loop(nvalid_ref[0], n_blocks, tail_done, 0)
        pad_rows(wait=True)

    for t in range(tm):
        put(xn_ref, t, pos0_ref[t]).start()
        put(xn_ref, t, pos1_ref[t]).start()

    for t in range(2 * tm):
        put(xn_ref, 0, 0).wait()


def _dispatch(pos0, pos1, counts_i32, starts, nvalid, xn2, n_rows, tm):
    t = xn2.shape[0]
    smem = lambda: pl.BlockSpec(memory_space=pltpu.SMEM)
    pos_spec = lambda: pl.BlockSpec((tm,), lambda i: (i,), memory_space=pltpu.SMEM)
    return pl.pallas_call(
        _dispatch_body,
        grid=(t // tm,),
        in_specs=[
            pos_spec(), pos_spec(),
            smem(), smem(), smem(),
            pl.BlockSpec((tm, D_HALF), lambda i: (i, 0)),
        ],
        out_specs=pl.BlockSpec(memory_space=pl.ANY),
        out_shape=jax.ShapeDtypeStruct((n_rows, D_HALF), U32),
        scratch_shapes=[pltpu.VMEM((ROW_BLOCK, D_HALF), U32), pltpu.SemaphoreType.DMA(())],
        compiler_params=pltpu.CompilerParams(dimension_semantics=("arbitrary",),
                                             vmem_limit_bytes=VMEM_LIMIT, has_side_effects=True),
        name="dispatch",
    )(pos0, pos1, counts_i32, starts, nvalid, xn2)


def _expert_body(blk_exp_ref, blk_src_ref, blk_meta_ref, nvalid_ref, xs_ref, wg_hbm, wu_hbm, wd_hbm,
                 ys_ref, wg_buf, wu_buf, wd_buf, sems):
    i = pl.program_id(0)
    valid = i < nvalid_ref[0]
    meta = blk_meta_ref[i]
    first = meta % 2
    slot = (meta // 2) % 2
    next_expert = meta // 4 - 1

    def weight_copies(expert, to_slot):
        return [pltpu.make_async_copy(hbm.at[expert], buf.at[to_slot], sems.at[to_slot])
                for hbm, buf in ((wg_hbm, wg_buf), (wu_hbm, wu_buf), (wd_hbm, wd_buf))]

    @pl.when(i == 0)
    def _():
        for cp in weight_copies(blk_exp_ref[0], 0):
            cp.start()

    @pl.when(first == 1)
    def _():
        for cp in weight_copies(blk_exp_ref[i], slot):
            cp.wait()

        @pl.when(next_expert >= 0)
        def _():
            for cp in weight_copies(next_expert, 1 - slot):
                cp.start()

    @pl.when(valid)
    def _():
        lo, hi = _unpack_halves(xs_ref[...])
        xb = jnp.concatenate([lo.astype(BF16), hi.astype(BF16)], axis=1)
        g = jnp.dot(xb, wg_buf[slot].astype(BF16), preferred_element_type=F32)
        u = jnp.dot(xb, wu_buf[slot].astype(BF16), preferred_element_type=F32)
        hmid = (g * jax.nn.sigmoid(g) * u).astype(BF16)
        ys_ref[...] = _pack_halves(jnp.dot(hmid, wd_buf[slot].astype(BF16), preferred_element_type=F32))

    @pl.when(jnp.logical_not(valid))
    def _():
        ys_ref[...] = jnp.zeros_like(ys_ref)


def _experts(blk_exp, blk_src, blk_meta, nvalid, xs, wg, wu, wd):
    n_rows = xs.shape[0]
    n_blocks = n_rows // ROW_BLOCK
    hbm = lambda: pl.BlockSpec(memory_space=pl.ANY)
    grid_spec = pltpu.PrefetchScalarGridSpec(
        num_scalar_prefetch=4,
        grid=(n_blocks,),
        in_specs=[
            pl.BlockSpec((ROW_BLOCK, D_HALF), lambda i, be, bs, bm, nv: (bs[i], 0)),
            hbm(), hbm(), hbm(),
        ],
        out_specs=pl.BlockSpec((ROW_BLOCK, D_HALF), lambda i, be, bs, bm, nv: (i, 0)),
        scratch_shapes=[
            pltpu.VMEM((2, D_MODEL, D_EXPERT), F32),
            pltpu.VMEM((2, D_MODEL, D_EXPERT), F32),
            pltpu.VMEM((2, D_EXPERT, D_MODEL), F32),
            pltpu.SemaphoreType.DMA((2,)),
        ],
    )
    return pl.pallas_call(
        _expert_body,
        grid_spec=grid_spec,
        out_shape=jax.ShapeDtypeStruct((n_rows, D_HALF), U32),
        compiler_params=_cparams(("arbitrary",)),
        name="experts",
    )(blk_exp, blk_src, blk_meta, nvalid, xs, wg, wu, wd)


def _combine_body(pos0_ref, pos1_ref, pos0_next_ref, pos1_next_ref, ys_hbm, h_ref, route_ref,
                  ya_ref, yb_ref, buf_ref, sems, *, steps_a):
    n = pl.program_id(0)
    tm = h_ref.shape[0]
    slot = n % 2

    def row_copy(row, k, t, to_slot):
        return pltpu.make_async_copy(ys_hbm.at[pl.ds(row, 1)],
                                     buf_ref.at[to_slot, k, pl.ds(t, 1)], sems.at[to_slot])

    def gather(p0_ref, p1_ref, to_slot):
        for t in range(tm):
            row_copy(p0_ref[t], 0, t, to_slot).start()
            row_copy(p1_ref[t], 1, t, to_slot).start()

    @pl.when(n == 0)
    def _():
        gather(pos0_ref, pos1_ref, 0)

    @pl.when(n + 1 < pl.num_programs(0))
    def _():
        gather(pos0_next_ref, pos1_next_ref, 1 - slot)

    for t in range(tm):
        row_copy(0, 0, t, slot).wait()
        row_copy(0, 1, t, slot).wait()
    route = route_ref[...]
    lo1, hi1 = _unpack_halves(buf_ref[slot, 0])
    lo2, hi2 = _unpack_halves(buf_ref[slot, 1])
    g1, g2 = route[:, 2:3], route[:, 3:4]
    y_lo = h_ref[:, :D_HALF] + g1 * lo1 + g2 * lo2
    y_hi = h_ref[:, D_HALF:] + g1 * hi1 + g2 * hi2

    @pl.when(n < steps_a)
    def _():
        ya_ref[:, :D_HALF] = y_lo
        ya_ref[:, D_HALF:] = y_hi

    @pl.when(n >= steps_a)
    def _():
        yb_ref[:, :D_HALF] = y_lo
        yb_ref[:, D_HALF:] = y_hi


def _combine(pos0, pos1, ys, h, route, t_a, tm):
    t = h.shape[0]
    nt = t // tm
    steps_a = t_a // tm
    cur = lambda: pl.BlockSpec((tm,), lambda i: (i,), memory_space=pltpu.SMEM)
    nxt = lambda: pl.BlockSpec((tm,), lambda i: (jnp.minimum(i + 1, nt - 1),), memory_space=pltpu.SMEM)
    return pl.pallas_call(
        functools.partial(_combine_body, steps_a=steps_a),
        grid=(nt,),
        in_specs=[
            cur(), cur(), nxt(), nxt(),
            pl.BlockSpec(memory_space=pl.ANY),
            pl.BlockSpec((tm, D_MODEL), lambda i: (i, 0)),
            pl.BlockSpec((tm, LANES), lambda i: (i, 0)),
        ],
        out_specs=[
            pl.BlockSpec((tm, D_MODEL), lambda i: (jnp.minimum(i, steps_a - 1), 0)),
            pl.BlockSpec((tm, D_MODEL), lambda i: (jnp.maximum(i - steps_a, 0), 0)),
        ],
        out_shape=[
            jax.ShapeDtypeStruct((t_a, D_MODEL), F32),
            jax.ShapeDtypeStruct((t - t_a, D_MODEL), F32),
        ],
        scratch_shapes=[pltpu.VMEM((2, 2, tm, D_HALF), U32), pltpu.SemaphoreType.DMA((2,))],
        compiler_params=_cparams(("arbitrary",)),
        name="combine",
    )(pos0, pos1, pos0, pos1, ys, h, route)


def _pad_rope(a, axis):
    x1, x2 = jnp.split(a, 2, axis=axis)
    z = jnp.zeros_like(x1)
    return jnp.concatenate([x1, z, x2, z], axis=axis)


def _prepare(ln1_g, w_in, conv_w, q_a_norm_g, w_uq, kv_a_norm_g, w_ukv, q_norm_g, k_norm_g,
             w_conv_out, w_attn_out, w_out, ln2_g, w_router_group, b_router_group,
             w_router_exp, b_router_exp, w_gate, w_up, w_down):
    w_in0 = w_in[0]
    w_main = jnp.concatenate([w_in0[:, :KPE_OFF], w_in0[:, KPE_OFF + QK_ROPE:]], axis=1).astype(BF16)
    w_pe = _pad_rope(w_in0[:, KPE_OFF:KPE_OFF + QK_ROPE], 1).astype(BF16)

    wq = w_uq[0].reshape(Q_LORA, N_HEADS, QK_DIM)
    wq = jnp.concatenate([wq[:, :, :QK_NOPE], _pad_rope(wq[:, :, QK_NOPE:], 2)], axis=2)
    wq_t = wq.reshape(Q_LORA, N_HEADS * QK_PAD).T.astype(BF16)
    wkv = w_ukv[0].reshape(KV_LORA, N_HEADS, QK_NOPE + V_DIM)
    wkn = wkv[:, :, :QK_NOPE].reshape(KV_LORA, N_HEADS * QK_NOPE).astype(BF16)
    wv_t = wkv[:, :, QK_NOPE:].reshape(KV_LORA, N_HEADS * V_DIM).T.astype(BF16)

    qg = q_norm_g[0]
    score_scale = QK_DIM ** -0.5 * math.log2(math.e)
    gq = (jnp.concatenate([qg[:QK_NOPE], _pad_rope(qg[QK_NOPE:], 0)]) * score_scale).reshape(QK_PAD, 1)
    kg = k_norm_g[0]
    gkn = kg[:QK_NOPE].reshape(1, LANES)
    gkr = _pad_rope(kg[QK_NOPE:], 0).reshape(1, LANES)

    wr = jnp.concatenate([w_router_group[0], w_router_exp[0],
                          jnp.zeros((D_MODEL, LANES - N_GROUPS - N_EXPERTS), F32)], axis=1)
    wr_hi = wr.astype(BF16)
    wr_hl = jnp.concatenate([wr_hi, (wr - wr_hi.astype(F32)).astype(BF16)], axis=1)
    br = jnp.concatenate([b_router_group[0], b_router_exp[0],
                          jnp.zeros((LANES - N_GROUPS - N_EXPERTS,), F32)]).reshape(1, LANES)
    return dict(
        ln1_g=ln1_g, w_main=w_main, w_pe=w_pe, conv_w=conv_w[0],
        qkv=(q_a_norm_g, kv_a_norm_g, gq, gkn, gkr, wq_t, wkn, wv_t),
        wc=w_conv_out[0].astype(BF16), wa=w_attn_out[0].astype(BF16), wo=w_out[0].astype(BF16),
        ln2_g=ln2_g, wr_hl=wr_hl, br=br,
        wg=w_gate[0], wu=w_up[0], wd=w_down[0],
    )


def _rope_tables(s):
    inv = ROPE_THETA ** (-jnp.arange(0, QK_ROPE, 2, dtype=F32) / QK_ROPE)
    ang = jnp.arange(s, dtype=F32)[:, None] * inv[None, :]
    cos, sin = jnp.cos(ang), jnp.sin(ang)
    z = jnp.zeros_like(cos)
    cos_k = jnp.concatenate([cos, z, cos, z], axis=1)
    sin_k = jnp.concatenate([-sin, z, sin, z], axis=1)
    return cos_k, sin_k, cos.T, sin.T


def _tiles(b, s):
    t = b * s
    pick = lambda n, pref: pref if n % pref == 0 else n
    return dict(
        in_tm=pick(t, 512), in_tn=2048,
        qkv_tm=pick(s, 256),
        attn_tq=pick(s, 1024), attn_kc=pick(s, 256),
        merge_tm=pick(s, 256), halo=16,
    )


def _moe_tiles(t_a, t_b):
    both = math.gcd(t_a, t_b)
    pick = lambda pref: pref if both % pref == 0 else both
    return dict(route_tm=pick(512), disp_tm=pick(1024), comb_tm=pick(512))


def _mixer(x, p):
    b, s, _ = x.shape
    t = b * s
    tl = _tiles(b, s)
    x2d = x.reshape(t, D_MODEL)
    proj, kpe = _in_proj(x2d, p["ln1_g"], p["w_main"], p["w_pe"], tl["in_tm"], tl["in_tn"])
    qt, k, vt = _qkv(proj, kpe, _rope_tables(s), p["qkv"], b, s, tl["qkv_tm"])
    o = _attention(qt, k, vt, tl["attn_tq"], tl["attn_kc"])
    merged = _merge(proj, o.reshape(t, D_MODEL), p["conv_w"], p["wc"], p["wa"], b, s,
                    tl["merge_tm"], tl["halo"])
    return x2d, merged


def _forward(x_a, x_b, p):
    xa, mga = _mixer(x_a, p)
    xb, mgb = _mixer(x_b, p)
    t_a, t_b = xa.shape[0], xb.shape[0]
    t = t_a + t_b
    tl = _moe_tiles(t_a, t_b)
    h, xn2, route, route_t, counts = _out_route(xa, mga, xb, mgb, p["wo"], p["ln2_g"], p["wr_hl"],
                                                p["br"], tl["route_tm"])
    n_blocks = 2 * t // ROW_BLOCK + N_EXPERTS
    counts_i32 = counts.reshape(LANES).astype(jnp.int32)
    pos_t, starts, blk_exp, blk_src, blk_meta, nvalid = _plan(counts_i32, route_t, n_blocks)
    pos0, pos1 = pos_t[0], pos_t[1]
    xs = _dispatch(pos0, pos1, counts_i32, starts, nvalid, xn2, n_blocks * ROW_BLOCK, tl["disp_tm"])
    ys = _experts(blk_exp, blk_src, blk_meta, nvalid, xs, p["wg"], p["wu"], p["wd"])
    y_a, y_b = _combine(pos0, pos1, ys, h, route, t_a, tl["comb_tm"])
    return y_a.reshape(x_a.shape), y_b.reshape(x_b.shape)


def kernel(x_prompt, x_sample, ln1_g, w_in, conv_w, q_a_norm_g, w_uq, kv_a_norm_g, w_ukv, q_norm_g,
           k_norm_g, w_conv_out, w_attn_out, w_out, ln2_g, w_router_group, b_router_group,
           w_router_exp, b_router_exp, w_gate, w_up, w_down):
    p = _prepare(ln1_g, w_in, conv_w, q_a_norm_g, w_uq, kv_a_norm_g, w_ukv, q_norm_g, k_norm_g,
                 w_conv_out, w_attn_out, w_out, ln2_g, w_router_group, b_router_group,
                 w_router_exp, b_router_exp, w_gate, w_up, w_down)
    return _forward(x_prompt, x_sample, p)
```

```python
import functools
import math

import jax
import jax.numpy as jnp
from jax import lax
from jax.experimental import pallas as pl
from jax.experimental.pallas import tpu as pltpu

F32 = jnp.float32
BF16 = jnp.bfloat16

D_MODEL = 2048
CONV_CH = 1024
N_HEADS = 16
QK_NOPE = 128
QK_ROPE = 64
HALF_ROPE = QK_ROPE // 2
QK_DIM = QK_NOPE + QK_ROPE
QK_PAD = 256
V_DIM = 128
Q_LORA = 512
KV_LORA = 512
ROPE_THETA = 10000.0
N_GROUPS = 8
EXP_PER_GROUP = 8
N_EXPERTS = 64
D_EXPERT = 512
EPS = 1e-6
PROJ_W = 8192
KPE_OFF = 4096
LANES = 128
EXP_LANE0 = N_GROUPS
ROW_BLOCK = 256
VMEM_LIMIT = 56 * 1024 * 1024

_NT = (((1,), (1,)), ((), ()))


def _cparams(sem):
    return pltpu.CompilerParams(dimension_semantics=sem, vmem_limit_bytes=VMEM_LIMIT)


D_HALF = D_MODEL // 2
U32 = jnp.uint32


def _pack_halves(x):
    lo = lax.bitcast_convert_type(x[:, :D_HALF].astype(BF16).astype(F32), U32)
    hi = lax.bitcast_convert_type(x[:, D_HALF:].astype(BF16).astype(F32), U32)
    return (lo >> 16) | hi


def _unpack_halves(w):
    lo = lax.bitcast_convert_type(w << 16, F32)
    hi = lax.bitcast_convert_type(w & jnp.uint32(0xFFFF0000), F32)
    return lo, hi


def _inproj_body(x_ref, g_ref, w_ref, wpe_ref, out_ref, kpe_ref, xn_ref):
    @pl.when(pl.program_id(1) == 0)
    def _():
        x = x_ref[...]
        ms = jnp.mean(x * x, axis=-1, keepdims=True)
        xn = (x * lax.rsqrt(ms + EPS) * g_ref[...]).astype(BF16)
        xn_ref[...] = xn
        kpe_ref[...] = jnp.dot(xn, wpe_ref[...], preferred_element_type=F32)

    out_ref[...] = jnp.dot(xn_ref[...], w_ref[...], preferred_element_type=F32).astype(BF16)


def _in_proj(x2d, ln1_g, w_main, w_pe, tm, tn):
    t = x2d.shape[0]
    return pl.pallas_call(
        _inproj_body,
        grid=(t // tm, PROJ_W // tn),
        in_specs=[
            pl.BlockSpec((tm, D_MODEL), lambda i, j: (i, 0)),
            pl.BlockSpec((1, D_MODEL), lambda i, j: (0, 0)),
            pl.BlockSpec((D_MODEL, tn), lambda i, j: (0, j)),
            pl.BlockSpec((D_MODEL, LANES), lambda i, j: (0, 0)),
        ],
        out_specs=[
            pl.BlockSpec((tm, tn), lambda i, j: (i, j)),
            pl.BlockSpec((tm, LANES), lambda i, j: (i, 0)),
        ],
        out_shape=[
            jax.ShapeDtypeStruct((t, PROJ_W), BF16),
            jax.ShapeDtypeStruct((t, LANES), F32),
        ],
        scratch_shapes=[pltpu.VMEM((tm, D_MODEL), BF16)],
        compiler_params=_cparams(("arbitrary", "arbitrary")),
        name="in_proj",
    )(x2d, ln1_g, w_main, w_pe)


def _qkv_body(ql_ref, kvl_ref, kpe_ref, cosk_ref, sink_ref, cost_ref, sint_ref,
              gqa_ref, gkva_ref, gq_ref, gkn_ref, gkr_ref, wq_ref, wkn_ref, wv_ref,
              qt_ref, k_ref, vt_ref):
    def latent_norm(ref, g_ref):
        v = ref[...].astype(F32)
        ms = jnp.mean(v * v, axis=-1, keepdims=True)
        return (v * lax.rsqrt(ms + EPS) * g_ref[...]).astype(BF16)

    qn = latent_norm(ql_ref, gqa_ref)
    kvn = latent_norm(kvl_ref, gkva_ref)
    tm = qn.shape[0]

    kn = jnp.dot(kvn, wkn_ref[...], preferred_element_type=F32)
    kpe = kpe_ref[...]
    ss_pe = jnp.sum(kpe * kpe, axis=-1, keepdims=True)
    kr = kpe * gkr_ref[...]
    kr = kr * cosk_ref[...] + pltpu.roll(kr, 2 * HALF_ROPE, axis=1) * sink_ref[...]
    gkn = gkn_ref[...]
    for h in range(N_HEADS):
        kh = kn[:, h * QK_NOPE:(h + 1) * QK_NOPE]
        ss = jnp.sum(kh * kh, axis=-1, keepdims=True) + ss_pe
        r = lax.rsqrt(ss * (1.0 / QK_DIM) + EPS)
        k_ref[0, h, :, 0:QK_NOPE] = (kh * gkn * r).astype(BF16)
        k_ref[0, h, :, QK_NOPE:QK_PAD] = (kr * r).astype(BF16)

    vt = lax.dot_general(wv_ref[...], kvn, _NT, preferred_element_type=F32)
    for h in range(N_HEADS):
        vt_ref[0, h] = vt[h * V_DIM:(h + 1) * V_DIM, :].astype(BF16)

    cost = cost_ref[...]
    sint = sint_ref[...]
    gq = gq_ref[...]
    zeros = jnp.zeros((HALF_ROPE, tm), BF16)
    for h in range(N_HEADS):
        qt = lax.dot_general(wq_ref[h * QK_PAD:(h + 1) * QK_PAD, :], qn, _NT,
                             preferred_element_type=F32)
        ss = jnp.sum(qt * qt, axis=0, keepdims=True)
        r = lax.rsqrt(ss * (1.0 / QK_DIM) + EPS)
        qs = qt * gq * r
        x1 = qs[QK_NOPE:QK_NOPE + HALF_ROPE]
        x2 = qs[QK_NOPE + 2 * HALF_ROPE:QK_NOPE + 3 * HALF_ROPE]
        qt_ref[0, h, 0:QK_NOPE, :] = qs[0:QK_NOPE].astype(BF16)
        qt_ref[0, h, QK_NOPE:QK_NOPE + HALF_ROPE, :] = (x1 * cost - x2 * sint).astype(BF16)
        qt_ref[0, h, QK_NOPE + HALF_ROPE:QK_NOPE + 2 * HALF_ROPE, :] = zeros
        qt_ref[0, h, QK_NOPE + 2 * HALF_ROPE:QK_NOPE + 3 * HALF_ROPE, :] = (
            x1 * sint + x2 * cost).astype(BF16)
        qt_ref[0, h, QK_NOPE + 3 * HALF_ROPE:QK_PAD, :] = zeros


def _qkv(proj, kpe, tabs, wts, b, s, tm):
    ns = s // tm
    cos_k, sin_k, cos_t, sin_t = tabs
    gqa, gkva, gq, gkn, gkr, wq_t, wkn, wv_t = wts
    const = lambda shape: pl.BlockSpec(shape, lambda bi, i: (0,) * len(shape))
    return pl.pallas_call(
        _qkv_body,
        grid=(b, ns),
        in_specs=[
            pl.BlockSpec((tm, Q_LORA), lambda bi, i: (bi * ns + i, 3 * CONV_CH // Q_LORA)),
            pl.BlockSpec((tm, KV_LORA), lambda bi, i: (bi * ns + i, 3 * CONV_CH // KV_LORA + 1)),
            pl.BlockSpec((tm, LANES), lambda bi, i: (bi * ns + i, 0)),
            pl.BlockSpec((tm, LANES), lambda bi, i: (i, 0)),
            pl.BlockSpec((tm, LANES), lambda bi, i: (i, 0)),
            pl.BlockSpec((HALF_ROPE, tm), lambda bi, i: (0, i)),
            pl.BlockSpec((HALF_ROPE, tm), lambda bi, i: (0, i)),
            const((1, Q_LORA)), const((1, KV_LORA)), const((QK_PAD, 1)),
            const((1, LANES)), const((1, LANES)),
            const((N_HEADS * QK_PAD, Q_LORA)),
            const((KV_LORA, N_HEADS * QK_NOPE)),
            const((N_HEADS * V_DIM, KV_LORA)),
        ],
        out_specs=[
            pl.BlockSpec((1, N_HEADS, QK_PAD, tm), lambda bi, i: (bi, 0, 0, i)),
            pl.BlockSpec((1, N_HEADS, tm, QK_PAD), lambda bi, i: (bi, 0, i, 0)),
            pl.BlockSpec((1, N_HEADS, V_DIM, tm), lambda bi, i: (bi, 0, 0, i)),
        ],
        out_shape=[
            jax.ShapeDtypeStruct((b, N_HEADS, QK_PAD, s), BF16),
            jax.ShapeDtypeStruct((b, N_HEADS, s, QK_PAD), BF16),
            jax.ShapeDtypeStruct((b, N_HEADS, V_DIM, s), BF16),
        ],
        compiler_params=_cparams(("arbitrary", "arbitrary")),
        name="qkv",
    )(proj, proj, kpe, cos_k, sin_k, cos_t, sin_t, gqa, gkva, gq, gkn, gkr, wq_t, wkn, wv_t)


def _attn_body(qt_ref, k_ref, vt_ref, o_ref, s_a, m_a, s_b, m_b, *, kc):
    n = pl.program_id(0)
    s_len = k_ref.shape[2]
    chunks = [(c * kc, (c + 1) * kc) for c in range(s_len // kc)]

    @pl.when(n == 0)
    def _():
        s_b[...] = jnp.zeros_like(s_b)
        m_b[...] = jnp.zeros_like(m_b)

    def step(s_w, m_w, s_r, m_r):
        m_prev = m_r[...]
        qt = qt_ref[0, 0]
        l = None
        acc = None
        m = None
        for lo, hi in chunks:
            p = jnp.exp2(s_r[lo:hi, :] - m_prev)
            lc = jnp.sum(p, axis=0, keepdims=True)
            l = lc if l is None else l + lc
            pv = jnp.dot(vt_ref[0, 0, :, lo:hi], p.astype(BF16), preferred_element_type=F32)
            acc = pv if acc is None else acc + pv

            sc = jnp.dot(k_ref[0, 0, lo:hi, :], qt, preferred_element_type=F32)
            s_w[lo:hi, :] = sc
            mc = jnp.max(sc, axis=0, keepdims=True)
            m = mc if m is None else jnp.maximum(m, mc)
        o_ref[0] = (acc * (1.0 / l)).T.astype(BF16)
        m_w[...] = m

    @pl.when(n % 2 == 0)
    def _():
        step(s_a, m_a, s_b, m_b)

    @pl.when(n % 2 == 1)
    def _():
        step(s_b, m_b, s_a, m_a)


def _attention(qt, k, vt, tq, kc):
    b, _, _, s = qt.shape
    nq = s // tq
    n_tiles = b * N_HEADS * nq

    def bhi(tile):
        return tile // (N_HEADS * nq), (tile // nq) % N_HEADS, tile % nq

    def score_tile(n):
        return bhi(jnp.minimum(n, n_tiles - 1))

    def value_tile(n):
        return bhi(jnp.maximum(n - 1, 0))

    def qt_map(n):
        bi, h, i = score_tile(n)
        return bi, h, 0, i

    def k_map(n):
        bi, h, _ = score_tile(n)
        return bi, h, 0, 0

    def vt_map(n):
        bi, h, _ = value_tile(n)
        return bi, h, 0, 0

    def o_map(n):
        bi, h, i = value_tile(n)
        return bi, i, h

    return pl.pallas_call(
        functools.partial(_attn_body, kc=kc),
        grid=(n_tiles + 1,),
        in_specs=[
            pl.BlockSpec((1, 1, QK_PAD, tq), qt_map),
            pl.BlockSpec((1, 1, s, QK_PAD), k_map),
            pl.BlockSpec((1, 1, V_DIM, s), vt_map),
        ],
        out_specs=pl.BlockSpec((1, tq, V_DIM), o_map),
        out_shape=jax.ShapeDtypeStruct((b, s, N_HEADS * V_DIM), BF16),
        scratch_shapes=[pltpu.VMEM((s, tq), F32), pltpu.VMEM((1, tq), F32),
                        pltpu.VMEM((s, tq), F32), pltpu.VMEM((1, tq), F32)],
        compiler_params=_cparams(("arbitrary",)),
        name="attention",
    )(qt, k, vt)


def _merge_body(u_ref, gb_ref, gc_ref, up_ref, gcp_ref, un_ref, gcn_ref, o_ref, gtc_ref, gta_ref,
                cw_ref, wc_ref, wa_ref, out_ref):
    i = pl.program_id(1)
    last = pl.num_programs(1) - 1
    halo = up_ref.shape[0]
    v = gc_ref[...].astype(F32) * u_ref[...].astype(F32)
    tm = v.shape[0]
    v_before = gcp_ref[halo - 1:halo, :].astype(F32) * up_ref[halo - 1:halo, :].astype(F32)
    v_after = gcn_ref[0:1, :].astype(F32) * un_ref[0:1, :].astype(F32)
    v_before = jnp.where(i == 0, 0.0, v_before)
    v_after = jnp.where(i == last, 0.0, v_after)
    row = lax.broadcasted_iota(jnp.int32, (tm, 1), 0)
    v_prev = jnp.where(row == 0, v_before, pltpu.roll(v, 1, axis=0))
    v_next = jnp.where(row == tm - 1, v_after, pltpu.roll(v, tm - 1, axis=0))
    cw = cw_ref[...]
    conv = v_prev * cw[0:1, :] + v * cw[1:2, :] + v_next * cw[2:3, :]
    z = (gb_ref[...].astype(F32) * conv).astype(BF16)
    conv_out = jnp.dot(z, wc_ref[...], preferred_element_type=F32)
    attn_out = jnp.dot(o_ref[...], wa_ref[...], preferred_element_type=F32)
    merged = (jax.nn.sigmoid(gtc_ref[...].astype(F32)) * conv_out
              + jax.nn.sigmoid(gta_ref[...].astype(F32)) * attn_out)
    out_ref[...] = merged.astype(BF16)


def _merge(proj, o2d, conv_w, wc, wa, b, s, tm, halo):
    ns = s // tm
    t = b * s
    hb = tm // halo
    nh = t // halo
    row = lambda bi, i: bi * ns + i
    prev = lambda bi, i: (jnp.maximum(row(bi, i) * hb - 1, 0))
    nxt = lambda bi, i: (jnp.minimum((row(bi, i) + 1) * hb, nh - 1))
    gate0 = KPE_OFF // D_MODEL
    return pl.pallas_call(
        _merge_body,
        grid=(b, ns),
        in_specs=[
            pl.BlockSpec((tm, CONV_CH), lambda bi, i: (row(bi, i), 0)),
            pl.BlockSpec((tm, CONV_CH), lambda bi, i: (row(bi, i), 1)),
            pl.BlockSpec((tm, CONV_CH), lambda bi, i: (row(bi, i), 2)),
            pl.BlockSpec((halo, CONV_CH), lambda bi, i: (prev(bi, i), 0)),
            pl.BlockSpec((halo, CONV_CH), lambda bi, i: (prev(bi, i), 2)),
            pl.BlockSpec((halo, CONV_CH), lambda bi, i: (nxt(bi, i), 0)),
            pl.BlockSpec((halo, CONV_CH), lambda bi, i: (nxt(bi, i), 2)),
            pl.BlockSpec((tm, D_MODEL), lambda bi, i: (row(bi, i), 0)),
            pl.BlockSpec((tm, D_MODEL), lambda bi, i: (row(bi, i), gate0)),
            pl.BlockSpec((tm, D_MODEL), lambda bi, i: (row(bi, i), gate0 + 1)),
            pl.BlockSpec((3, CONV_CH), lambda bi, i: (0, 0)),
            pl.BlockSpec((CONV_CH, D_MODEL), lambda bi, i: (0, 0)),
            pl.BlockSpec((D_MODEL, D_MODEL), lambda bi, i: (0, 0)),
        ],
        out_specs=pl.BlockSpec((tm, D_MODEL), lambda bi, i: (row(bi, i), 0)),
        out_shape=jax.ShapeDtypeStruct((t, D_MODEL), BF16),
        compiler_params=_cparams(("arbitrary", "arbitrary")),
        name="merge",
    )(proj, proj, proj, proj, proj, proj, proj, o2d, proj, proj, conv_w, wc, wa)


def _out_route_body(xa_ref, mga_ref, xb_ref, mgb_ref, wo_ref, g2_ref, wr_ref, br_ref,
                    h_ref, xn_ref, route_ref, routet_ref, cnt_ref, base_ref, logit_ref, *, steps_a):
    step = pl.program_id(0)
    tm = h_ref.shape[0]
    slot = step % 2

    @pl.when(step == 0)
    def _():
        base_ref[...] = jnp.zeros_like(base_ref)
        logit_ref[...] = jnp.zeros_like(logit_ref)

    in_a = step < steps_a
    mg = jnp.where(in_a, mga_ref[...], mgb_ref[...])
    n_chunks = 4
    cw = D_MODEL // n_chunks

    def h_chunk(c):
        x = jnp.where(in_a, xa_ref[:, c * cw:(c + 1) * cw], xb_ref[:, c * cw:(c + 1) * cw])
        return x + jnp.dot(mg, wo_ref[:, c * cw:(c + 1) * cw], preferred_element_type=F32)

    logits = logit_ref[1 - slot]
    lane = lax.broadcasted_iota(jnp.int32, (tm, LANES), 1)
    neg = -jnp.inf
    lg = jnp.where(lane < N_GROUPS, logits, neg)
    gmax = jnp.max(lg, axis=-1, keepdims=True)
    g_p = 1.0 / jnp.sum(jnp.exp(lg - gmax), axis=-1, keepdims=True)
    g_sel = jnp.min(jnp.where(lg == gmax, lane, LANES), axis=-1, keepdims=True)
    h_parts = [h_chunk(0)]

    lo_lane = EXP_LANE0 + g_sel * EXP_PER_GROUP
    le = jnp.where((lane >= lo_lane) & (lane < lo_lane + EXP_PER_GROUP), logits, neg)
    m1 = jnp.max(le, axis=-1, keepdims=True)
    i1 = jnp.min(jnp.where(le == m1, lane, LANES), axis=-1, keepdims=True)
    le2 = jnp.where(lane == i1, neg, le)
    m2 = jnp.max(le2, axis=-1, keepdims=True)
    i2 = jnp.min(jnp.where(le2 == m2, lane, LANES), axis=-1, keepdims=True)
    e2 = jnp.exp(m2 - m1)
    gate1 = g_p / (1.0 + e2)
    gate2 = g_p * e2 / (1.0 + e2)
    h_parts.append(h_chunk(1))

    sel1 = lane == i1
    sel2 = lane == i2
    onehot = jnp.where(sel1 | sel2, 1.0, 0.0)
    r_i = lax.broadcasted_iota(jnp.int32, (tm, tm), 0)
    c_i = lax.broadcasted_iota(jnp.int32, (tm, tm), 1)
    lower = jnp.where(r_i > c_i, 1.0, 0.0).astype(BF16)
    before = jnp.dot(lower, onehot.astype(BF16), preferred_element_type=F32) + base_ref[...]
    rank1 = jnp.sum(jnp.where(sel1, before, 0.0), axis=-1, keepdims=True)
    rank2 = jnp.sum(jnp.where(sel2, before, 0.0), axis=-1, keepdims=True)
    counted = jnp.where(step > 0, 1.0, 0.0)
    base_ref[...] = base_ref[...] + counted * jnp.sum(onehot, axis=0, keepdims=True)
    cnt_ref[...] = base_ref[...]
    h_parts.append(h_chunk(2))

    e1f = (i1 - EXP_LANE0).astype(F32)
    e2f = (i2 - EXP_LANE0).astype(F32)
    cols = (e1f, e2f, gate1, gate2, rank1, rank2)
    route = jnp.zeros((tm, LANES), F32)
    for c, val in enumerate(cols):
        route = jnp.where(lane == c, val, route)
    route_ref[...] = route
    routet_ref[...] = route.T[0:8, :]
    h_parts.append(h_chunk(3))

    h = jnp.concatenate(h_parts, axis=1)
    h_ref[...] = h
    ms = jnp.mean(h * h, axis=-1, keepdims=True)
    xn = h * lax.rsqrt(ms + EPS) * g2_ref[...]
    xn_ref[...] = _pack_halves(xn)

    hi = xn.astype(BF16)
    lo = (xn - hi.astype(F32)).astype(BF16)
    w_hl = wr_ref[...]
    both = jnp.dot(hi, w_hl, preferred_element_type=F32)
    logit_ref[slot] = (both[:, :LANES] + both[:, LANES:]
                       + jnp.dot(lo, w_hl[:, :LANES], preferred_element_type=F32)) + br_ref[...]


def _out_route(xa, mga, xb, mgb, wo, ln2_g, wr_hl, br, tm):
    steps_a, steps_b = xa.shape[0] // tm, xb.shape[0] // tm
    t = xa.shape[0] + xb.shape[0]
    const = lambda shape: pl.BlockSpec(shape, lambda i: (0,) * len(shape))
    n_tiles = steps_a + steps_b
    seg_a = pl.BlockSpec((tm, D_MODEL), lambda i: (jnp.minimum(i, steps_a - 1), 0))
    seg_b = pl.BlockSpec((tm, D_MODEL), lambda i: (jnp.clip(i - steps_a, 0, steps_b - 1), 0))
    this_tile = lambda i: jnp.minimum(i, n_tiles - 1)
    prev_tile = lambda i: jnp.maximum(i - 1, 0)
    return pl.pallas_call(
        functools.partial(_out_route_body, steps_a=steps_a),
        grid=(n_tiles + 1,),
        in_specs=[
            seg_a, seg_a, seg_b, seg_b,
            const((D_MODEL, D_MODEL)), const((1, D_MODEL)),
            const((D_MODEL, 2 * LANES)), const((1, LANES)),
        ],
        out_specs=[
            pl.BlockSpec((tm, D_MODEL), lambda i: (this_tile(i), 0)),
            pl.BlockSpec((tm, D_HALF), lambda i: (this_tile(i), 0)),
            pl.BlockSpec((tm, LANES), lambda i: (prev_tile(i), 0)),
            pl.BlockSpec((8, tm), lambda i: (0, prev_tile(i))),
            const((1, LANES)),
        ],
        out_shape=[
            jax.ShapeDtypeStruct((t, D_MODEL), F32),
            jax.ShapeDtypeStruct((t, D_HALF), U32),
            jax.ShapeDtypeStruct((t, LANES), F32),
            jax.ShapeDtypeStruct((8, t), F32),
            jax.ShapeDtypeStruct((1, LANES), F32),
        ],
        scratch_shapes=[pltpu.VMEM((1, LANES), F32), pltpu.VMEM((2, tm, LANES), F32)],
        compiler_params=_cparams(("arbitrary",)),
        name="out_route",
    )(xa, mga, xb, mgb, wo, ln2_g, wr_hl, br)


def _block_meta(next_expert, slot, first):
    return (next_expert + 1) * 4 + slot * 2 + first


def _plan_body(cnt_ref, routet_ref, pos_ref, start_ref, blk_exp_ref, blk_src_ref, blk_meta_ref,
               nvalid_ref, next_ref):
    n_blocks = blk_exp_ref.shape[0]

    def backwards(k, following):
        e = N_EXPERTS - 1 - k
        next_ref[e] = following
        return jnp.where(cnt_ref[EXP_LANE0 + e] > 0, e, following)

    lax.fori_loop(0, N_EXPERTS, backwards, -1)

    def per_expert(e, carry):
        acc, ordinal = carry
        c = cnt_ref[EXP_LANE0 + e]
        nb = (c + ROW_BLOCK - 1) // ROW_BLOCK
        start_ref[e] = acc * ROW_BLOCK

        def fill(j, inner):
            blk_exp_ref[acc + j] = e
            blk_src_ref[acc + j] = acc + j
            blk_meta_ref[acc + j] = _block_meta(next_ref[e], ordinal % 2, jnp.where(j == 0, 1, 0))
            return inner

        lax.fori_loop(0, nb, fill, 0)
        return acc + nb, ordinal + jnp.where(nb > 0, 1, 0)

    nvalid, _ = lax.fori_loop(0, N_EXPERTS, per_expert, (0, 0))
    nvalid_ref[0] = nvalid
    last_exp = blk_exp_ref[nvalid - 1]

    def tail(j, carry):
        blk_exp_ref[j] = last_exp
        blk_src_ref[j] = nvalid - 1
        blk_meta_ref[j] = 0
        return carry

    lax.fori_loop(nvalid, n_blocks, tail, 0)

    ef = routet_ref[0:2, :]
    off = jnp.zeros_like(ef)
    for e in range(N_EXPERTS):
        off = jnp.where(ef == float(e), start_ref[e].astype(F32), off)
    pos_ref[...] = (off + routet_ref[4:6, :]).astype(jnp.int32)


def _plan(counts_i32, routet, n_blocks):
    t = routet.shape[1]
    smem = lambda: pl.BlockSpec(memory_space=pltpu.SMEM)
    return pl.pallas_call(
        _plan_body,
        in_specs=[smem(), pl.BlockSpec(memory_space=pltpu.VMEM)],
        out_specs=[pl.BlockSpec(memory_space=pltpu.VMEM), smem(), smem(), smem(), smem(), smem()],
        out_shape=[
            jax.ShapeDtypeStruct((2, t), jnp.int32),
            jax.ShapeDtypeStruct((N_EXPERTS,), jnp.int32),
            jax.ShapeDtypeStruct((n_blocks,), jnp.int32),
            jax.ShapeDtypeStruct((n_blocks,), jnp.int32),
            jax.ShapeDtypeStruct((n_blocks,), jnp.int32),
            jax.ShapeDtypeStruct((1,), jnp.int32),
        ],
        scratch_shapes=[pltpu.SMEM((N_EXPERTS,), jnp.int32)],
        compiler_params=pltpu.CompilerParams(vmem_limit_bytes=VMEM_LIMIT),
        name="plan",
    )(counts_i32, routet)


def _dispatch_body(pos0_ref, pos1_ref, cnt_ref, start_ref, nvalid_ref, xn_ref, xs_hbm, zero_ref, sem):
    tm = xn_ref.shape[0]
    n_blocks = xs_hbm.shape[0] // ROW_BLOCK

    def put(src_ref, src_row, dst_row):
        return pltpu.make_async_copy(src_ref.at[pl.ds(src_row, 1)], xs_hbm.at[pl.ds(dst_row, 1)], sem)

    def put_block(blk):
        return pltpu.make_async_copy(zero_ref, xs_hbm.at[pl.ds(blk * ROW_BLOCK, ROW_BLOCK)], sem)

    @pl.when(pl.program_id(0) == 0)
    def _():
        zero_ref[...] = jnp.zeros_like(zero_ref)

        def tail(blk, carry):
            put_block(blk).start()
            return carry

        def tail_done(blk, carry):
            put_block(0).wait()
            return carry

        lax.fori_loop(nvalid_ref[0], n_blocks, tail, 0)

        def pad_rows(wait):
            def per_expert(e, carry):
                c = cnt_ref[EXP_LANE0 + e]
                end = (c + ROW_BLOCK - 1) // ROW_BLOCK * ROW_BLOCK
                base = start_ref[e]

                def fill(r, inner):
                    if wait:
                        put(zero_ref, 0, 0).wait()
                    else:
                        put(zero_ref, 0, base + r).start()
                    return inner

                lax.fori_loop(c, end, fill, 0)
                return carry

            lax.fori_loop(0, N_EXPERTS, per_expert, 0)

        pad_rows(wait=False)
        lax.fori_loop(nvalid_ref[0], n_blocks, tail_done, 0)
        pad_rows(wait=True)

    for t in range(tm):
        put(xn_ref, t, pos0_ref[t]).start(priority=0)
        put(xn_ref, t, pos1_ref[t]).start(priority=1)

    for t in range(2 * tm):
        put(xn_ref, 0, 0).wait()


def _dispatch(pos0, pos1, counts_i32, starts, nvalid, xn2, n_rows, tm):
    t = xn2.shape[0]
    smem = lambda: pl.BlockSpec(memory_space=pltpu.SMEM)
    pos_spec = lambda: pl.BlockSpec((tm,), lambda i: (i,), memory_space=pltpu.SMEM)
    return pl.pallas_call(
        _dispatch_body,
        grid=(t // tm,),
        in_specs=[
            pos_spec(), pos_spec(),
            smem(), smem(), smem(),
            pl.BlockSpec((tm, D_HALF), lambda i: (i, 0)),
        ],
        out_specs=pl.BlockSpec(memory_space=pl.ANY),
        out_shape=jax.ShapeDtypeStruct((n_rows, D_HALF), U32),
        scratch_shapes=[pltpu.VMEM((ROW_BLOCK, D_HALF), U32), pltpu.SemaphoreType.DMA(())],
        compiler_params=pltpu.CompilerParams(dimension_semantics=("arbitrary",),
                                             vmem_limit_bytes=VMEM_LIMIT, has_side_effects=True),
        name="dispatch",
    )(pos0, pos1, counts_i32, starts, nvalid, xn2)


def _expert_body(blk_exp_ref, blk_src_ref, blk_meta_ref, nvalid_ref, xs_ref, wg_hbm, wu_hbm, wd_hbm,
                 ys_ref, wg_buf, wu_buf, wd_buf, sems):
    i = pl.program_id(0)
    valid = i < nvalid_ref[0]
    meta = blk_meta_ref[i]
    first = meta % 2
    slot = (meta // 2) % 2
    next_expert = meta // 4 - 1

    def weight_copies(expert, to_slot):
        return [pltpu.make_async_copy(hbm.at[expert], buf.at[to_slot], sems.at[to_slot])
                for hbm, buf in ((wg_hbm, wg_buf), (wu_hbm, wu_buf), (wd_hbm, wd_buf))]

    @pl.when(i == 0)
    def _():
        for cp in weight_copies(blk_exp_ref[0], 0):
            cp.start()

    @pl.when(first == 1)
    def _():
        for cp in weight_copies(blk_exp_ref[i], slot):
            cp.wait()

        @pl.when(next_expert >= 0)
        def _():
            for cp in weight_copies(next_expert, 1 - slot):
                cp.start(priority=1)

    @pl.when(valid)
    def _():
        lo, hi = _unpack_halves(xs_ref[...])
        xb = jnp.concatenate([lo.astype(BF16), hi.astype(BF16)], axis=1)
        g = jnp.dot(xb, wg_buf[slot].astype(BF16), preferred_element_type=F32)
        u = jnp.dot(xb, wu_buf[slot].astype(BF16), preferred_element_type=F32)
        hmid = (g * jax.nn.sigmoid(g) * u).astype(BF16)
        ys_ref[...] = _pack_halves(jnp.dot(hmid, wd_buf[slot].astype(BF16), preferred_element_type=F32))

    @pl.when(jnp.logical_not(valid))
    def _():
        ys_ref[...] = jnp.zeros_like(ys_ref)


def _experts(blk_exp, blk_src, blk_meta, nvalid, xs, wg, wu, wd):
    n_rows = xs.shape[0]
    n_blocks = n_rows // ROW_BLOCK
    hbm = lambda: pl.BlockSpec(memory_space=pl.ANY)
    grid_spec = pltpu.PrefetchScalarGridSpec(
        num_scalar_prefetch=4,
        grid=(n_blocks,),
        in_specs=[
            pl.BlockSpec((ROW_BLOCK, D_HALF), lambda i, be, bs, bm, nv: (bs[i], 0)),
            hbm(), hbm(), hbm(),
        ],
        out_specs=pl.BlockSpec((ROW_BLOCK, D_HALF), lambda i, be, bs, bm, nv: (i, 0)),
        scratch_shapes=[
            pltpu.VMEM((2, D_MODEL, D_EXPERT), F32),
            pltpu.VMEM((2, D_MODEL, D_EXPERT), F32),
            pltpu.VMEM((2, D_EXPERT, D_MODEL), F32),
            pltpu.SemaphoreType.DMA((2,)),
        ],
    )
    return pl.pallas_call(
        _expert_body,
        grid_spec=grid_spec,
        out_shape=jax.ShapeDtypeStruct((n_rows, D_HALF), U32),
        compiler_params=_cparams(("arbitrary",)),
        name="experts",
    )(blk_exp, blk_src, blk_meta, nvalid, xs, wg, wu, wd)


def _combine_body(pos0_ref, pos1_ref, pos0_next_ref, pos1_next_ref, ys_hbm, h_ref, route_ref,
                  ya_ref, yb_ref, buf_ref, sems, *, steps_a):
    n = pl.program_id(0)
    tm = h_ref.shape[0]
    slot = n % 2

    def row_copy(row, k, t, to_slot):
        return pltpu.make_async_copy(ys_hbm.at[pl.ds(row, 1)],
                                     buf_ref.at[to_slot, k, pl.ds(t, 1)], sems.at[to_slot])

    def gather(p0_ref, p1_ref, to_slot):
        for t in range(tm):
            row_copy(p0_ref[t], 0, t, to_slot).start(priority=0)
            row_copy(p1_ref[t], 1, t, to_slot).start(priority=1)

    @pl.when(n == 0)
    def _():
        gather(pos0_ref, pos1_ref, 0)

    @pl.when(n + 1 < pl.num_programs(0))
    def _():
        gather(pos0_next_ref, pos1_next_ref, 1 - slot)

    for t in range(tm):
        row_copy(0, 0, t, slot).wait()
        row_copy(0, 1, t, slot).wait()
    route = route_ref[...]
    lo1, hi1 = _unpack_halves(buf_ref[slot, 0])
    lo2, hi2 = _unpack_halves(buf_ref[slot, 1])
    g1, g2 = route[:, 2:3], route[:, 3:4]
    y_lo = h_ref[:, :D_HALF] + g1 * lo1 + g2 * lo2
    y_hi = h_ref[:, D_HALF:] + g1 * hi1 + g2 * hi2

    @pl.when(n < steps_a)
    def _():
        ya_ref[:, :D_HALF] = y_lo
        ya_ref[:, D_HALF:] = y_hi

    @pl.when(n >= steps_a)
    def _():
        yb_ref[:, :D_HALF] = y_lo
        yb_ref[:, D_HALF:] = y_hi


def _combine(pos0, pos1, ys, h, route, t_a, tm):
    t = h.shape[0]
    nt = t // tm
    steps_a = t_a // tm
    cur = lambda: pl.BlockSpec((tm,), lambda i: (i,), memory_space=pltpu.SMEM)
    nxt = lambda: pl.BlockSpec((tm,), lambda i: (jnp.minimum(i + 1, nt - 1),), memory_space=pltpu.SMEM)
    return pl.pallas_call(
        functools.partial(_combine_body, steps_a=steps_a),
        grid=(nt,),
        in_specs=[
            cur(), cur(), nxt(), nxt(),
            pl.BlockSpec(memory_space=pl.ANY),
            pl.BlockSpec((tm, D_MODEL), lambda i: (i, 0)),
            pl.BlockSpec((tm, LANES), lambda i: (i, 0)),
        ],
        out_specs=[
            pl.BlockSpec((tm, D_MODEL), lambda i: (jnp.minimum(i, steps_a - 1), 0)),
            pl.BlockSpec((tm, D_MODEL), lambda i: (jnp.maximum(i - steps_a, 0), 0)),
        ],
        out_shape=[
            jax.ShapeDtypeStruct((t_a, D_MODEL), F32),
            jax.ShapeDtypeStruct((t - t_a, D_MODEL), F32),
        ],
        scratch_shapes=[pltpu.VMEM((2, 2, tm, D_HALF), U32), pltpu.SemaphoreType.DMA((2,))],
        compiler_params=_cparams(("arbitrary",)),
        name="combine",
    )(pos0, pos1, pos0, pos1, ys, h, route)


def _pad_rope(a, axis):
    x1, x2 = jnp.split(a, 2, axis=axis)
    z = jnp.zeros_like(x1)
    return jnp.concatenate([x1, z, x2, z], axis=axis)


def _prepare(ln1_g, w_in, conv_w, q_a_norm_g, w_uq, kv_a_norm_g, w_ukv, q_norm_g, k_norm_g,
             w_conv_out, w_attn_out, w_out, ln2_g, w_router_group, b_router_group,
             w_router_exp, b_router_exp, w_gate, w_up, w_down):
    w_in0 = w_in[0]
    w_main = jnp.concatenate([w_in0[:, :KPE_OFF], w_in0[:, KPE_OFF + QK_ROPE:]], axis=1).astype(BF16)
    w_pe = _pad_rope(w_in0[:, KPE_OFF:KPE_OFF + QK_ROPE], 1).astype(BF16)

    wq = w_uq[0].reshape(Q_LORA, N_HEADS, QK_DIM)
    wq = jnp.concatenate([wq[:, :, :QK_NOPE], _pad_rope(wq[:, :, QK_NOPE:], 2)], axis=2)
    wq_t = wq.reshape(Q_LORA, N_HEADS * QK_PAD).T.astype(BF16)
    wkv = w_ukv[0].reshape(KV_LORA, N_HEADS, QK_NOPE + V_DIM)
    wkn = wkv[:, :, :QK_NOPE].reshape(KV_LORA, N_HEADS * QK_NOPE).astype(BF16)
    wv_t = wkv[:, :, QK_NOPE:].reshape(KV_LORA, N_HEADS * V_DIM).T.astype(BF16)

    qg = q_norm_g[0]
    score_scale = QK_DIM ** -0.5 * math.log2(math.e)
    gq = (jnp.concatenate([qg[:QK_NOPE], _pad_rope(qg[QK_NOPE:], 0)]) * score_scale).reshape(QK_PAD, 1)
    kg = k_norm_g[0]
    gkn = kg[:QK_NOPE].reshape(1, LANES)
    gkr = _pad_rope(kg[QK_NOPE:], 0).reshape(1, LANES)

    wr = jnp.concatenate([w_router_group[0], w_router_exp[0],
                          jnp.zeros((D_MODEL, LANES - N_GROUPS - N_EXPERTS), F32)], axis=1)
    wr_hi = wr.astype(BF16)
    wr_hl = jnp.concatenate([wr_hi, (wr - wr_hi.astype(F32)).astype(BF16)], axis=1)
    br = jnp.concatenate([b_router_group[0], b_router_exp[0],
                          jnp.zeros((LANES - N_GROUPS - N_EXPERTS,), F32)]).reshape(1, LANES)
    return dict(
        ln1_g=ln1_g, w_main=w_main, w_pe=w_pe, conv_w=conv_w[0],
        qkv=(q_a_norm_g, kv_a_norm_g, gq, gkn, gkr, wq_t, wkn, wv_t),
        wc=w_conv_out[0].astype(BF16), wa=w_attn_out[0].astype(BF16), wo=w_out[0].astype(BF16),
        ln2_g=ln2_g, wr_hl=wr_hl, br=br,
        wg=w_gate[0], wu=w_up[0], wd=w_down[0],
    )


def _rope_tables(s):
    inv = ROPE_THETA ** (-jnp.arange(0, QK_ROPE, 2, dtype=F32) / QK_ROPE)
    ang = jnp.arange(s, dtype=F32)[:, None] * inv[None, :]
    cos, sin = jnp.cos(ang), jnp.sin(ang)
    z = jnp.zeros_like(cos)
    cos_k = jnp.concatenate([cos, z, cos, z], axis=1)
    sin_k = jnp.concatenate([-sin, z, sin, z], axis=1)
    return cos_k, sin_k, cos.T, sin.T


def _tiles(b, s):
    t = b * s
    pick = lambda n, pref: pref if n % pref == 0 else n
    return dict(
        in_tm=pick(t, 512), in_tn=2048,
        qkv_tm=pick(s, 256),
        attn_tq=pick(s, 1024), attn_kc=pick(s, 256),
        merge_tm=pick(s, 256), halo=16,
    )


def _moe_tiles(t_a, t_b):
    both = math.gcd(t_a, t_b)
    pick = lambda pref: pref if both % pref == 0 else both
    return dict(route_tm=pick(512), disp_tm=pick(1024), comb_tm=pick(256))


def _mixer(x, p):
    b, s, _ = x.shape
    t = b * s
    tl = _tiles(b, s)
    x2d = x.reshape(t, D_MODEL)
    proj, kpe = _in_proj(x2d, p["ln1_g"], p["w_main"], p["w_pe"], tl["in_tm"], tl["in_tn"])
    qt, k, vt = _qkv(proj, kpe, _rope_tables(s), p["qkv"], b, s, tl["qkv_tm"])
    o = _attention(qt, k, vt, tl["attn_tq"], tl["attn_kc"])
    merged = _merge(proj, o.reshape(t, D_MODEL), p["conv_w"], p["wc"], p["wa"], b, s,
                    tl["merge_tm"], tl["halo"])
    return x2d, merged


def _forward(x_a, x_b, p):
    xa, mga = _mixer(x_a, p)
    xb, mgb = _mixer(x_b, p)
    t_a, t_b = xa.shape[0], xb.shape[0]
    t = t_a + t_b
    tl = _moe_tiles(t_a, t_b)
    h, xn2, route, route_t, counts = _out_route(xa, mga, xb, mgb, p["wo"], p["ln2_g"], p["wr_hl"],
                                                p["br"], tl["route_tm"])
    n_blocks = 2 * t // ROW_BLOCK + N_EXPERTS
    counts_i32 = counts.reshape(LANES).astype(jnp.int32)
    pos_t, starts, blk_exp, blk_src, blk_meta, nvalid = _plan(counts_i32, route_t, n_blocks)
    pos0, pos1 = pos_t[0], pos_t[1]
    xs = _dispatch(pos0, pos1, counts_i32, starts, nvalid, xn2, n_blocks * ROW_BLOCK, tl["disp_tm"])
    ys = _experts(blk_exp, blk_src, blk_meta, nvalid, xs, p["wg"], p["wu"], p["wd"])
    y_a, y_b = _combine(pos0, pos1, ys, h, route, t_a, tl["comb_tm"])
    return y_a.reshape(x_a.shape), y_b.reshape(x_b.shape)


def kernel(x_prompt, x_sample, ln1_g, w_in, conv_w, q_a_norm_g, w_uq, kv_a_norm_g, w_ukv, q_norm_g,
           k_norm_g, w_conv_out, w_attn_out, w_out, ln2_g, w_router_group, b_router_group,
           w_router_exp, b_router_exp, w_gate, w_up, w_down):
    p = _prepare(ln1_g, w_in, conv_w, q_a_norm_g, w_uq, kv_a_norm_g, w_ukv, q_norm_g, k_norm_g,
                 w_conv_out, w_attn_out, w_out, ln2_g, w_router_group, b_router_group,
                 w_router_exp, b_router_exp, w_gate, w_up, w_down)
    return _forward(x_prompt, x_sample, p)
```

```python
import functools
import math

import jax
import jax.numpy as jnp
from jax import lax
from jax.experimental import pallas as pl
from jax.experimental.pallas import tpu as pltpu

F32 = jnp.float32
BF16 = jnp.bfloat16

D_MODEL = 2048
CONV_CH = 1024
N_HEADS = 16
QK_NOPE = 128
QK_ROPE = 64
HALF_ROPE = QK_ROPE // 2
QK_DIM = QK_NOPE + QK_ROPE
QK_PAD = 256
V_DIM = 128
Q_LORA = 512
KV_LORA = 512
ROPE_THETA = 10000.0
N_GROUPS = 8
EXP_PER_GROUP = 8
N_EXPERTS = 64
D_EXPERT = 512
EPS = 1e-6
PROJ_W = 8192
KPE_OFF = 4096
LANES = 128
EXP_LANE0 = N_GROUPS
ROW_BLOCK = 256
VMEM_LIMIT = 56 * 1024 * 1024

_NT = (((1,), (1,)), ((), ()))


def _cparams(sem):
    return pltpu.CompilerParams(dimension_semantics=sem, vmem_limit_bytes=VMEM_LIMIT)


D_HALF = D_MODEL // 2
U32 = jnp.uint32


def _pack_halves(x):
    lo = lax.bitcast_convert_type(x[:, :D_HALF].astype(BF16).astype(F32), U32)
    hi = lax.bitcast_convert_type(x[:, D_HALF:].astype(BF16).astype(F32), U32)
    return (lo >> 16) | hi


def _unpack_halves(w):
    lo = lax.bitcast_convert_type(w << 16, F32)
    hi = lax.bitcast_convert_type(w & jnp.uint32(0xFFFF0000), F32)
    return lo, hi


def _inproj_body(x_ref, g_ref, w_ref, wpe_ref, out_ref, kpe_ref, xn_ref):
    @pl.when(pl.program_id(1) == 0)
    def _():
        x = x_ref[...]
        ms = jnp.mean(x * x, axis=-1, keepdims=True)
        xn = (x * lax.rsqrt(ms + EPS) * g_ref[...]).astype(BF16)
        xn_ref[...] = xn
        kpe_ref[...] = jnp.dot(xn, wpe_ref[...], preferred_element_type=F32)

    out_ref[...] = jnp.dot(xn_ref[...], w_ref[...], preferred_element_type=F32).astype(BF16)


def _in_proj(x2d, ln1_g, w_main, w_pe, tm, tn):
    t = x2d.shape[0]
    return pl.pallas_call(
        _inproj_body,
        grid=(t // tm, PROJ_W // tn),
        in_specs=[
            pl.BlockSpec((tm, D_MODEL), lambda i, j: (i, 0)),
            pl.BlockSpec((1, D_MODEL), lambda i, j: (0, 0)),
            pl.BlockSpec((D_MODEL, tn), lambda i, j: (0, j)),
            pl.BlockSpec((D_MODEL, LANES), lambda i, j: (0, 0)),
        ],
        out_specs=[
            pl.BlockSpec((tm, tn), lambda i, j: (i, j)),
            pl.BlockSpec((tm, LANES), lambda i, j: (i, 0)),
        ],
        out_shape=[
            jax.ShapeDtypeStruct((t, PROJ_W), BF16),
            jax.ShapeDtypeStruct((t, LANES), F32),
        ],
        scratch_shapes=[pltpu.VMEM((tm, D_MODEL), BF16)],
        compiler_params=_cparams(("arbitrary", "arbitrary")),
        name="in_proj",
    )(x2d, ln1_g, w_main, w_pe)


def _qkv_body(ql_ref, kvl_ref, kpe_ref, cosk_ref, sink_ref, cost_ref, sint_ref,
              gqa_ref, gkva_ref, gq_ref, gkn_ref, gkr_ref, wq_ref, wkn_ref, wv_ref,
              qt_ref, k_ref, vt_ref):
    def latent_norm(ref, g_ref):
        v = ref[...].astype(F32)
        ms = jnp.mean(v * v, axis=-1, keepdims=True)
        return (v * lax.rsqrt(ms + EPS) * g_ref[...]).astype(BF16)

    qn = latent_norm(ql_ref, gqa_ref)
    kvn = latent_norm(kvl_ref, gkva_ref)
    tm = qn.shape[0]

    kn = jnp.dot(kvn, wkn_ref[...], preferred_element_type=F32)
    kpe = kpe_ref[...]
    ss_pe = jnp.sum(kpe * kpe, axis=-1, keepdims=True)
    kr = kpe * gkr_ref[...]
    kr = kr * cosk_ref[...] + pltpu.roll(kr, 2 * HALF_ROPE, axis=1) * sink_ref[...]
    gkn = gkn_ref[...]
    for h in range(N_HEADS):
        kh = kn[:, h * QK_NOPE:(h + 1) * QK_NOPE]
        ss = jnp.sum(kh * kh, axis=-1, keepdims=True) + ss_pe
        r = lax.rsqrt(ss * (1.0 / QK_DIM) + EPS)
        k_ref[0, h, :, 0:QK_NOPE] = (kh * gkn * r).astype(BF16)
        k_ref[0, h, :, QK_NOPE:QK_PAD] = (kr * r).astype(BF16)

    vt = lax.dot_general(wv_ref[...], kvn, _NT, preferred_element_type=F32)
    for h in range(N_HEADS):
        vt_ref[0, h] = vt[h * V_DIM:(h + 1) * V_DIM, :].astype(BF16)

    cost = cost_ref[...]
    sint = sint_ref[...]
    gq = gq_ref[...]
    zeros = jnp.zeros((HALF_ROPE, tm), BF16)
    for h in range(N_HEADS):
        qt = lax.dot_general(wq_ref[h * QK_PAD:(h + 1) * QK_PAD, :], qn, _NT,
                             preferred_element_type=F32)
        ss = jnp.sum(qt * qt, axis=0, keepdims=True)
        r = lax.rsqrt(ss * (1.0 / QK_DIM) + EPS)
        qs = qt * gq * r
        x1 = qs[QK_NOPE:QK_NOPE + HALF_ROPE]
        x2 = qs[QK_NOPE + 2 * HALF_ROPE:QK_NOPE + 3 * HALF_ROPE]
        qt_ref[0, h, 0:QK_NOPE, :] = qs[0:QK_NOPE].astype(BF16)
        qt_ref[0, h, QK_NOPE:QK_NOPE + HALF_ROPE, :] = (x1 * cost - x2 * sint).astype(BF16)
        qt_ref[0, h, QK_NOPE + HALF_ROPE:QK_NOPE + 2 * HALF_ROPE, :] = zeros
        qt_ref[0, h, QK_NOPE + 2 * HALF_ROPE:QK_NOPE + 3 * HALF_ROPE, :] = (
            x1 * sint + x2 * cost).astype(BF16)
        qt_ref[0, h, QK_NOPE + 3 * HALF_ROPE:QK_PAD, :] = zeros


def _qkv(proj, kpe, tabs, wts, b, s, tm):
    ns = s // tm
    cos_k, sin_k, cos_t, sin_t = tabs
    gqa, gkva, gq, gkn, gkr, wq_t, wkn, wv_t = wts
    const = lambda shape: pl.BlockSpec(shape, lambda bi, i: (0,) * len(shape))
    return pl.pallas_call(
        _qkv_body,
        grid=(b, ns),
        in_specs=[
            pl.BlockSpec((tm, Q_LORA), lambda bi, i: (bi * ns + i, 3 * CONV_CH // Q_LORA)),
            pl.BlockSpec((tm, KV_LORA), lambda bi, i: (bi * ns + i, 3 * CONV_CH // KV_LORA + 1)),
            pl.BlockSpec((tm, LANES), lambda bi, i: (bi * ns + i, 0)),
            pl.BlockSpec((tm, LANES), lambda bi, i: (i, 0)),
            pl.BlockSpec((tm, LANES), lambda bi, i: (i, 0)),
            pl.BlockSpec((HALF_ROPE, tm), lambda bi, i: (0, i)),
            pl.BlockSpec((HALF_ROPE, tm), lambda bi, i: (0, i)),
            const((1, Q_LORA)), const((1, KV_LORA)), const((QK_PAD, 1)),
            const((1, LANES)), const((1, LANES)),
            const((N_HEADS * QK_PAD, Q_LORA)),
            const((KV_LORA, N_HEADS * QK_NOPE)),
            const((N_HEADS * V_DIM, KV_LORA)),
        ],
        out_specs=[
            pl.BlockSpec((1, N_HEADS, QK_PAD, tm), lambda bi, i: (bi, 0, 0, i)),
            pl.BlockSpec((1, N_HEADS, tm, QK_PAD), lambda bi, i: (bi, 0, i, 0)),
            pl.BlockSpec((1, N_HEADS, V_DIM, tm), lambda bi, i: (bi, 0, 0, i)),
        ],
        out_shape=[
            jax.ShapeDtypeStruct((b, N_HEADS, QK_PAD, s), BF16),
            jax.ShapeDtypeStruct((b, N_HEADS, s, QK_PAD), BF16),
            jax.ShapeDtypeStruct((b, N_HEADS, V_DIM, s), BF16),
        ],
        compiler_params=_cparams(("arbitrary", "arbitrary")),
        name="qkv",
    )(proj, proj, kpe, cos_k, sin_k, cos_t, sin_t, gqa, gkva, gq, gkn, gkr, wq_t, wkn, wv_t)


def _attn_body(qt_ref, k_ref, vt_ref, o_ref, s_a, m_a, s_b, m_b, *, kc):
    n = pl.program_id(0)
    s_len = k_ref.shape[2]
    chunks = [(c * kc, (c + 1) * kc) for c in range(s_len // kc)]

    @pl.when(n == 0)
    def _():
        s_b[...] = jnp.zeros_like(s_b)
        m_b[...] = jnp.zeros_like(m_b)

    def step(s_w, m_w, s_r, m_r):
        m_prev = m_r[...]
        qt = qt_ref[0, 0]
        l = None
        acc = None
        m = None
        for lo, hi in chunks:
            p = jnp.exp2(s_r[lo:hi, :] - m_prev)
            lc = jnp.sum(p, axis=0, keepdims=True)
            l = lc if l is None else l + lc
            pv = jnp.dot(vt_ref[0, 0, :, lo:hi], p.astype(BF16), preferred_element_type=F32)
            acc = pv if acc is None else acc + pv

            sc = jnp.dot(k_ref[0, 0, lo:hi, :], qt, preferred_element_type=F32)
            s_w[lo:hi, :] = sc
            mc = jnp.max(sc, axis=0, keepdims=True)
            m = mc if m is None else jnp.maximum(m, mc)
        o_ref[0] = (acc * (1.0 / l)).T.astype(BF16)
        m_w[...] = m

    @pl.when(n % 2 == 0)
    def _():
        step(s_a, m_a, s_b, m_b)

    @pl.when(n % 2 == 1)
    def _():
        step(s_b, m_b, s_a, m_a)


def _attention(qt, k, vt, tq, kc):
    b, _, _, s = qt.shape
    nq = s // tq
    n_tiles = b * N_HEADS * nq

    def bhi(tile):
        return tile // (N_HEADS * nq), (tile // nq) % N_HEADS, tile % nq

    def score_tile(n):
        return bhi(jnp.minimum(n, n_tiles - 1))

    def value_tile(n):
        return bhi(jnp.maximum(n - 1, 0))

    def qt_map(n):
        bi, h, i = score_tile(n)
        return bi, h, 0, i

    def k_map(n):
        bi, h, _ = score_tile(n)
        return bi, h, 0, 0

    def vt_map(n):
        bi, h, _ = value_tile(n)
        return bi, h, 0, 0

    def o_map(n):
        bi, h, i = value_tile(n)
        return bi, i, h

    return pl.pallas_call(
        functools.partial(_attn_body, kc=kc),
        grid=(n_tiles + 1,),
        in_specs=[
            pl.BlockSpec((1, 1, QK_PAD, tq), qt_map),
            pl.BlockSpec((1, 1, s, QK_PAD), k_map),
            pl.BlockSpec((1, 1, V_DIM, s), vt_map),
        ],
        out_specs=pl.BlockSpec((1, tq, V_DIM), o_map),
        out_shape=jax.ShapeDtypeStruct((b, s, N_HEADS * V_DIM), BF16),
        scratch_shapes=[pltpu.VMEM((s, tq), F32), pltpu.VMEM((1, tq), F32),
                        pltpu.VMEM((s, tq), F32), pltpu.VMEM((1, tq), F32)],
        compiler_params=_cparams(("arbitrary",)),
        name="attention",
    )(qt, k, vt)


def _merge_body(u_ref, gb_ref, gc_ref, up_ref, gcp_ref, un_ref, gcn_ref, o_ref, gtc_ref, gta_ref,
                cw_ref, wc_ref, wa_ref, out_ref):
    i = pl.program_id(1)
    last = pl.num_programs(1) - 1
    halo = up_ref.shape[0]
    v = gc_ref[...].astype(F32) * u_ref[...].astype(F32)
    tm = v.shape[0]
    v_before = gcp_ref[halo - 1:halo, :].astype(F32) * up_ref[halo - 1:halo, :].astype(F32)
    v_after = gcn_ref[0:1, :].astype(F32) * un_ref[0:1, :].astype(F32)
    v_before = jnp.where(i == 0, 0.0, v_before)
    v_after = jnp.where(i == last, 0.0, v_after)
    row = lax.broadcasted_iota(jnp.int32, (tm, 1), 0)
    v_prev = jnp.where(row == 0, v_before, pltpu.roll(v, 1, axis=0))
    v_next = jnp.where(row == tm - 1, v_after, pltpu.roll(v, tm - 1, axis=0))
    cw = cw_ref[...]
    conv = v_prev * cw[0:1, :] + v * cw[1:2, :] + v_next * cw[2:3, :]
    z = (gb_ref[...].astype(F32) * conv).astype(BF16)
    conv_out = jnp.dot(z, wc_ref[...], preferred_element_type=F32)
    attn_out = jnp.dot(o_ref[...], wa_ref[...], preferred_element_type=F32)
    merged = (jax.nn.sigmoid(gtc_ref[...].astype(F32)) * conv_out
              + jax.nn.sigmoid(gta_ref[...].astype(F32)) * attn_out)
    out_ref[...] = merged.astype(BF16)


def _merge(proj, o2d, conv_w, wc, wa, b, s, tm, halo):
    ns = s // tm
    t = b * s
    hb = tm // halo
    nh = t // halo
    row = lambda bi, i: bi * ns + i
    prev = lambda bi, i: (jnp.maximum(row(bi, i) * hb - 1, 0))
    nxt = lambda bi, i: (jnp.minimum((row(bi, i) + 1) * hb, nh - 1))
    gate0 = KPE_OFF // D_MODEL
    return pl.pallas_call(
        _merge_body,
        grid=(b, ns),
        in_specs=[
            pl.BlockSpec((tm, CONV_CH), lambda bi, i: (row(bi, i), 0)),
            pl.BlockSpec((tm, CONV_CH), lambda bi, i: (row(bi, i), 1)),
            pl.BlockSpec((tm, CONV_CH), lambda bi, i: (row(bi, i), 2)),
            pl.BlockSpec((halo, CONV_CH), lambda bi, i: (prev(bi, i), 0)),
            pl.BlockSpec((halo, CONV_CH), lambda bi, i: (prev(bi, i), 2)),
            pl.BlockSpec((halo, CONV_CH), lambda bi, i: (nxt(bi, i), 0)),
            pl.BlockSpec((halo, CONV_CH), lambda bi, i: (nxt(bi, i), 2)),
            pl.BlockSpec((tm, D_MODEL), lambda bi, i: (row(bi, i), 0)),
            pl.BlockSpec((tm, D_MODEL), lambda bi, i: (row(bi, i), gate0)),
            pl.BlockSpec((tm, D_MODEL), lambda bi, i: (row(bi, i), gate0 + 1)),
            pl.BlockSpec((3, CONV_CH), lambda bi, i: (0, 0)),
            pl.BlockSpec((CONV_CH, D_MODEL), lambda bi, i: (0, 0), pipeline_mode=pl.Buffered(1)),
            pl.BlockSpec((D_MODEL, D_MODEL), lambda bi, i: (0, 0), pipeline_mode=pl.Buffered(1)),
        ],
        out_specs=pl.BlockSpec((tm, D_MODEL), lambda bi, i: (row(bi, i), 0)),
        out_shape=jax.ShapeDtypeStruct((t, D_MODEL), BF16),
        compiler_params=_cparams(("arbitrary", "arbitrary")),
        name="merge",
    )(proj, proj, proj, proj, proj, proj, proj, o2d, proj, proj, conv_w, wc, wa)


def _out_route_body(xa_ref, mga_ref, xb_ref, mgb_ref, wo_ref, g2_ref, wr_ref, br_ref,
                    h_ref, xn_ref, route_ref, routet_ref, cnt_ref, base_ref, logit_ref, *, steps_a):
    step = pl.program_id(0)
    tm = h_ref.shape[0]
    slot = step % 2

    @pl.when(step == 0)
    def _():
        base_ref[...] = jnp.zeros_like(base_ref)
        logit_ref[...] = jnp.zeros_like(logit_ref)

    in_a = step < steps_a
    mg = jnp.where(in_a, mga_ref[...], mgb_ref[...])
    n_chunks = 4
    cw = D_MODEL // n_chunks

    def h_chunk(c):
        x = jnp.where(in_a, xa_ref[:, c * cw:(c + 1) * cw], xb_ref[:, c * cw:(c + 1) * cw])
        return x + jnp.dot(mg, wo_ref[:, c * cw:(c + 1) * cw], preferred_element_type=F32)

    logits = logit_ref[1 - slot]
    lane = lax.broadcasted_iota(jnp.int32, (tm, LANES), 1)
    neg = -jnp.inf
    lg = jnp.where(lane < N_GROUPS, logits, neg)
    gmax = jnp.max(lg, axis=-1, keepdims=True)
    g_p = 1.0 / jnp.sum(jnp.exp(lg - gmax), axis=-1, keepdims=True)
    g_sel = jnp.min(jnp.where(lg == gmax, lane, LANES), axis=-1, keepdims=True)
    h_parts = [h_chunk(0)]

    lo_lane = EXP_LANE0 + g_sel * EXP_PER_GROUP
    le = jnp.where((lane >= lo_lane) & (lane < lo_lane + EXP_PER_GROUP), logits, neg)
    m1 = jnp.max(le, axis=-1, keepdims=True)
    i1 = jnp.min(jnp.where(le == m1, lane, LANES), axis=-1, keepdims=True)
    le2 = jnp.where(lane == i1, neg, le)
    m2 = jnp.max(le2, axis=-1, keepdims=True)
    i2 = jnp.min(jnp.where(le2 == m2, lane, LANES), axis=-1, keepdims=True)
    e2 = jnp.exp(m2 - m1)
    gate1 = g_p / (1.0 + e2)
    gate2 = g_p * e2 / (1.0 + e2)
    h_parts.append(h_chunk(1))

    sel1 = lane == i1
    sel2 = lane == i2
    onehot = jnp.where(sel1 | sel2, 1.0, 0.0)
    r_i = lax.broadcasted_iota(jnp.int32, (tm, tm), 0)
    c_i = lax.broadcasted_iota(jnp.int32, (tm, tm), 1)
    lower = jnp.where(r_i > c_i, 1.0, 0.0).astype(BF16)
    before = jnp.dot(lower, onehot.astype(BF16), preferred_element_type=F32) + base_ref[...]
    rank1 = jnp.sum(jnp.where(sel1, before, 0.0), axis=-1, keepdims=True)
    rank2 = jnp.sum(jnp.where(sel2, before, 0.0), axis=-1, keepdims=True)
    counted = jnp.where(step > 0, 1.0, 0.0)
    base_ref[...] = base_ref[...] + counted * jnp.sum(onehot, axis=0, keepdims=True)
    cnt_ref[...] = base_ref[...]
    h_parts.append(h_chunk(2))

    e1f = (i1 - EXP_LANE0).astype(F32)
    e2f = (i2 - EXP_LANE0).astype(F32)
    cols = (e1f, e2f, gate1, gate2, rank1, rank2)
    route = jnp.zeros((tm, LANES), F32)
    for c, val in enumerate(cols):
        route = jnp.where(lane == c, val, route)
    route_ref[...] = route
    routet_ref[...] = route.T[0:8, :]
    h_parts.append(h_chunk(3))

    h = jnp.concatenate(h_parts, axis=1)
    h_ref[...] = h
    ms = jnp.mean(h * h, axis=-1, keepdims=True)
    xn = h * lax.rsqrt(ms + EPS) * g2_ref[...]
    xn_ref[...] = _pack_halves(xn)

    hi = xn.astype(BF16)
    lo = (xn - hi.astype(F32)).astype(BF16)
    w_hl = wr_ref[...]
    both = jnp.dot(hi, w_hl, preferred_element_type=F32)
    logit_ref[slot] = (both[:, :LANES] + both[:, LANES:]
                       + jnp.dot(lo, w_hl[:, :LANES], preferred_element_type=F32)) + br_ref[...]


def _out_route(xa, mga, xb, mgb, wo, ln2_g, wr_hl, br, tm):
    steps_a, steps_b = xa.shape[0] // tm, xb.shape[0] // tm
    t = xa.shape[0] + xb.shape[0]
    const = lambda shape: pl.BlockSpec(shape, lambda i: (0,) * len(shape))
    n_tiles = steps_a + steps_b
    seg_a = pl.BlockSpec((tm, D_MODEL), lambda i: (jnp.minimum(i, steps_a - 1), 0))
    seg_b = pl.BlockSpec((tm, D_MODEL), lambda i: (jnp.clip(i - steps_a, 0, steps_b - 1), 0))
    this_tile = lambda i: jnp.minimum(i, n_tiles - 1)
    prev_tile = lambda i: jnp.maximum(i - 1, 0)
    return pl.pallas_call(
        functools.partial(_out_route_body, steps_a=steps_a),
        grid=(n_tiles + 1,),
        in_specs=[
            seg_a, seg_a, seg_b, seg_b,
            const((D_MODEL, D_MODEL)), const((1, D_MODEL)),
            const((D_MODEL, 2 * LANES)), const((1, LANES)),
        ],
        out_specs=[
            pl.BlockSpec((tm, D_MODEL), lambda i: (this_tile(i), 0)),
            pl.BlockSpec((tm, D_HALF), lambda i: (this_tile(i), 0)),
            pl.BlockSpec((tm, LANES), lambda i: (prev_tile(i), 0)),
            pl.BlockSpec((8, tm), lambda i: (0, prev_tile(i))),
            const((1, LANES)),
        ],
        out_shape=[
            jax.ShapeDtypeStruct((t, D_MODEL), F32),
            jax.ShapeDtypeStruct((t, D_HALF), U32),
            jax.ShapeDtypeStruct((t, LANES), F32),
            jax.ShapeDtypeStruct((8, t), F32),
            jax.ShapeDtypeStruct((1, LANES), F32),
        ],
        scratch_shapes=[pltpu.VMEM((1, LANES), F32), pltpu.VMEM((2, tm, LANES), F32)],
        compiler_params=_cparams(("arbitrary",)),
        name="out_route",
    )(xa, mga, xb, mgb, wo, ln2_g, wr_hl, br)


def _block_meta(next_expert, slot, first):
    return (next_expert + 1) * 4 + slot * 2 + first


def _plan_body(cnt_ref, routet_ref, pos_ref, start_ref, blk_exp_ref, blk_src_ref, blk_meta_ref,
               nvalid_ref, next_ref):
    n_blocks = blk_exp_ref.shape[0]

    def backwards(k, following):
        e = N_EXPERTS - 1 - k
        next_ref[e] = following
        return jnp.where(cnt_ref[EXP_LANE0 + e] > 0, e, following)

    lax.fori_loop(0, N_EXPERTS, backwards, -1)

    def per_expert(e, carry):
        acc, ordinal = carry
        c = cnt_ref[EXP_LANE0 + e]
        nb = (c + ROW_BLOCK - 1) // ROW_BLOCK
        start_ref[e] = acc * ROW_BLOCK

        def fill(j, inner):
            blk_exp_ref[acc + j] = e
            blk_src_ref[acc + j] = acc + j
            blk_meta_ref[acc + j] = _block_meta(next_ref[e], ordinal % 2, jnp.where(j == 0, 1, 0))
            return inner

        lax.fori_loop(0, nb, fill, 0)
        return acc + nb, ordinal + jnp.where(nb > 0, 1, 0)

    nvalid, _ = lax.fori_loop(0, N_EXPERTS, per_expert, (0, 0))
    nvalid_ref[0] = nvalid
    last_exp = blk_exp_ref[nvalid - 1]

    def tail(j, carry):
        blk_exp_ref[j] = last_exp
        blk_src_ref[j] = nvalid - 1
        blk_meta_ref[j] = 0
        return carry

    lax.fori_loop(nvalid, n_blocks, tail, 0)

    ef = routet_ref[0:2, :]
    off = jnp.zeros_like(ef)
    for e in range(N_EXPERTS):
        off = jnp.where(ef == float(e), start_ref[e].astype(F32), off)
    pos_ref[...] = (off + routet_ref[4:6, :]).astype(jnp.int32)


def _plan(counts_i32, routet, n_blocks):
    t = routet.shape[1]
    smem = lambda: pl.BlockSpec(memory_space=pltpu.SMEM)
    return pl.pallas_call(
        _plan_body,
        in_specs=[smem(), pl.BlockSpec(memory_space=pltpu.VMEM)],
        out_specs=[pl.BlockSpec(memory_space=pltpu.VMEM), smem(), smem(), smem(), smem(), smem()],
        out_shape=[
            jax.ShapeDtypeStruct((2, t), jnp.int32),
            jax.ShapeDtypeStruct((N_EXPERTS,), jnp.int32),
            jax.ShapeDtypeStruct((n_blocks,), jnp.int32),
            jax.ShapeDtypeStruct((n_blocks,), jnp.int32),
            jax.ShapeDtypeStruct((n_blocks,), jnp.int32),
            jax.ShapeDtypeStruct((1,), jnp.int32),
        ],
        scratch_shapes=[pltpu.SMEM((N_EXPERTS,), jnp.int32)],
        compiler_params=pltpu.CompilerParams(vmem_limit_bytes=VMEM_LIMIT),
        name="plan",
    )(counts_i32, routet)


def _dispatch_body(pos0_ref, pos1_ref, cnt_ref, start_ref, nvalid_ref, xn_ref, xs_hbm, zero_ref, sem):
    tm = xn_ref.shape[0]
    n_blocks = xs_hbm.shape[0] // ROW_BLOCK

    def put(src_ref, src_row, dst_row):
        return pltpu.make_async_copy(src_ref.at[pl.ds(src_row, 1)], xs_hbm.at[pl.ds(dst_row, 1)], sem)

    def put_block(blk):
        return pltpu.make_async_copy(zero_ref, xs_hbm.at[pl.ds(blk * ROW_BLOCK, ROW_BLOCK)], sem)

    @pl.when(pl.program_id(0) == 0)
    def _():
        zero_ref[...] = jnp.zeros_like(zero_ref)

        def tail(blk, carry):
            put_block(blk).start()
            return carry

        def tail_done(blk, carry):
            put_block(0).wait()
            return carry

        lax.fori_loop(nvalid_ref[0], n_blocks, tail, 0)

        def pad_rows(wait):
            def per_expert(e, carry):
                c = cnt_ref[EXP_LANE0 + e]
                end = (c + ROW_BLOCK - 1) // ROW_BLOCK * ROW_BLOCK
                base = start_ref[e]

                def fill(r, inner):
                    if wait:
                        put(zero_ref, 0, 0).wait()
                    else:
                        put(zero_ref, 0, base + r).start()
                    return inner

                lax.fori_loop(c, end, fill, 0)
                return carry

            lax.fori_loop(0, N_EXPERTS, per_expert, 0)

        pad_rows(wait=False)
        lax.fori_loop(nvalid_ref[0], n_blocks, tail_done, 0)
        pad_rows(wait=True)

    for t in range(tm):
        put(xn_ref, t, pos0_ref[t]).start(priority=0)
        put(xn_ref, t, pos1_ref[t]).start(priority=1)

    for t in range(2 * tm):
        put(xn_ref, 0, 0).wait()


def _dispatch(pos0, pos1, counts_i32, starts, nvalid, xn2, n_rows, tm):
    t = xn2.shape[0]
    smem = lambda: pl.BlockSpec(memory_space=pltpu.SMEM)
    pos_spec = lambda: pl.BlockSpec((tm,), lambda i: (i,), memory_space=pltpu.SMEM)
    return pl.pallas_call(
        _dispatch_body,
        grid=(t // tm,),
        in_specs=[
            pos_spec(), pos_spec(),
            smem(), smem(), smem(),
            pl.BlockSpec((tm, D_HALF), lambda i: (i, 0)),
        ],
        out_specs=pl.BlockSpec(memory_space=pl.ANY),
        out_shape=jax.ShapeDtypeStruct((n_rows, D_HALF), U32),
        scratch_shapes=[pltpu.VMEM((ROW_BLOCK, D_HALF), U32), pltpu.SemaphoreType.DMA(())],
        compiler_params=pltpu.CompilerParams(dimension_semantics=("arbitrary",),
                                             vmem_limit_bytes=VMEM_LIMIT, has_side_effects=True),
        name="dispatch",
    )(pos0, pos1, counts_i32, starts, nvalid, xn2)


def _expert_body(blk_exp_ref, blk_src_ref, blk_meta_ref, nvalid_ref, xs_ref, wg_hbm, wu_hbm, wd_hbm,
                 ys_ref, wg_buf, wu_buf, wd_buf, sems):
    i = pl.program_id(0)
    valid = i < nvalid_ref[0]
    meta = blk_meta_ref[i]
    first = meta % 2
    slot = (meta // 2) % 2
    next_expert = meta // 4 - 1

    def weight_copies(expert, to_slot):
        return [pltpu.make_async_copy(hbm.at[expert], buf.at[to_slot], sems.at[to_slot])
                for hbm, buf in ((wg_hbm, wg_buf), (wu_hbm, wu_buf), (wd_hbm, wd_buf))]

    @pl.when(i == 0)
    def _():
        for cp in weight_copies(blk_exp_ref[0], 0):
            cp.start()

    @pl.when(first == 1)
    def _():
        for cp in weight_copies(blk_exp_ref[i], slot):
            cp.wait()

        @pl.when(next_expert >= 0)
        def _():
            for cp in weight_copies(next_expert, 1 - slot):
                cp.start(priority=1)

    @pl.when(valid)
    def _():
        lo, hi = _unpack_halves(xs_ref[...])
        xb = jnp.concatenate([lo.astype(BF16), hi.astype(BF16)], axis=1)
        g = jnp.dot(xb, wg_buf[slot].astype(BF16), preferred_element_type=F32)
        u = jnp.dot(xb, wu_buf[slot].astype(BF16), preferred_element_type=F32)
        hmid = (g * jax.nn.sigmoid(g) * u).astype(BF16)
        ys_ref[...] = _pack_halves(jnp.dot(hmid, wd_buf[slot].astype(BF16), preferred_element_type=F32))

    @pl.when(jnp.logical_not(valid))
    def _():
        ys_ref[...] = jnp.zeros_like(ys_ref)


def _experts(blk_exp, blk_src, blk_meta, nvalid, xs, wg, wu, wd):
    n_rows = xs.shape[0]
    n_blocks = n_rows // ROW_BLOCK
    hbm = lambda: pl.BlockSpec(memory_space=pl.ANY)
    grid_spec = pltpu.PrefetchScalarGridSpec(
        num_scalar_prefetch=4,
        grid=(n_blocks,),
        in_specs=[
            pl.BlockSpec((ROW_BLOCK, D_HALF), lambda i, be, bs, bm, nv: (bs[i], 0)),
            hbm(), hbm(), hbm(),
        ],
        out_specs=pl.BlockSpec((ROW_BLOCK, D_HALF), lambda i, be, bs, bm, nv: (i, 0)),
        scratch_shapes=[
            pltpu.VMEM((2, D_MODEL, D_EXPERT), F32),
            pltpu.VMEM((2, D_MODEL, D_EXPERT), F32),
            pltpu.VMEM((2, D_EXPERT, D_MODEL), F32),
            pltpu.SemaphoreType.DMA((2,)),
        ],
    )
    return pl.pallas_call(
        _expert_body,
        grid_spec=grid_spec,
        out_shape=jax.ShapeDtypeStruct((n_rows, D_HALF), U32),
        compiler_params=_cparams(("arbitrary",)),
        name="experts",
    )(blk_exp, blk_src, blk_meta, nvalid, xs, wg, wu, wd)


def _combine_body(pos0_ref, pos1_ref, pos0_next_ref, pos1_next_ref, ys_hbm, h_ref, route_ref,
                  ya_ref, yb_ref, buf_ref, sems, *, steps_a):
    n = pl.program_id(0)
    tm = h_ref.shape[0]
    slot = n % 2

    def row_copy(row, k, t, to_slot):
        return pltpu.make_async_copy(ys_hbm.at[pl.ds(row, 1)],
                                     buf_ref.at[to_slot, k, pl.ds(t, 1)], sems.at[to_slot])

    def gather(p0_ref, p1_ref, to_slot):
        for t in range(tm):
            row_copy(p0_ref[t], 0, t, to_slot).start(priority=0)
            row_copy(p1_ref[t], 1, t, to_slot).start(priority=1)

    @pl.when(n == 0)
    def _():
        gather(pos0_ref, pos1_ref, 0)

    @pl.when(n + 1 < pl.num_programs(0))
    def _():
        gather(pos0_next_ref, pos1_next_ref, 1 - slot)

    for t in range(tm):
        row_copy(0, 0, t, slot).wait()
        row_copy(0, 1, t, slot).wait()
    route = route_ref[...]
    lo1, hi1 = _unpack_halves(buf_ref[slot, 0])
    lo2, hi2 = _unpack_halves(buf_ref[slot, 1])
    g1, g2 = route[:, 2:3], route[:, 3:4]
    y_lo = h_ref[:, :D_HALF] + g1 * lo1 + g2 * lo2
    y_hi = h_ref[:, D_HALF:] + g1 * hi1 + g2 * hi2

    @pl.when(n < steps_a)
    def _():
        ya_ref[:, :D_HALF] = y_lo
        ya_ref[:, D_HALF:] = y_hi

    @pl.when(n >= steps_a)
    def _():
        yb_ref[:, :D_HALF] = y_lo
        yb_ref[:, D_HALF:] = y_hi


def _combine(pos0, pos1, ys, h, route, t_a, tm):
    t = h.shape[0]
    nt = t // tm
    steps_a = t_a // tm
    cur = lambda: pl.BlockSpec((tm,), lambda i: (i,), memory_space=pltpu.SMEM)
    nxt = lambda: pl.BlockSpec((tm,), lambda i: (jnp.minimum(i + 1, nt - 1),), memory_space=pltpu.SMEM)
    return pl.pallas_call(
        functools.partial(_combine_body, steps_a=steps_a),
        grid=(nt,),
        in_specs=[
            cur(), cur(), nxt(), nxt(),
            pl.BlockSpec(memory_space=pl.ANY),
            pl.BlockSpec((tm, D_MODEL), lambda i: (i, 0)),
            pl.BlockSpec((tm, LANES), lambda i: (i, 0)),
        ],
        out_specs=[
            pl.BlockSpec((tm, D_MODEL), lambda i: (jnp.minimum(i, steps_a - 1), 0)),
            pl.BlockSpec((tm, D_MODEL), lambda i: (jnp.maximum(i - steps_a, 0), 0)),
        ],
        out_shape=[
            jax.ShapeDtypeStruct((t_a, D_MODEL), F32),
            jax.ShapeDtypeStruct((t - t_a, D_MODEL), F32),
        ],
        scratch_shapes=[pltpu.VMEM((2, 2, tm, D_HALF), U32), pltpu.SemaphoreType.DMA((2,))],
        compiler_params=_cparams(("arbitrary",)),
        name="combine",
    )(pos0, pos1, pos0, pos1, ys, h, route)


def _pad_rope(a, axis):
    x1, x2 = jnp.split(a, 2, axis=axis)
    z = jnp.zeros_like(x1)
    return jnp.concatenate([x1, z, x2, z], axis=axis)


def _prepare(ln1_g, w_in, conv_w, q_a_norm_g, w_uq, kv_a_norm_g, w_ukv, q_norm_g, k_norm_g,
             w_conv_out, w_attn_out, w_out, ln2_g, w_router_group, b_router_group,
             w_router_exp, b_router_exp, w_gate, w_up, w_down):
    w_in0 = w_in[0]
    w_main = jnp.concatenate([w_in0[:, :KPE_OFF], w_in0[:, KPE_OFF + QK_ROPE:]], axis=1).astype(BF16)
    w_pe = _pad_rope(w_in0[:, KPE_OFF:KPE_OFF + QK_ROPE], 1).astype(BF16)

    wq = w_uq[0].reshape(Q_LORA, N_HEADS, QK_DIM)
    wq = jnp.concatenate([wq[:, :, :QK_NOPE], _pad_rope(wq[:, :, QK_NOPE:], 2)], axis=2)
    wq_t = wq.reshape(Q_LORA, N_HEADS * QK_PAD).T.astype(BF16)
    wkv = w_ukv[0].reshape(KV_LORA, N_HEADS, QK_NOPE + V_DIM)
    wkn = wkv[:, :, :QK_NOPE].reshape(KV_LORA, N_HEADS * QK_NOPE).astype(BF16)
    wv_t = wkv[:, :, QK_NOPE:].reshape(KV_LORA, N_HEADS * V_DIM).T.astype(BF16)

    qg = q_norm_g[0]
    score_scale = QK_DIM ** -0.5 * math.log2(math.e)
    gq = (jnp.concatenate([qg[:QK_NOPE], _pad_rope(qg[QK_NOPE:], 0)]) * score_scale).reshape(QK_PAD, 1)
    kg = k_norm_g[0]
    gkn = kg[:QK_NOPE].reshape(1, LANES)
    gkr = _pad_rope(kg[QK_NOPE:], 0).reshape(1, LANES)

    wr = jnp.concatenate([w_router_group[0], w_router_exp[0],
                          jnp.zeros((D_MODEL, LANES - N_GROUPS - N_EXPERTS), F32)], axis=1)
    wr_hi = wr.astype(BF16)
    wr_hl = jnp.concatenate([wr_hi, (wr - wr_hi.astype(F32)).astype(BF16)], axis=1)
    br = jnp.concatenate([b_router_group[0], b_router_exp[0],
                          jnp.zeros((LANES - N_GROUPS - N_EXPERTS,), F32)]).reshape(1, LANES)
    return dict(
        ln1_g=ln1_g, w_main=w_main, w_pe=w_pe, conv_w=conv_w[0],
        qkv=(q_a_norm_g, kv_a_norm_g, gq, gkn, gkr, wq_t, wkn, wv_t),
        wc=w_conv_out[0].astype(BF16), wa=w_attn_out[0].astype(BF16), wo=w_out[0].astype(BF16),
        ln2_g=ln2_g, wr_hl=wr_hl, br=br,
        wg=w_gate[0], wu=w_up[0], wd=w_down[0],
    )


def _rope_tables(s):
    inv = ROPE_THETA ** (-jnp.arange(0, QK_ROPE, 2, dtype=F32) / QK_ROPE)
    ang = jnp.arange(s, dtype=F32)[:, None] * inv[None, :]
    cos, sin = jnp.cos(ang), jnp.sin(ang)
    z = jnp.zeros_like(cos)
    cos_k = jnp.concatenate([cos, z, cos, z], axis=1)
    sin_k = jnp.concatenate([-sin, z, sin, z], axis=1)
    return cos_k, sin_k, cos.T, sin.T


def _tiles(b, s):
    t = b * s
    pick = lambda n, pref: pref if n % pref == 0 else n
    return dict(
        in_tm=pick(t, 1024), in_tn=2048,
        qkv_tm=pick(s, 512),
        attn_tq=pick(s, 1024), attn_kc=pick(s, 256),
        merge_tm=pick(s, 512), halo=16,
    )


def _moe_tiles(t_a, t_b):
    both = math.gcd(t_a, t_b)
    pick = lambda pref: pref if both % pref == 0 else both
    return dict(route_tm=pick(512), disp_tm=pick(2048), comb_tm=pick(256))


def _mixer(x, p):
    b, s, _ = x.shape
    t = b * s
    tl = _tiles(b, s)
    x2d = x.reshape(t, D_MODEL)
    proj, kpe = _in_proj(x2d, p["ln1_g"], p["w_main"], p["w_pe"], tl["in_tm"], tl["in_tn"])
    qt, k, vt = _qkv(proj, kpe, _rope_tables(s), p["qkv"], b, s, tl["qkv_tm"])
    o = _attention(qt, k, vt, tl["attn_tq"], tl["attn_kc"])
    merged = _merge(proj, o.reshape(t, D_MODEL), p["conv_w"], p["wc"], p["wa"], b, s,
                    tl["merge_tm"], tl["halo"])
    return x2d, merged


def _forward(x_a, x_b, p):
    xa, mga = _mixer(x_a, p)
    xb, mgb = _mixer(x_b, p)
    t_a, t_b = xa.shape[0], xb.shape[0]
    t = t_a + t_b
    tl = _moe_tiles(t_a, t_b)
    h, xn2, route, route_t, counts = _out_route(xa, mga, xb, mgb, p["wo"], p["ln2_g"], p["wr_hl"],
                                                p["br"], tl["route_tm"])
    n_blocks = 2 * t // ROW_BLOCK + N_EXPERTS
    counts_i32 = counts.reshape(LANES).astype(jnp.int32)
    pos_t, starts, blk_exp, blk_src, blk_meta, nvalid = _plan(counts_i32, route_t, n_blocks)
    pos0, pos1 = pos_t[0], pos_t[1]
    xs = _dispatch(pos0, pos1, counts_i32, starts, nvalid, xn2, n_blocks * ROW_BLOCK, tl["disp_tm"])
    ys = _experts(blk_exp, blk_src, blk_meta, nvalid, xs, p["wg"], p["wu"], p["wd"])
    y_a, y_b = _combine(pos0, pos1, ys, h, route, t_a, tl["comb_tm"])
    return y_a.reshape(x_a.shape), y_b.reshape(x_b.shape)


def kernel(x_prompt, x_sample, ln1_g, w_in, conv_w, q_a_norm_g, w_uq, kv_a_norm_g, w_ukv, q_norm_g,
           k_norm_g, w_conv_out, w_attn_out, w_out, ln2_g, w_router_group, b_router_group,
           w_router_exp, b_router_exp, w_gate, w_up, w_down):
    p = _prepare(ln1_g, w_in, conv_w, q_a_norm_g, w_uq, kv_a_norm_g, w_ukv, q_norm_g, k_norm_g,
                 w_conv_out, w_attn_out, w_out, ln2_g, w_router_group, b_router_group,
                 w_router_exp, b_router_exp, w_gate, w_up, w_down)
    return _forward(x_prompt, x_sample, p)
```

```python
import functools
import math

import jax
import jax.numpy as jnp
from jax import lax
from jax.experimental import pallas as pl
from jax.experimental.pallas import tpu as pltpu

F32 = jnp.float32
BF16 = jnp.bfloat16

D_MODEL = 2048
CONV_CH = 1024
N_HEADS = 16
QK_NOPE = 128
QK_ROPE = 64
HALF_ROPE = QK_ROPE // 2
QK_DIM = QK_NOPE + QK_ROPE
QK_PAD = 256
V_DIM = 128
Q_LORA = 512
KV_LORA = 512
ROPE_THETA = 10000.0
N_GROUPS = 8
EXP_PER_GROUP = 8
N_EXPERTS = 64
D_EXPERT = 512
EPS = 1e-6
PROJ_W = 8192
KPE_OFF = 4096
LANES = 128
EXP_LANE0 = N_GROUPS
ROW_BLOCK = 256
VMEM_LIMIT = 56 * 1024 * 1024

_NT = (((1,), (1,)), ((), ()))


def _cparams(sem):
    return pltpu.CompilerParams(dimension_semantics=sem, vmem_limit_bytes=VMEM_LIMIT)


D_HALF = D_MODEL // 2
U32 = jnp.uint32


def _pack_halves(x):
    lo = lax.bitcast_convert_type(x[:, :D_HALF].astype(BF16).astype(F32), U32)
    hi = lax.bitcast_convert_type(x[:, D_HALF:].astype(BF16).astype(F32), U32)
    return (lo >> 16) | hi


def _unpack_halves(w):
    lo = lax.bitcast_convert_type(w << 16, F32)
    hi = lax.bitcast_convert_type(w & jnp.uint32(0xFFFF0000), F32)
    return lo, hi


def _inproj_body(x_ref, g_ref, w_ref, wpe_ref, out_ref, kpe_ref, xn_ref):
    @pl.when(pl.program_id(1) == 0)
    def _():
        x = x_ref[...]
        ms = jnp.mean(x * x, axis=-1, keepdims=True)
        xn = (x * lax.rsqrt(ms + EPS) * g_ref[...]).astype(BF16)
        xn_ref[...] = xn
        kpe_ref[...] = jnp.dot(xn, wpe_ref[...], preferred_element_type=F32)

    out_ref[...] = jnp.dot(xn_ref[...], w_ref[...], preferred_element_type=F32).astype(BF16)


def _in_proj(x2d, ln1_g, w_main, w_pe, tm, tn):
    t = x2d.shape[0]
    return pl.pallas_call(
        _inproj_body,
        grid=(t // tm, PROJ_W // tn),
        in_specs=[
            pl.BlockSpec((tm, D_MODEL), lambda i, j: (i, 0)),
            pl.BlockSpec((1, D_MODEL), lambda i, j: (0, 0)),
            pl.BlockSpec((D_MODEL, tn), lambda i, j: (0, j)),
            pl.BlockSpec((D_MODEL, LANES), lambda i, j: (0, 0)),
        ],
        out_specs=[
            pl.BlockSpec((tm, tn), lambda i, j: (i, j)),
            pl.BlockSpec((tm, LANES), lambda i, j: (i, 0)),
        ],
        out_shape=[
            jax.ShapeDtypeStruct((t, PROJ_W), BF16),
            jax.ShapeDtypeStruct((t, LANES), F32),
        ],
        scratch_shapes=[pltpu.VMEM((tm, D_MODEL), BF16)],
        compiler_params=_cparams(("arbitrary", "arbitrary")),
        name="in_proj",
    )(x2d, ln1_g, w_main, w_pe)


def _qkv_body(ql_ref, kvl_ref, kpe_ref, cosk_ref, sink_ref, cost_ref, sint_ref,
              gqa_ref, gkva_ref, gq_ref, gkn_ref, gkr_ref, wq_ref, wkn_ref, wv_ref,
              qt_ref, k_ref, vt_ref):
    def latent_norm(ref, g_ref):
        v = ref[...].astype(F32)
        ms = jnp.mean(v * v, axis=-1, keepdims=True)
        return (v * lax.rsqrt(ms + EPS) * g_ref[...]).astype(BF16)

    qn = latent_norm(ql_ref, gqa_ref)
    kvn = latent_norm(kvl_ref, gkva_ref)
    tm = qn.shape[0]

    kn = jnp.dot(kvn, wkn_ref[...], preferred_element_type=F32)
    kpe = kpe_ref[...]
    ss_pe = jnp.sum(kpe * kpe, axis=-1, keepdims=True)
    kr = kpe * gkr_ref[...]
    kr = kr * cosk_ref[...] + pltpu.roll(kr, 2 * HALF_ROPE, axis=1) * sink_ref[...]
    gkn = gkn_ref[...]
    for h in range(N_HEADS):
        kh = kn[:, h * QK_NOPE:(h + 1) * QK_NOPE]
        ss = jnp.sum(kh * kh, axis=-1, keepdims=True) + ss_pe
        r = lax.rsqrt(ss * (1.0 / QK_DIM) + EPS)
        k_ref[0, h, :, 0:QK_NOPE] = (kh * gkn * r).astype(BF16)
        k_ref[0, h, :, QK_NOPE:QK_PAD] = (kr * r).astype(BF16)

    vt = lax.dot_general(wv_ref[...], kvn, _NT, preferred_element_type=F32)
    for h in range(N_HEADS):
        vt_ref[0, h, 0] = vt[h * V_DIM:(h + 1) * V_DIM, :].astype(BF16)

    cost = cost_ref[...]
    sint = sint_ref[...]
    gq = gq_ref[...]
    zeros = jnp.zeros((HALF_ROPE, tm), BF16)
    for h in range(N_HEADS):
        qt = lax.dot_general(wq_ref[h * QK_PAD:(h + 1) * QK_PAD, :], qn, _NT,
                             preferred_element_type=F32)
        ss = jnp.sum(qt * qt, axis=0, keepdims=True)
        r = lax.rsqrt(ss * (1.0 / QK_DIM) + EPS)
        qs = qt * gq * r
        x1 = qs[QK_NOPE:QK_NOPE + HALF_ROPE]
        x2 = qs[QK_NOPE + 2 * HALF_ROPE:QK_NOPE + 3 * HALF_ROPE]
        qt_ref[0, h, 0, 0:QK_NOPE, :] = qs[0:QK_NOPE].astype(BF16)
        qt_ref[0, h, 0, QK_NOPE:QK_NOPE + HALF_ROPE, :] = (x1 * cost - x2 * sint).astype(BF16)
        qt_ref[0, h, 0, QK_NOPE + HALF_ROPE:QK_NOPE + 2 * HALF_ROPE, :] = zeros
        qt_ref[0, h, 0, QK_NOPE + 2 * HALF_ROPE:QK_NOPE + 3 * HALF_ROPE, :] = (
            x1 * sint + x2 * cost).astype(BF16)
        qt_ref[0, h, 0, QK_NOPE + 3 * HALF_ROPE:QK_PAD, :] = zeros


def _qkv(proj, kpe, tabs, wts, b, s, tm):
    ns = s // tm
    cos_k, sin_k, cos_t, sin_t = tabs
    gqa, gkva, gq, gkn, gkr, wq_t, wkn, wv_t = wts
    const = lambda shape: pl.BlockSpec(shape, lambda bi, i: (0,) * len(shape))
    return pl.pallas_call(
        _qkv_body,
        grid=(b, ns),
        in_specs=[
            pl.BlockSpec((tm, Q_LORA), lambda bi, i: (bi * ns + i, 3 * CONV_CH // Q_LORA)),
            pl.BlockSpec((tm, KV_LORA), lambda bi, i: (bi * ns + i, 3 * CONV_CH // KV_LORA + 1)),
            pl.BlockSpec((tm, LANES), lambda bi, i: (bi * ns + i, 0)),
            pl.BlockSpec((tm, LANES), lambda bi, i: (i, 0)),
            pl.BlockSpec((tm, LANES), lambda bi, i: (i, 0)),
            pl.BlockSpec((HALF_ROPE, tm), lambda bi, i: (0, i)),
            pl.BlockSpec((HALF_ROPE, tm), lambda bi, i: (0, i)),
            const((1, Q_LORA)), const((1, KV_LORA)), const((QK_PAD, 1)),
            const((1, LANES)), const((1, LANES)),
            const((N_HEADS * QK_PAD, Q_LORA)),
            const((KV_LORA, N_HEADS * QK_NOPE)),
            const((N_HEADS * V_DIM, KV_LORA)),
        ],
        out_specs=[
            pl.BlockSpec((1, N_HEADS, 1, QK_PAD, tm), lambda bi, i: (bi, 0, i, 0, 0)),
            pl.BlockSpec((1, N_HEADS, tm, QK_PAD), lambda bi, i: (bi, 0, i, 0)),
            pl.BlockSpec((1, N_HEADS, 1, V_DIM, tm), lambda bi, i: (bi, 0, i, 0, 0)),
        ],
        out_shape=[
            jax.ShapeDtypeStruct((b, N_HEADS, ns, QK_PAD, tm), BF16),
            jax.ShapeDtypeStruct((b, N_HEADS, s, QK_PAD), BF16),
            jax.ShapeDtypeStruct((b, N_HEADS, ns, V_DIM, tm), BF16),
        ],
        compiler_params=_cparams(("arbitrary", "arbitrary")),
        name="qkv",
    )(proj, proj, kpe, cos_k, sin_k, cos_t, sin_t, gqa, gkva, gq, gkn, gkr, wq_t, wkn, wv_t)


def _attn_body(qt_ref, k_ref, vt_ref, o_ref, s_a, m_a, s_b, m_b, *, kc):
    n = pl.program_id(0)
    s_len = k_ref.shape[2]
    chunks = [(c * kc, (c + 1) * kc) for c in range(s_len // kc)]

    @pl.when(n == 0)
    def _():
        s_b[...] = jnp.zeros_like(s_b)
        m_b[...] = jnp.zeros_like(m_b)

    def step(s_w, m_w, s_r, m_r):
        m_prev = m_r[...]
        seq_tile = vt_ref.shape[4]
        qt = jnp.concatenate([qt_ref[0, 0, j] for j in range(qt_ref.shape[2])], axis=1)
        l = None
        acc = None
        m = None
        for lo, hi in chunks:
            p = jnp.exp2(s_r[lo:hi, :] - m_prev)
            lc = jnp.sum(p, axis=0, keepdims=True)
            l = lc if l is None else l + lc
            v_chunk = vt_ref[0, 0, lo // seq_tile, :, lo % seq_tile:lo % seq_tile + kc]
            pv = jnp.dot(v_chunk, p.astype(BF16), preferred_element_type=F32)
            acc = pv if acc is None else acc + pv

            sc = jnp.dot(k_ref[0, 0, lo:hi, :], qt, preferred_element_type=F32)
            s_w[lo:hi, :] = sc
            mc = jnp.max(sc, axis=0, keepdims=True)
            m = mc if m is None else jnp.maximum(m, mc)
        o_ref[0, 0] = (acc * (1.0 / l)).T.astype(BF16)
        m_w[...] = m

    @pl.when(n % 2 == 0)
    def _():
        step(s_a, m_a, s_b, m_b)

    @pl.when(n % 2 == 1)
    def _():
        step(s_b, m_b, s_a, m_a)


def _attention(qt, k, vt, tq, kc):
    b, _, ns, _, seq_tile = qt.shape
    s = ns * seq_tile
    assert tq % seq_tile == 0 and seq_tile % kc == 0
    q_tiles = tq // seq_tile
    nq = s // tq
    n_tiles = b * N_HEADS * nq

    def bhi(tile):
        return tile // (N_HEADS * nq), (tile // nq) % N_HEADS, tile % nq

    def score_tile(n):
        return bhi(jnp.minimum(n, n_tiles - 1))

    def value_tile(n):
        return bhi(jnp.maximum(n - 1, 0))

    def qt_map(n):
        bi, h, i = score_tile(n)
        return bi, h, i, 0, 0

    def k_map(n):
        bi, h, _ = score_tile(n)
        return bi, h, 0, 0

    def vt_map(n):
        bi, h, _ = value_tile(n)
        return bi, h, 0, 0, 0

    def o_map(n):
        bi, h, i = value_tile(n)
        return bi, h, i, 0

    return pl.pallas_call(
        functools.partial(_attn_body, kc=kc),
        grid=(n_tiles + 1,),
        in_specs=[
            pl.BlockSpec((1, 1, q_tiles, QK_PAD, seq_tile), qt_map),
            pl.BlockSpec((1, 1, s, QK_PAD), k_map),
            pl.BlockSpec((1, 1, ns, V_DIM, seq_tile), vt_map),
        ],
        out_specs=pl.BlockSpec((1, 1, tq, V_DIM), o_map),
        out_shape=jax.ShapeDtypeStruct((b, N_HEADS, s, V_DIM), BF16),
        scratch_shapes=[pltpu.VMEM((s, tq), F32), pltpu.VMEM((1, tq), F32),
                        pltpu.VMEM((s, tq), F32), pltpu.VMEM((1, tq), F32)],
        compiler_params=_cparams(("arbitrary",)),
        name="attention",
    )(qt, k, vt)


def _merge_body(u_ref, gb_ref, gc_ref, up_ref, gcp_ref, un_ref, gcn_ref, o_ref, gtc_ref, gta_ref,
                cw_ref, wc_ref, wa_ref, out_ref):
    i = pl.program_id(1)
    last = pl.num_programs(1) - 1
    halo = up_ref.shape[0]
    v = gc_ref[...].astype(F32) * u_ref[...].astype(F32)
    tm = v.shape[0]
    v_before = gcp_ref[halo - 1:halo, :].astype(F32) * up_ref[halo - 1:halo, :].astype(F32)
    v_after = gcn_ref[0:1, :].astype(F32) * un_ref[0:1, :].astype(F32)
    v_before = jnp.where(i == 0, 0.0, v_before)
    v_after = jnp.where(i == last, 0.0, v_after)
    row = lax.broadcasted_iota(jnp.int32, (tm, 1), 0)
    v_prev = jnp.where(row == 0, v_before, pltpu.roll(v, 1, axis=0))
    v_next = jnp.where(row == tm - 1, v_after, pltpu.roll(v, tm - 1, axis=0))
    cw = cw_ref[...]
    conv = v_prev * cw[0:1, :] + v * cw[1:2, :] + v_next * cw[2:3, :]
    z = (gb_ref[...].astype(F32) * conv).astype(BF16)
    conv_out = jnp.dot(z, wc_ref[...], preferred_element_type=F32)
    o = jnp.concatenate([o_ref[0, h] for h in range(N_HEADS)], axis=1)
    attn_out = jnp.dot(o, wa_ref[...], preferred_element_type=F32)
    merged = (jax.nn.sigmoid(gtc_ref[...].astype(F32)) * conv_out
              + jax.nn.sigmoid(gta_ref[...].astype(F32)) * attn_out)
    out_ref[...] = merged.astype(BF16)


def _merge(proj, o_heads, conv_w, wc, wa, b, s, tm, halo):
    ns = s // tm
    t = b * s
    hb = tm // halo
    nh = t // halo
    row = lambda bi, i: bi * ns + i
    prev = lambda bi, i: (jnp.maximum(row(bi, i) * hb - 1, 0))
    nxt = lambda bi, i: (jnp.minimum((row(bi, i) + 1) * hb, nh - 1))
    gate0 = KPE_OFF // D_MODEL
    return pl.pallas_call(
        _merge_body,
        grid=(b, ns),
        in_specs=[
            pl.BlockSpec((tm, CONV_CH), lambda bi, i: (row(bi, i), 0)),
            pl.BlockSpec((tm, CONV_CH), lambda bi, i: (row(bi, i), 1)),
            pl.BlockSpec((tm, CONV_CH), lambda bi, i: (row(bi, i), 2)),
            pl.BlockSpec((halo, CONV_CH), lambda bi, i: (prev(bi, i), 0)),
            pl.BlockSpec((halo, CONV_CH), lambda bi, i: (prev(bi, i), 2)),
            pl.BlockSpec((halo, CONV_CH), lambda bi, i: (nxt(bi, i), 0)),
            pl.BlockSpec((halo, CONV_CH), lambda bi, i: (nxt(bi, i), 2)),
            pl.BlockSpec((1, N_HEADS, tm, V_DIM), lambda bi, i: (bi, 0, i, 0)),
            pl.BlockSpec((tm, D_MODEL), lambda bi, i: (row(bi, i), gate0)),
            pl.BlockSpec((tm, D_MODEL), lambda bi, i: (row(bi, i), gate0 + 1)),
            pl.BlockSpec((3, CONV_CH), lambda bi, i: (0, 0)),
            pl.BlockSpec((CONV_CH, D_MODEL), lambda bi, i: (0, 0), pipeline_mode=pl.Buffered(1)),
            pl.BlockSpec((D_MODEL, D_MODEL), lambda bi, i: (0, 0), pipeline_mode=pl.Buffered(1)),
        ],
        out_specs=pl.BlockSpec((tm, D_MODEL), lambda bi, i: (row(bi, i), 0)),
        out_shape=jax.ShapeDtypeStruct((t, D_MODEL), BF16),
        compiler_params=_cparams(("arbitrary", "arbitrary")),
        name="merge",
    )(proj, proj, proj, proj, proj, proj, proj, o_heads, proj, proj, conv_w, wc, wa)


def _out_route_body(xa_ref, mga_ref, xb_ref, mgb_ref, wo_ref, g2_ref, wr_ref, br_ref,
                    h_ref, xn_ref, route_ref, routet_ref, cnt_ref, base_ref, logit_ref, *, steps_a):
    step = pl.program_id(0)
    tm = h_ref.shape[0]
    slot = step % 2

    @pl.when(step == 0)
    def _():
        base_ref[...] = jnp.zeros_like(base_ref)
        logit_ref[...] = jnp.zeros_like(logit_ref)

    in_a = step < steps_a
    mg = jnp.where(in_a, mga_ref[...], mgb_ref[...])
    n_chunks = 4
    cw = D_MODEL // n_chunks

    def h_chunk(c):
        x = jnp.where(in_a, xa_ref[:, c * cw:(c + 1) * cw], xb_ref[:, c * cw:(c + 1) * cw])
        return x + jnp.dot(mg, wo_ref[:, c * cw:(c + 1) * cw], preferred_element_type=F32)

    logits = logit_ref[1 - slot]
    lane = lax.broadcasted_iota(jnp.int32, (tm, LANES), 1)
    neg = -jnp.inf
    lg = jnp.where(lane < N_GROUPS, logits, neg)
    gmax = jnp.max(lg, axis=-1, keepdims=True)
    g_p = 1.0 / jnp.sum(jnp.exp(lg - gmax), axis=-1, keepdims=True)
    g_sel = jnp.min(jnp.where(lg == gmax, lane, LANES), axis=-1, keepdims=True)
    h_parts = [h_chunk(0)]

    lo_lane = EXP_LANE0 + g_sel * EXP_PER_GROUP
    le = jnp.where((lane >= lo_lane) & (lane < lo_lane + EXP_PER_GROUP), logits, neg)
    m1 = jnp.max(le, axis=-1, keepdims=True)
    i1 = jnp.min(jnp.where(le == m1, lane, LANES), axis=-1, keepdims=True)
    le2 = jnp.where(lane == i1, neg, le)
    m2 = jnp.max(le2, axis=-1, keepdims=True)
    i2 = jnp.min(jnp.where(le2 == m2, lane, LANES), axis=-1, keepdims=True)
    e2 = jnp.exp(m2 - m1)
    gate1 = g_p / (1.0 + e2)
    gate2 = g_p * e2 / (1.0 + e2)
    h_parts.append(h_chunk(1))

    sel1 = lane == i1
    sel2 = lane == i2
    onehot = jnp.where(sel1 | sel2, 1.0, 0.0)
    r_i = lax.broadcasted_iota(jnp.int32, (tm, tm), 0)
    c_i = lax.broadcasted_iota(jnp.int32, (tm, tm), 1)
    lower = jnp.where(r_i > c_i, 1.0, 0.0).astype(BF16)
    before = jnp.dot(lower, onehot.astype(BF16), preferred_element_type=F32) + base_ref[...]
    rank1 = jnp.sum(jnp.where(sel1, before, 0.0), axis=-1, keepdims=True)
    rank2 = jnp.sum(jnp.where(sel2, before, 0.0), axis=-1, keepdims=True)
    counted = jnp.where(step > 0, 1.0, 0.0)
    base_ref[...] = base_ref[...] + counted * jnp.sum(onehot, axis=0, keepdims=True)
    cnt_ref[...] = base_ref[...]
    h_parts.append(h_chunk(2))

    e1f = (i1 - EXP_LANE0).astype(F32)
    e2f = (i2 - EXP_LANE0).astype(F32)
    cols = (e1f, e2f, gate1, gate2, rank1, rank2)
    route = jnp.zeros((tm, LANES), F32)
    for c, val in enumerate(cols):
        route = jnp.where(lane == c, val, route)
    route_ref[...] = route
    routet_ref[...] = route.T[0:8, :]
    h_parts.append(h_chunk(3))

    h = jnp.concatenate(h_parts, axis=1)
    h_ref[...] = h
    ms = jnp.mean(h * h, axis=-1, keepdims=True)
    xn = h * lax.rsqrt(ms + EPS) * g2_ref[...]
    xn_ref[...] = _pack_halves(xn)

    hi = xn.astype(BF16)
    lo = (xn - hi.astype(F32)).astype(BF16)
    w_hl = wr_ref[...]
    both = jnp.dot(hi, w_hl, preferred_element_type=F32)
    logit_ref[slot] = (both[:, :LANES] + both[:, LANES:]
                       + jnp.dot(lo, w_hl[:, :LANES], preferred_element_type=F32)) + br_ref[...]


def _out_route(xa, mga, xb, mgb, wo, ln2_g, wr_hl, br, tm):
    steps_a, steps_b = xa.shape[0] // tm, xb.shape[0] // tm
    t = xa.shape[0] + xb.shape[0]
    const = lambda shape: pl.BlockSpec(shape, lambda i: (0,) * len(shape))
    n_tiles = steps_a + steps_b
    seg_a = pl.BlockSpec((tm, D_MODEL), lambda i: (jnp.minimum(i, steps_a - 1), 0))
    seg_b = pl.BlockSpec((tm, D_MODEL), lambda i: (jnp.clip(i - steps_a, 0, steps_b - 1), 0))
    this_tile = lambda i: jnp.minimum(i, n_tiles - 1)
    prev_tile = lambda i: jnp.maximum(i - 1, 0)
    return pl.pallas_call(
        functools.partial(_out_route_body, steps_a=steps_a),
        grid=(n_tiles + 1,),
        in_specs=[
            seg_a, seg_a, seg_b, seg_b,
            const((D_MODEL, D_MODEL)), const((1, D_MODEL)),
            const((D_MODEL, 2 * LANES)), const((1, LANES)),
        ],
        out_specs=[
            pl.BlockSpec((tm, D_MODEL), lambda i: (this_tile(i), 0)),
            pl.BlockSpec((tm, D_HALF), lambda i: (this_tile(i), 0)),
            pl.BlockSpec((tm, LANES), lambda i: (prev_tile(i), 0)),
            pl.BlockSpec((8, tm), lambda i: (0, prev_tile(i))),
            const((1, LANES)),
        ],
        out_shape=[
            jax.ShapeDtypeStruct((t, D_MODEL), F32),
            jax.ShapeDtypeStruct((t, D_HALF), U32),
            jax.ShapeDtypeStruct((t, LANES), F32),
            jax.ShapeDtypeStruct((8, t), F32),
            jax.ShapeDtypeStruct((1, LANES), F32),
        ],
        scratch_shapes=[pltpu.VMEM((1, LANES), F32), pltpu.VMEM((2, tm, LANES), F32)],
        compiler_params=_cparams(("arbitrary",)),
        name="out_route",
    )(xa, mga, xb, mgb, wo, ln2_g, wr_hl, br)


def _block_meta(next_expert, slot, first):
    return (next_expert + 1) * 4 + slot * 2 + first


def _plan_body(cnt_ref, routet_ref, pos_ref, start_ref, blk_exp_ref, blk_src_ref, blk_meta_ref,
               nvalid_ref, next_ref):
    n_blocks = blk_exp_ref.shape[0]

    def backwards(k, following):
        e = N_EXPERTS - 1 - k
        next_ref[e] = following
        return jnp.where(cnt_ref[EXP_LANE0 + e] > 0, e, following)

    lax.fori_loop(0, N_EXPERTS, backwards, -1)

    def per_expert(e, carry):
        acc, ordinal = carry
        c = cnt_ref[EXP_LANE0 + e]
        nb = (c + ROW_BLOCK - 1) // ROW_BLOCK
        start_ref[e] = acc * ROW_BLOCK

        def fill(j, inner):
            blk_exp_ref[acc + j] = e
            blk_src_ref[acc + j] = acc + j
            blk_meta_ref[acc + j] = _block_meta(next_ref[e], ordinal % 2, jnp.where(j == 0, 1, 0))
            return inner

        lax.fori_loop(0, nb, fill, 0)
        return acc + nb, ordinal + jnp.where(nb > 0, 1, 0)

    nvalid, _ = lax.fori_loop(0, N_EXPERTS, per_expert, (0, 0))
    nvalid_ref[0] = nvalid
    last_exp = blk_exp_ref[nvalid - 1]

    def tail(j, carry):
        blk_exp_ref[j] = last_exp
        blk_src_ref[j] = nvalid - 1
        blk_meta_ref[j] = 0
        return carry

    lax.fori_loop(nvalid, n_blocks, tail, 0)

    ef = routet_ref[0:2, :]
    off = jnp.zeros_like(ef)
    for e in range(N_EXPERTS):
        off = jnp.where(ef == float(e), start_ref[e].astype(F32), off)
    pos_ref[...] = (off + routet_ref[4:6, :]).astype(jnp.int32)


def _plan(counts_i32, routet, n_blocks):
    t = routet.shape[1]
    smem = lambda: pl.BlockSpec(memory_space=pltpu.SMEM)
    return pl.pallas_call(
        _plan_body,
        in_specs=[smem(), pl.BlockSpec(memory_space=pltpu.VMEM)],
        out_specs=[pl.BlockSpec(memory_space=pltpu.VMEM), smem(), smem(), smem(), smem(), smem()],
        out_shape=[
            jax.ShapeDtypeStruct((2, t), jnp.int32),
            jax.ShapeDtypeStruct((N_EXPERTS,), jnp.int32),
            jax.ShapeDtypeStruct((n_blocks,), jnp.int32),
            jax.ShapeDtypeStruct((n_blocks,), jnp.int32),
            jax.ShapeDtypeStruct((n_blocks,), jnp.int32),
            jax.ShapeDtypeStruct((1,), jnp.int32),
        ],
        scratch_shapes=[pltpu.SMEM((N_EXPERTS,), jnp.int32)],
        compiler_params=pltpu.CompilerParams(vmem_limit_bytes=VMEM_LIMIT),
        name="plan",
    )(counts_i32, routet)


def _dispatch_body(pos0_ref, pos1_ref, cnt_ref, start_ref, nvalid_ref, xn_ref, xs_hbm, zero_ref, sem):
    tm = xn_ref.shape[0]
    n_blocks = xs_hbm.shape[0] // ROW_BLOCK

    def put(src_ref, src_row, dst_row):
        return pltpu.make_async_copy(src_ref.at[pl.ds(src_row, 1)], xs_hbm.at[pl.ds(dst_row, 1)], sem)

    def put_block(blk):
        return pltpu.make_async_copy(zero_ref, xs_hbm.at[pl.ds(blk * ROW_BLOCK, ROW_BLOCK)], sem)

    @pl.when(pl.program_id(0) == 0)
    def _():
        zero_ref[...] = jnp.zeros_like(zero_ref)

        def tail(blk, carry):
            put_block(blk).start()
            return carry

        def tail_done(blk, carry):
            put_block(0).wait()
            return carry

        lax.fori_loop(nvalid_ref[0], n_blocks, tail, 0)

        def pad_rows(wait):
            def per_expert(e, carry):
                c = cnt_ref[EXP_LANE0 + e]
                end = (c + ROW_BLOCK - 1) // ROW_BLOCK * ROW_BLOCK
                base = start_ref[e]

                def fill(r, inner):
                    if wait:
                        put(zero_ref, 0, 0).wait()
                    else:
                        put(zero_ref, 0, base + r).start()
                    return inner

                lax.fori_loop(c, end, fill, 0)
                return carry

            lax.fori_loop(0, N_EXPERTS, per_expert, 0)

        pad_rows(wait=False)
        lax.fori_loop(nvalid_ref[0], n_blocks, tail_done, 0)
        pad_rows(wait=True)

    for t in range(tm):
        put(xn_ref, t, pos0_ref[t]).start(priority=0)
        put(xn_ref, t, pos1_ref[t]).start(priority=1)

    for t in range(2 * tm):
        put(xn_ref, 0, 0).wait()


def _dispatch(pos0, pos1, counts_i32, starts, nvalid, xn2, n_rows, tm):
    t = xn2.shape[0]
    smem = lambda: pl.BlockSpec(memory_space=pltpu.SMEM)
    pos_spec = lambda: pl.BlockSpec((tm,), lambda i: (i,), memory_space=pltpu.SMEM)
    return pl.pallas_call(
        _dispatch_body,
        grid=(t // tm,),
        in_specs=[
            pos_spec(), pos_spec(),
            smem(), smem(), smem(),
            pl.BlockSpec((tm, D_HALF), lambda i: (i, 0)),
        ],
        out_specs=pl.BlockSpec(memory_space=pl.ANY),
        out_shape=jax.ShapeDtypeStruct((n_rows, D_HALF), U32),
        scratch_shapes=[pltpu.VMEM((ROW_BLOCK, D_HALF), U32), pltpu.SemaphoreType.DMA(())],
        compiler_params=pltpu.CompilerParams(dimension_semantics=("arbitrary",),
                                             vmem_limit_bytes=VMEM_LIMIT, has_side_effects=True),
        name="dispatch",
    )(pos0, pos1, counts_i32, starts, nvalid, xn2)


def _expert_body(blk_exp_ref, blk_src_ref, blk_meta_ref, nvalid_ref, xs_ref, wg_hbm, wu_hbm, wd_hbm,
                 ys_ref, wg_buf, wu_buf, wd_buf, sems):
    i = pl.program_id(0)
    valid = i < nvalid_ref[0]
    meta = blk_meta_ref[i]
    first = meta % 2
    slot = (meta // 2) % 2
    next_expert = meta // 4 - 1

    def weight_copies(expert, to_slot):
        return [pltpu.make_async_copy(hbm.at[expert], buf.at[to_slot], sems.at[to_slot])
                for hbm, buf in ((wg_hbm, wg_buf), (wu_hbm, wu_buf), (wd_hbm, wd_buf))]

    @pl.when(i == 0)
    def _():
        for cp in weight_copies(blk_exp_ref[0], 0):
            cp.start()

    @pl.when(first == 1)
    def _():
        for cp in weight_copies(blk_exp_ref[i], slot):
            cp.wait()

        @pl.when(next_expert >= 0)
        def _():
            for cp in weight_copies(next_expert, 1 - slot):
                cp.start(priority=1)

    @pl.when(valid)
    def _():
        lo, hi = _unpack_halves(xs_ref[...])
        xb = jnp.concatenate([lo.astype(BF16), hi.astype(BF16)], axis=1)
        g = jnp.dot(xb, wg_buf[slot].astype(BF16), preferred_element_type=F32)
        u = jnp.dot(xb, wu_buf[slot].astype(BF16), preferred_element_type=F32)
        hmid = (g * jax.nn.sigmoid(g) * u).astype(BF16)
        ys_ref[...] = _pack_halves(jnp.dot(hmid, wd_buf[slot].astype(BF16), preferred_element_type=F32))

    @pl.when(jnp.logical_not(valid))
    def _():
        ys_ref[...] = jnp.zeros_like(ys_ref)


def _experts(blk_exp, blk_src, blk_meta, nvalid, xs, wg, wu, wd):
    n_rows = xs.shape[0]
    n_blocks = n_rows // ROW_BLOCK
    hbm = lambda: pl.BlockSpec(memory_space=pl.ANY)
    grid_spec = pltpu.PrefetchScalarGridSpec(
        num_scalar_prefetch=4,
        grid=(n_blocks,),
        in_specs=[
            pl.BlockSpec((ROW_BLOCK, D_HALF), lambda i, be, bs, bm, nv: (bs[i], 0)),
            hbm(), hbm(), hbm(),
        ],
        out_specs=pl.BlockSpec((ROW_BLOCK, D_HALF), lambda i, be, bs, bm, nv: (i, 0)),
        scratch_shapes=[
            pltpu.VMEM((2, D_MODEL, D_EXPERT), F32),
            pltpu.VMEM((2, D_MODEL, D_EXPERT), F32),
            pltpu.VMEM((2, D_EXPERT, D_MODEL), F32),
            pltpu.SemaphoreType.DMA((2,)),
        ],
    )
    return pl.pallas_call(
        _expert_body,
        grid_spec=grid_spec,
        out_shape=jax.ShapeDtypeStruct((n_rows, D_HALF), U32),
        compiler_params=_cparams(("arbitrary",)),
        name="experts",
    )(blk_exp, blk_src, blk_meta, nvalid, xs, wg, wu, wd)


def _combine_body(pos0_ref, pos1_ref, pos0_next_ref, pos1_next_ref, ys_hbm, h_ref, route_ref,
                  ya_ref, yb_ref, buf_ref, sems, *, steps_a):
    n = pl.program_id(0)
    tm = h_ref.shape[0]
    slot = n % 2

    def row_copy(row, k, t, to_slot):
        return pltpu.make_async_copy(ys_hbm.at[pl.ds(row, 1)],
                                     buf_ref.at[to_slot, k, pl.ds(t, 1)], sems.at[to_slot])

    def gather(p0_ref, p1_ref, to_slot):
        for t in range(tm):
            row_copy(p0_ref[t], 0, t, to_slot).start(priority=0)
            row_copy(p1_ref[t], 1, t, to_slot).start(priority=1)

    @pl.when(n == 0)
    def _():
        gather(pos0_ref, pos1_ref, 0)

    @pl.when(n + 1 < pl.num_programs(0))
    def _():
        gather(pos0_next_ref, pos1_next_ref, 1 - slot)

    for t in range(tm):
        row_copy(0, 0, t, slot).wait()
        row_copy(0, 1, t, slot).wait()
    route = route_ref[...]
    lo1, hi1 = _unpack_halves(buf_ref[slot, 0])
    lo2, hi2 = _unpack_halves(buf_ref[slot, 1])
    g1, g2 = route[:, 2:3], route[:, 3:4]
    y_lo = h_ref[:, :D_HALF] + g1 * lo1 + g2 * lo2
    y_hi = h_ref[:, D_HALF:] + g1 * hi1 + g2 * hi2

    @pl.when(n < steps_a)
    def _():
        ya_ref[:, :D_HALF] = y_lo
        ya_ref[:, D_HALF:] = y_hi

    @pl.when(n >= steps_a)
    def _():
        yb_ref[:, :D_HALF] = y_lo
        yb_ref[:, D_HALF:] = y_hi


def _combine(pos0, pos1, ys, h, route, t_a, tm):
    t = h.shape[0]
    nt = t // tm
    steps_a = t_a // tm
    cur = lambda: pl.BlockSpec((tm,), lambda i: (i,), memory_space=pltpu.SMEM)
    nxt = lambda: pl.BlockSpec((tm,), lambda i: (jnp.minimum(i + 1, nt - 1),), memory_space=pltpu.SMEM)
    return pl.pallas_call(
        functools.partial(_combine_body, steps_a=steps_a),
        grid=(nt,),
        in_specs=[
            cur(), cur(), nxt(), nxt(),
            pl.BlockSpec(memory_space=pl.ANY),
            pl.BlockSpec((tm, D_MODEL), lambda i: (i, 0)),
            pl.BlockSpec((tm, LANES), lambda i: (i, 0)),
        ],
        out_specs=[
            pl.BlockSpec((tm, D_MODEL), lambda i: (jnp.minimum(i, steps_a - 1), 0)),
            pl.BlockSpec((tm, D_MODEL), lambda i: (jnp.maximum(i - steps_a, 0), 0)),
        ],
        out_shape=[
            jax.ShapeDtypeStruct((t_a, D_MODEL), F32),
            jax.ShapeDtypeStruct((t - t_a, D_MODEL), F32),
        ],
        scratch_shapes=[pltpu.VMEM((2, 2, tm, D_HALF), U32), pltpu.SemaphoreType.DMA((2,))],
        compiler_params=_cparams(("arbitrary",)),
        name="combine",
    )(pos0, pos1, pos0, pos1, ys, h, route)


def _pad_rope(a, axis):
    x1, x2 = jnp.split(a, 2, axis=axis)
    z = jnp.zeros_like(x1)
    return jnp.concatenate([x1, z, x2, z], axis=axis)


def _prepare(ln1_g, w_in, conv_w, q_a_norm_g, w_uq, kv_a_norm_g, w_ukv, q_norm_g, k_norm_g,
             w_conv_out, w_attn_out, w_out, ln2_g, w_router_group, b_router_group,
             w_router_exp, b_router_exp, w_gate, w_up, w_down):
    w_in0 = w_in[0]
    w_main = jnp.concatenate([w_in0[:, :KPE_OFF], w_in0[:, KPE_OFF + QK_ROPE:]], axis=1).astype(BF16)
    w_pe = _pad_rope(w_in0[:, KPE_OFF:KPE_OFF + QK_ROPE], 1).astype(BF16)

    wq = w_uq[0].reshape(Q_LORA, N_HEADS, QK_DIM)
    wq = jnp.concatenate([wq[:, :, :QK_NOPE], _pad_rope(wq[:, :, QK_NOPE:], 2)], axis=2)
    wq_t = wq.reshape(Q_LORA, N_HEADS * QK_PAD).T.astype(BF16)
    wkv = w_ukv[0].reshape(KV_LORA, N_HEADS, QK_NOPE + V_DIM)
    wkn = wkv[:, :, :QK_NOPE].reshape(KV_LORA, N_HEADS * QK_NOPE).astype(BF16)
    wv_t = wkv[:, :, QK_NOPE:].reshape(KV_LORA, N_HEADS * V_DIM).T.astype(BF16)

    qg = q_norm_g[0]
    score_scale = QK_DIM ** -0.5 * math.log2(math.e)
    gq = (jnp.concatenate([qg[:QK_NOPE], _pad_rope(qg[QK_NOPE:], 0)]) * score_scale).reshape(QK_PAD, 1)
    kg = k_norm_g[0]
    gkn = kg[:QK_NOPE].reshape(1, LANES)
    gkr = _pad_rope(kg[QK_NOPE:], 0).reshape(1, LANES)

    wr = jnp.concatenate([w_router_group[0], w_router_exp[0],
                          jnp.zeros((D_MODEL, LANES - N_GROUPS - N_EXPERTS), F32)], axis=1)
    wr_hi = wr.astype(BF16)
    wr_hl = jnp.concatenate([wr_hi, (wr - wr_hi.astype(F32)).astype(BF16)], axis=1)
    br = jnp.concatenate([b_router_group[0], b_router_exp[0],
                          jnp.zeros((LANES - N_GROUPS - N_EXPERTS,), F32)]).reshape(1, LANES)
    return dict(
        ln1_g=ln1_g, w_main=w_main, w_pe=w_pe, conv_w=conv_w[0],
        qkv=(q_a_norm_g, kv_a_norm_g, gq, gkn, gkr, wq_t, wkn, wv_t),
        wc=w_conv_out[0].astype(BF16), wa=w_attn_out[0].astype(BF16), wo=w_out[0].astype(BF16),
        ln2_g=ln2_g, wr_hl=wr_hl, br=br,
        wg=w_gate[0], wu=w_up[0], wd=w_down[0],
    )


def _rope_tables(s):
    inv = ROPE_THETA ** (-jnp.arange(0, QK_ROPE, 2, dtype=F32) / QK_ROPE)
    ang = jnp.arange(s, dtype=F32)[:, None] * inv[None, :]
    cos, sin = jnp.cos(ang), jnp.sin(ang)
    z = jnp.zeros_like(cos)
    cos_k = jnp.concatenate([cos, z, cos, z], axis=1)
    sin_k = jnp.concatenate([-sin, z, sin, z], axis=1)
    return cos_k, sin_k, cos.T, sin.T


def _tiles(b, s):
    t = b * s
    pick = lambda n, pref: pref if n % pref == 0 else n
    return dict(
        in_tm=pick(t, 1024), in_tn=2048,
        qkv_tm=pick(s, 512),
        attn_tq=pick(s, 1024), attn_kc=pick(s, 256),
        merge_tm=pick(s, 512), halo=16,
    )


def _moe_tiles(t_a, t_b):
    both = math.gcd(t_a, t_b)
    pick = lambda pref: pref if both % pref == 0 else both
    return dict(route_tm=pick(512), disp_tm=pick(2048), comb_tm=pick(256))


def _mixer(x, p):
    b, s, _ = x.shape
    t = b * s
    tl = _tiles(b, s)
    x2d = x.reshape(t, D_MODEL)
    proj, kpe = _in_proj(x2d, p["ln1_g"], p["w_main"], p["w_pe"], tl["in_tm"], tl["in_tn"])
    qt, k, vt = _qkv(proj, kpe, _rope_tables(s), p["qkv"], b, s, tl["qkv_tm"])
    o = _attention(qt, k, vt, tl["attn_tq"], tl["attn_kc"])
    merged = _merge(proj, o, p["conv_w"], p["wc"], p["wa"], b, s,
                    tl["merge_tm"], tl["halo"])
    return x2d, merged


def _forward(x_a, x_b, p):
    xa, mga = _mixer(x_a, p)
    xb, mgb = _mixer(x_b, p)
    t_a, t_b = xa.shape[0], xb.shape[0]
    t = t_a + t_b
    tl = _moe_tiles(t_a, t_b)
    h, xn2, route, route_t, counts = _out_route(xa, mga, xb, mgb, p["wo"], p["ln2_g"], p["wr_hl"],
                                                p["br"], tl["route_tm"])
    n_blocks = 2 * t // ROW_BLOCK + N_EXPERTS
    counts_i32 = counts.reshape(LANES).astype(jnp.int32)
    pos_t, starts, blk_exp, blk_src, blk_meta, nvalid = _plan(counts_i32, route_t, n_blocks)
    pos0, pos1 = pos_t[0], pos_t[1]
    xs = _dispatch(pos0, pos1, counts_i32, starts, nvalid, xn2, n_blocks * ROW_BLOCK, tl["disp_tm"])
    ys = _experts(blk_exp, blk_src, blk_meta, nvalid, xs, p["wg"], p["wu"], p["wd"])
    y_a, y_b = _combine(pos0, pos1, ys, h, route, t_a, tl["comb_tm"])
    return y_a.reshape(x_a.shape), y_b.reshape(x_b.shape)


def kernel(x_prompt, x_sample, ln1_g, w_in, conv_w, q_a_norm_g, w_uq, kv_a_norm_g, w_ukv, q_norm_g,
           k_norm_g, w_conv_out, w_attn_out, w_out, ln2_g, w_router_group, b_router_group,
           w_router_exp, b_router_exp, w_gate, w_up, w_down):
    p = _prepare(ln1_g, w_in, conv_w, q_a_norm_g, w_uq, kv_a_norm_g, w_ukv, q_norm_g, k_norm_g,
                 w_conv_out, w_attn_out, w_out, ln2_g, w_router_group, b_router_group,
                 w_router_exp, b_router_exp, w_gate, w_up, w_down)
    return _forward(x_prompt, x_sample, p)
```

```python
import functools
import math

import jax
import jax.numpy as jnp
from jax import lax
from jax.experimental import pallas as pl
from jax.experimental.pallas import tpu as pltpu

F32 = jnp.float32
BF16 = jnp.bfloat16

D_MODEL = 2048
CONV_CH = 1024
N_HEADS = 16
QK_NOPE = 128
QK_ROPE = 64
HALF_ROPE = QK_ROPE // 2
QK_DIM = QK_NOPE + QK_ROPE
QK_PAD = 256
V_DIM = 128
Q_LORA = 512
KV_LORA = 512
ROPE_THETA = 10000.0
N_GROUPS = 8
EXP_PER_GROUP = 8
N_EXPERTS = 64
D_EXPERT = 512
EPS = 1e-6
PROJ_W = 8192
KPE_OFF = 4096
LANES = 128
EXP_LANE0 = N_GROUPS
ROW_BLOCK = 256
VMEM_LIMIT = 56 * 1024 * 1024

_NT = (((1,), (1,)), ((), ()))


def _cparams(sem):
    return pltpu.CompilerParams(dimension_semantics=sem, vmem_limit_bytes=VMEM_LIMIT)


D_HALF = D_MODEL // 2
U32 = jnp.uint32


def _pack_halves(x):
    lo = lax.bitcast_convert_type(x[:, :D_HALF].astype(BF16).astype(F32), U32)
    hi = lax.bitcast_convert_type(x[:, D_HALF:].astype(BF16).astype(F32), U32)
    return (lo >> 16) | hi


def _unpack_halves(w):
    lo = lax.bitcast_convert_type(w << 16, F32)
    hi = lax.bitcast_convert_type(w & jnp.uint32(0xFFFF0000), F32)
    return lo, hi


def _inproj_body(x_ref, g_ref, w_ref, wpe_ref, out_ref, kpe_ref, xn_ref):
    @pl.when(pl.program_id(1) == 0)
    def _():
        x = x_ref[...]
        ms = jnp.mean(x * x, axis=-1, keepdims=True)
        xn = (x * lax.rsqrt(ms + EPS) * g_ref[...]).astype(BF16)
        xn_ref[...] = xn
        kpe_ref[...] = jnp.dot(xn, wpe_ref[...], preferred_element_type=F32)

    out_ref[...] = jnp.dot(xn_ref[...], w_ref[...], preferred_element_type=F32).astype(BF16)


def _in_proj(x2d, ln1_g, w_main, w_pe, tm, tn):
    t = x2d.shape[0]
    return pl.pallas_call(
        _inproj_body,
        grid=(t // tm, PROJ_W // tn),
        in_specs=[
            pl.BlockSpec((tm, D_MODEL), lambda i, j: (i, 0)),
            pl.BlockSpec((1, D_MODEL), lambda i, j: (0, 0)),
            pl.BlockSpec((D_MODEL, tn), lambda i, j: (0, j)),
            pl.BlockSpec((D_MODEL, LANES), lambda i, j: (0, 0)),
        ],
        out_specs=[
            pl.BlockSpec((tm, tn), lambda i, j: (i, j)),
            pl.BlockSpec((tm, LANES), lambda i, j: (i, 0)),
        ],
        out_shape=[
            jax.ShapeDtypeStruct((t, PROJ_W), BF16),
            jax.ShapeDtypeStruct((t, LANES), F32),
        ],
        scratch_shapes=[pltpu.VMEM((tm, D_MODEL), BF16)],
        compiler_params=_cparams(("arbitrary", "arbitrary")),
        name="in_proj",
    )(x2d, ln1_g, w_main, w_pe)


def _qkv_body(ql_ref, kvl_ref, kpe_ref, cosk_ref, sink_ref, cost_ref, sint_ref,
              gqa_ref, gkva_ref, gq_ref, gkn_ref, gkr_ref, wq_ref, wkn_ref, wv_ref,
              qt_ref, k_ref, vt_ref):
    def latent_norm(ref, g_ref):
        v = ref[...].astype(F32)
        ms = jnp.mean(v * v, axis=-1, keepdims=True)
        return (v * lax.rsqrt(ms + EPS) * g_ref[...]).astype(BF16)

    qn = latent_norm(ql_ref, gqa_ref)
    kvn = latent_norm(kvl_ref, gkva_ref)
    tm = qn.shape[0]

    kn = jnp.dot(kvn, wkn_ref[...], preferred_element_type=F32)
    kpe = kpe_ref[...]
    ss_pe = jnp.sum(kpe * kpe, axis=-1, keepdims=True)
    kr = kpe * gkr_ref[...]
    kr = kr * cosk_ref[...] + pltpu.roll(kr, 2 * HALF_ROPE, axis=1) * sink_ref[...]
    gkn = gkn_ref[...]
    for h in range(N_HEADS):
        kh = kn[:, h * QK_NOPE:(h + 1) * QK_NOPE]
        ss = jnp.sum(kh * kh, axis=-1, keepdims=True) + ss_pe
        r = lax.rsqrt(ss * (1.0 / QK_DIM) + EPS)
        k_ref[0, h, :, 0:QK_NOPE] = (kh * gkn * r).astype(BF16)
        k_ref[0, h, :, QK_NOPE:QK_PAD] = (kr * r).astype(BF16)

    vt = lax.dot_general(wv_ref[...], kvn, _NT, preferred_element_type=F32)
    for h in range(N_HEADS):
        vt_ref[0, h] = vt[h * V_DIM:(h + 1) * V_DIM, :].astype(BF16)

    cost = cost_ref[...]
    sint = sint_ref[...]
    gq = gq_ref[...]
    zeros = jnp.zeros((HALF_ROPE, tm), BF16)
    for h in range(N_HEADS):
        qt = lax.dot_general(wq_ref[h * QK_PAD:(h + 1) * QK_PAD, :], qn, _NT,
                             preferred_element_type=F32)
        ss = jnp.sum(qt * qt, axis=0, keepdims=True)
        r = lax.rsqrt(ss * (1.0 / QK_DIM) + EPS)
        qs = qt * gq * r
        x1 = qs[QK_NOPE:QK_NOPE + HALF_ROPE]
        x2 = qs[QK_NOPE + 2 * HALF_ROPE:QK_NOPE + 3 * HALF_ROPE]
        qt_ref[0, h, 0:QK_NOPE, :] = qs[0:QK_NOPE].astype(BF16)
        qt_ref[0, h, QK_NOPE:QK_NOPE + HALF_ROPE, :] = (x1 * cost - x2 * sint).astype(BF16)
        qt_ref[0, h, QK_NOPE + HALF_ROPE:QK_NOPE + 2 * HALF_ROPE, :] = zeros
        qt_ref[0, h, QK_NOPE + 2 * HALF_ROPE:QK_NOPE + 3 * HALF_ROPE, :] = (
            x1 * sint + x2 * cost).astype(BF16)
        qt_ref[0, h, QK_NOPE + 3 * HALF_ROPE:QK_PAD, :] = zeros


def _qkv(proj, kpe, tabs, wts, b, s, tm):
    ns = s // tm
    cos_k, sin_k, cos_t, sin_t = tabs
    gqa, gkva, gq, gkn, gkr, wq_t, wkn, wv_t = wts
    const = lambda shape: pl.BlockSpec(shape, lambda bi, i: (0,) * len(shape))
    return pl.pallas_call(
        _qkv_body,
        grid=(b, ns),
        in_specs=[
            pl.BlockSpec((tm, Q_LORA), lambda bi, i: (bi * ns + i, 3 * CONV_CH // Q_LORA)),
            pl.BlockSpec((tm, KV_LORA), lambda bi, i: (bi * ns + i, 3 * CONV_CH // KV_LORA + 1)),
            pl.BlockSpec((tm, LANES), lambda bi, i: (bi * ns + i, 0)),
            pl.BlockSpec((tm, LANES), lambda bi, i: (i, 0)),
            pl.BlockSpec((tm, LANES), lambda bi, i: (i, 0)),
            pl.BlockSpec((HALF_ROPE, tm), lambda bi, i: (0, i)),
            pl.BlockSpec((HALF_ROPE, tm), lambda bi, i: (0, i)),
            const((1, Q_LORA)), const((1, KV_LORA)), const((QK_PAD, 1)),
            const((1, LANES)), const((1, LANES)),
            const((N_HEADS * QK_PAD, Q_LORA)),
            const((KV_LORA, N_HEADS * QK_NOPE)),
            const((N_HEADS * V_DIM, KV_LORA)),
        ],
        out_specs=[
            pl.BlockSpec((1, N_HEADS, QK_PAD, tm), lambda bi, i: (bi, 0, 0, i)),
            pl.BlockSpec((1, N_HEADS, tm, QK_PAD), lambda bi, i: (bi, 0, i, 0)),
            pl.BlockSpec((1, N_HEADS, V_DIM, tm), lambda bi, i: (bi, 0, 0, i)),
        ],
        out_shape=[
            jax.ShapeDtypeStruct((b, N_HEADS, QK_PAD, s), BF16),
            jax.ShapeDtypeStruct((b, N_HEADS, s, QK_PAD), BF16),
            jax.ShapeDtypeStruct((b, N_HEADS, V_DIM, s), BF16),
        ],
        compiler_params=_cparams(("arbitrary", "arbitrary")),
        name="qkv",
    )(proj, proj, kpe, cos_k, sin_k, cos_t, sin_t, gqa, gkva, gq, gkn, gkr, wq_t, wkn, wv_t)


def _attn_body(qt_ref, k_ref, vt_ref, o_ref, s_a, m_a, s_b, m_b, *, kc):
    n = pl.program_id(0)
    s_len = k_ref.shape[2]
    chunks = [(c * kc, (c + 1) * kc) for c in range(s_len // kc)]

    @pl.when(n == 0)
    def _():
        s_b[...] = jnp.zeros_like(s_b)
        m_b[...] = jnp.zeros_like(m_b)

    def step(s_w, m_w, s_r, m_r):
        m_prev = m_r[...]
        qt = qt_ref[0, 0]
        l = None
        acc = None
        m = None
        for lo, hi in chunks:
            p = jnp.exp2(s_r[lo:hi, :] - m_prev)
            lc = jnp.sum(p, axis=0, keepdims=True)
            l = lc if l is None else l + lc
            pv = jnp.dot(vt_ref[0, 0, :, lo:hi], p.astype(BF16), preferred_element_type=F32)
            acc = pv if acc is None else acc + pv

            sc = jnp.dot(k_ref[0, 0, lo:hi, :], qt, preferred_element_type=F32)
            s_w[lo:hi, :] = sc
            mc = jnp.max(sc, axis=0, keepdims=True)
            m = mc if m is None else jnp.maximum(m, mc)
        o_ref[0] = (acc * (1.0 / l)).T.astype(BF16)
        m_w[...] = m

    @pl.when(n % 2 == 0)
    def _():
        step(s_a, m_a, s_b, m_b)

    @pl.when(n % 2 == 1)
    def _():
        step(s_b, m_b, s_a, m_a)


def _attention(qt, k, vt, tq, kc):
    b, _, _, s = qt.shape
    nq = s // tq
    n_tiles = b * N_HEADS * nq

    def bhi(tile):
        return tile // (N_HEADS * nq), (tile // nq) % N_HEADS, tile % nq

    def score_tile(n):
        return bhi(jnp.minimum(n, n_tiles - 1))

    def value_tile(n):
        return bhi(jnp.maximum(n - 1, 0))

    def qt_map(n):
        bi, h, i = score_tile(n)
        return bi, h, 0, i

    def k_map(n):
        bi, h, _ = score_tile(n)
        return bi, h, 0, 0

    def vt_map(n):
        bi, h, _ = value_tile(n)
        return bi, h, 0, 0

    def o_map(n):
        bi, h, i = value_tile(n)
        return bi, i, h

    return pl.pallas_call(
        functools.partial(_attn_body, kc=kc),
        grid=(n_tiles + 1,),
        in_specs=[
            pl.BlockSpec((1, 1, QK_PAD, tq), qt_map),
            pl.BlockSpec((1, 1, s, QK_PAD), k_map),
            pl.BlockSpec((1, 1, V_DIM, s), vt_map),
        ],
        out_specs=pl.BlockSpec((1, tq, V_DIM), o_map),
        out_shape=jax.ShapeDtypeStruct((b, s, N_HEADS * V_DIM), BF16),
        scratch_shapes=[pltpu.VMEM((s, tq), F32), pltpu.VMEM((1, tq), F32),
                        pltpu.VMEM((s, tq), F32), pltpu.VMEM((1, tq), F32)],
        compiler_params=_cparams(("arbitrary",)),
        name="attention",
    )(qt, k, vt)


def _merge_body(u_ref, gb_ref, gc_ref, up_ref, gcp_ref, un_ref, gcn_ref, o_ref, gtc_ref, gta_ref,
                cw_ref, wc_ref, wa_ref, out_ref):
    i = pl.program_id(1)
    last = pl.num_programs(1) - 1
    halo = up_ref.shape[0]
    v = gc_ref[...].astype(F32) * u_ref[...].astype(F32)
    tm = v.shape[0]
    v_before = gcp_ref[halo - 1:halo, :].astype(F32) * up_ref[halo - 1:halo, :].astype(F32)
    v_after = gcn_ref[0:1, :].astype(F32) * un_ref[0:1, :].astype(F32)
    v_before = jnp.where(i == 0, 0.0, v_before)
    v_after = jnp.where(i == last, 0.0, v_after)
    row = lax.broadcasted_iota(jnp.int32, (tm, 1), 0)
    v_prev = jnp.where(row == 0, v_before, pltpu.roll(v, 1, axis=0))
    v_next = jnp.where(row == tm - 1, v_after, pltpu.roll(v, tm - 1, axis=0))
    cw = cw_ref[...]
    conv = v_prev * cw[0:1, :] + v * cw[1:2, :] + v_next * cw[2:3, :]
    z = (gb_ref[...].astype(F32) * conv).astype(BF16)
    conv_out = jnp.dot(z, wc_ref[...], preferred_element_type=F32)
    attn_out = jnp.dot(o_ref[...], wa_ref[...], preferred_element_type=F32)
    merged = (jax.nn.sigmoid(gtc_ref[...].astype(F32)) * conv_out
              + jax.nn.sigmoid(gta_ref[...].astype(F32)) * attn_out)
    out_ref[...] = merged.astype(BF16)


def _merge(proj, o2d, conv_w, wc, wa, b, s, tm, halo):
    ns = s // tm
    t = b * s
    hb = tm // halo
    nh = t // halo
    row = lambda bi, i: bi * ns + i
    prev = lambda bi, i: (jnp.maximum(row(bi, i) * hb - 1, 0))
    nxt = lambda bi, i: (jnp.minimum((row(bi, i) + 1) * hb, nh - 1))
    gate0 = KPE_OFF // D_MODEL
    return pl.pallas_call(
        _merge_body,
        grid=(b, ns),
        in_specs=[
            pl.BlockSpec((tm, CONV_CH), lambda bi, i: (row(bi, i), 0)),
            pl.BlockSpec((tm, CONV_CH), lambda bi, i: (row(bi, i), 1)),
            pl.BlockSpec((tm, CONV_CH), lambda bi, i: (row(bi, i), 2)),
            pl.BlockSpec((halo, CONV_CH), lambda bi, i: (prev(bi, i), 0)),
            pl.BlockSpec((halo, CONV_CH), lambda bi, i: (prev(bi, i), 2)),
            pl.BlockSpec((halo, CONV_CH), lambda bi, i: (nxt(bi, i), 0)),
            pl.BlockSpec((halo, CONV_CH), lambda bi, i: (nxt(bi, i), 2)),
            pl.BlockSpec((tm, D_MODEL), lambda bi, i: (row(bi, i), 0)),
            pl.BlockSpec((tm, D_MODEL), lambda bi, i: (row(bi, i), gate0)),
            pl.BlockSpec((tm, D_MODEL), lambda bi, i: (row(bi, i), gate0 + 1)),
            pl.BlockSpec((3, CONV_CH), lambda bi, i: (0, 0)),
            pl.BlockSpec((CONV_CH, D_MODEL), lambda bi, i: (0, 0), pipeline_mode=pl.Buffered(1)),
            pl.BlockSpec((D_MODEL, D_MODEL), lambda bi, i: (0, 0), pipeline_mode=pl.Buffered(1)),
        ],
        out_specs=pl.BlockSpec((tm, D_MODEL), lambda bi, i: (row(bi, i), 0)),
        out_shape=jax.ShapeDtypeStruct((t, D_MODEL), BF16),
        compiler_params=_cparams(("arbitrary", "arbitrary")),
        name="merge",
    )(proj, proj, proj, proj, proj, proj, proj, o2d, proj, proj, conv_w, wc, wa)


def _out_route_body(xa_ref, mga_ref, xb_ref, mgb_ref, wo_ref, g2_ref, wr_ref, br_ref,
                    h_ref, xn_ref, route_ref, routet_ref, cnt_ref, base_ref, logit_ref, *, steps_a):
    step = pl.program_id(0)
    tm = h_ref.shape[0]
    slot = step % 2

    @pl.when(step == 0)
    def _():
        base_ref[...] = jnp.zeros_like(base_ref)
        logit_ref[...] = jnp.zeros_like(logit_ref)

    in_a = step < steps_a
    mg = jnp.where(in_a, mga_ref[...], mgb_ref[...])
    n_chunks = 4
    cw = D_MODEL // n_chunks

    def h_chunk(c):
        x = jnp.where(in_a, xa_ref[:, c * cw:(c + 1) * cw], xb_ref[:, c * cw:(c + 1) * cw])
        return x + jnp.dot(mg, wo_ref[:, c * cw:(c + 1) * cw], preferred_element_type=F32)

    logits = logit_ref[1 - slot]
    lane = lax.broadcasted_iota(jnp.int32, (tm, LANES), 1)
    neg = -jnp.inf
    lg = jnp.where(lane < N_GROUPS, logits, neg)
    gmax = jnp.max(lg, axis=-1, keepdims=True)
    g_p = 1.0 / jnp.sum(jnp.exp(lg - gmax), axis=-1, keepdims=True)
    g_sel = jnp.min(jnp.where(lg == gmax, lane, LANES), axis=-1, keepdims=True)
    h_parts = [h_chunk(0)]

    lo_lane = EXP_LANE0 + g_sel * EXP_PER_GROUP
    le = jnp.where((lane >= lo_lane) & (lane < lo_lane + EXP_PER_GROUP), logits, neg)
    m1 = jnp.max(le, axis=-1, keepdims=True)
    i1 = jnp.min(jnp.where(le == m1, lane, LANES), axis=-1, keepdims=True)
    le2 = jnp.where(lane == i1, neg, le)
    m2 = jnp.max(le2, axis=-1, keepdims=True)
    i2 = jnp.min(jnp.where(le2 == m2, lane, LANES), axis=-1, keepdims=True)
    e2 = jnp.exp(m2 - m1)
    gate1 = g_p / (1.0 + e2)
    gate2 = g_p * e2 / (1.0 + e2)
    h_parts.append(h_chunk(1))

    sel1 = lane == i1
    sel2 = lane == i2
    onehot = jnp.where(sel1 | sel2, 1.0, 0.0)
    r_i = lax.broadcasted_iota(jnp.int32, (tm, tm), 0)
    c_i = lax.broadcasted_iota(jnp.int32, (tm, tm), 1)
    lower = jnp.where(r_i > c_i, 1.0, 0.0).astype(BF16)
    before = jnp.dot(lower, onehot.astype(BF16), preferred_element_type=F32) + base_ref[...]
    rank1 = jnp.sum(jnp.where(sel1, before, 0.0), axis=-1, keepdims=True)
    rank2 = jnp.sum(jnp.where(sel2, before, 0.0), axis=-1, keepdims=True)
    counted = jnp.where(step > 0, 1.0, 0.0)
    base_ref[...] = base_ref[...] + counted * jnp.sum(onehot, axis=0, keepdims=True)
    cnt_ref[...] = base_ref[...]
    h_parts.append(h_chunk(2))

    e1f = (i1 - EXP_LANE0).astype(F32)
    e2f = (i2 - EXP_LANE0).astype(F32)
    cols = (e1f, e2f, gate1, gate2, rank1, rank2)
    route = jnp.zeros((tm, LANES), F32)
    for c, val in enumerate(cols):
        route = jnp.where(lane == c, val, route)
    route_ref[...] = route
    routet_ref[...] = route.T[0:8, :]
    h_parts.append(h_chunk(3))

    h = jnp.concatenate(h_parts, axis=1)
    h_ref[...] = h
    ms = jnp.mean(h * h, axis=-1, keepdims=True)
    xn = h * lax.rsqrt(ms + EPS) * g2_ref[...]
    xn_ref[...] = _pack_halves(xn)

    hi = xn.astype(BF16)
    lo = (xn - hi.astype(F32)).astype(BF16)
    w_hl = wr_ref[...]
    both = jnp.dot(hi, w_hl, preferred_element_type=F32)
    logit_ref[slot] = (both[:, :LANES] + both[:, LANES:]
                       + jnp.dot(lo, w_hl[:, :LANES], preferred_element_type=F32)) + br_ref[...]


def _out_route(xa, mga, xb, mgb, wo, ln2_g, wr_hl, br, tm):
    steps_a, steps_b = xa.shape[0] // tm, xb.shape[0] // tm
    t = xa.shape[0] + xb.shape[0]
    const = lambda shape: pl.BlockSpec(shape, lambda i: (0,) * len(shape))
    n_tiles = steps_a + steps_b
    seg_a = pl.BlockSpec((tm, D_MODEL), lambda i: (jnp.minimum(i, steps_a - 1), 0))
    seg_b = pl.BlockSpec((tm, D_MODEL), lambda i: (jnp.clip(i - steps_a, 0, steps_b - 1), 0))
    this_tile = lambda i: jnp.minimum(i, n_tiles - 1)
    prev_tile = lambda i: jnp.maximum(i - 1, 0)
    return pl.pallas_call(
        functools.partial(_out_route_body, steps_a=steps_a),
        grid=(n_tiles + 1,),
        in_specs=[
            seg_a, seg_a, seg_b, seg_b,
            const((D_MODEL, D_MODEL)), const((1, D_MODEL)),
            const((D_MODEL, 2 * LANES)), const((1, LANES)),
        ],
        out_specs=[
            pl.BlockSpec((tm, D_MODEL), lambda i: (this_tile(i), 0)),
            pl.BlockSpec((tm, D_HALF), lambda i: (this_tile(i), 0)),
            pl.BlockSpec((tm, LANES), lambda i: (prev_tile(i), 0)),
            pl.BlockSpec((8, tm), lambda i: (0, prev_tile(i))),
            const((1, LANES)),
        ],
        out_shape=[
            jax.ShapeDtypeStruct((t, D_MODEL), F32),
            jax.ShapeDtypeStruct((t, D_HALF), U32),
            jax.ShapeDtypeStruct((t, LANES), F32),
            jax.ShapeDtypeStruct((8, t), F32),
            jax.ShapeDtypeStruct((1, LANES), F32),
        ],
        scratch_shapes=[pltpu.VMEM((1, LANES), F32), pltpu.VMEM((2, tm, LANES), F32)],
        compiler_params=_cparams(("arbitrary",)),
        name="out_route",
    )(xa, mga, xb, mgb, wo, ln2_g, wr_hl, br)


def _block_meta(next_expert, slot, first):
    return (next_expert + 1) * 4 + slot * 2 + first


def _plan_body(cnt_ref, routet_ref, pos_ref, start_ref, blk_exp_ref, blk_src_ref, blk_meta_ref,
               nvalid_ref, next_ref):
    n_blocks = blk_exp_ref.shape[0]

    def backwards(k, following):
        e = N_EXPERTS - 1 - k
        next_ref[e] = following
        return jnp.where(cnt_ref[EXP_LANE0 + e] > 0, e, following)

    lax.fori_loop(0, N_EXPERTS, backwards, -1)

    def per_expert(e, carry):
        acc, ordinal = carry
        c = cnt_ref[EXP_LANE0 + e]
        nb = (c + ROW_BLOCK - 1) // ROW_BLOCK
        start_ref[e] = acc * ROW_BLOCK

        def fill(j, inner):
            blk_exp_ref[acc + j] = e
            blk_src_ref[acc + j] = acc + j
            blk_meta_ref[acc + j] = _block_meta(next_ref[e], ordinal % 2, jnp.where(j == 0, 1, 0))
            return inner

        lax.fori_loop(0, nb, fill, 0)
        return acc + nb, ordinal + jnp.where(nb > 0, 1, 0)

    nvalid, _ = lax.fori_loop(0, N_EXPERTS, per_expert, (0, 0))
    nvalid_ref[0] = nvalid
    last_exp = blk_exp_ref[nvalid - 1]

    def tail(j, carry):
        blk_exp_ref[j] = last_exp
        blk_src_ref[j] = nvalid - 1
        blk_meta_ref[j] = 0
        return carry

    lax.fori_loop(nvalid, n_blocks, tail, 0)

    ef = routet_ref[0:2, :]
    off = jnp.zeros_like(ef)
    for e in range(N_EXPERTS):
        off = jnp.where(ef == float(e), start_ref[e].astype(F32), off)
    pos_ref[...] = (off + routet_ref[4:6, :]).astype(jnp.int32)


def _plan(counts_i32, routet, n_blocks):
    t = routet.shape[1]
    smem = lambda: pl.BlockSpec(memory_space=pltpu.SMEM)
    return pl.pallas_call(
        _plan_body,
        in_specs=[smem(), pl.BlockSpec(memory_space=pltpu.VMEM)],
        out_specs=[pl.BlockSpec(memory_space=pltpu.VMEM), smem(), smem(), smem(), smem(), smem()],
        out_shape=[
            jax.ShapeDtypeStruct((2, t), jnp.int32),
            jax.ShapeDtypeStruct((N_EXPERTS,), jnp.int32),
            jax.ShapeDtypeStruct((n_blocks,), jnp.int32),
            jax.ShapeDtypeStruct((n_blocks,), jnp.int32),
            jax.ShapeDtypeStruct((n_blocks,), jnp.int32),
            jax.ShapeDtypeStruct((1,), jnp.int32),
        ],
        scratch_shapes=[pltpu.SMEM((N_EXPERTS,), jnp.int32)],
        compiler_params=pltpu.CompilerParams(vmem_limit_bytes=VMEM_LIMIT),
        name="plan",
    )(counts_i32, routet)


def _dispatch_body(pos0_ref, pos1_ref, cnt_ref, start_ref, nvalid_ref, xn_ref, xs_hbm, zero_ref, sem):
    tm = xn_ref.shape[0]
    n_blocks = xs_hbm.shape[0] // ROW_BLOCK

    def put(src_ref, src_row, dst_row):
        return pltpu.make_async_copy(src_ref.at[pl.ds(src_row, 1)], xs_hbm.at[pl.ds(dst_row, 1)], sem)

    def put_block(blk):
        return pltpu.make_async_copy(zero_ref, xs_hbm.at[pl.ds(blk * ROW_BLOCK, ROW_BLOCK)], sem)

    @pl.when(pl.program_id(0) == 0)
    def _():
        zero_ref[...] = jnp.zeros_like(zero_ref)

        def tail(blk, carry):
            put_block(blk).start()
            return carry

        def tail_done(blk, carry):
            put_block(0).wait()
            return carry

        lax.fori_loop(nvalid_ref[0], n_blocks, tail, 0)

        def pad_rows(wait):
            def per_expert(e, carry):
                c = cnt_ref[EXP_LANE0 + e]
                end = (c + ROW_BLOCK - 1) // ROW_BLOCK * ROW_BLOCK
                base = start_ref[e]
                head = (8 - c % 8) % 8

                def fill(r, inner):
                    if wait:
                        put(zero_ref, 0, 0).wait()
                    else:
                        put(zero_ref, 0, base + c + r).start()
                    return inner

                lax.fori_loop(0, jnp.minimum(head, end - c), fill, 0)
                groups = (end - c - head) // 8
                cur = base + c + head
                run = ROW_BLOCK // 16
                while run >= 1:
                    rows = 8 * run

                    @pl.when((groups // run) % 2 == 1)
                    def _(rows=rows, cur=cur):
                        cp = pltpu.make_async_copy(zero_ref.at[pl.ds(0, rows)],
                                                   xs_hbm.at[pl.ds(pl.multiple_of(cur, 8), rows)], sem)
                        if wait:
                            cp.wait()
                        else:
                            cp.start()

                    cur = cur + rows * ((groups // run) % 2)
                    run //= 2
                return carry

            lax.fori_loop(0, N_EXPERTS, per_expert, 0)

        pad_rows(wait=False)
        lax.fori_loop(nvalid_ref[0], n_blocks, tail_done, 0)
        pad_rows(wait=True)

    for t in range(tm):
        put(xn_ref, t, pos0_ref[t]).start(priority=0)
        put(xn_ref, t, pos1_ref[t]).start(priority=1)

    for t in range(2 * tm):
        put(xn_ref, 0, 0).wait()


def _dispatch(pos0, pos1, counts_i32, starts, nvalid, xn2, n_rows, tm):
    t = xn2.shape[0]
    smem = lambda: pl.BlockSpec(memory_space=pltpu.SMEM)
    pos_spec = lambda: pl.BlockSpec((tm,), lambda i: (i,), memory_space=pltpu.SMEM)
    return pl.pallas_call(
        _dispatch_body,
        grid=(t // tm,),
        in_specs=[
            pos_spec(), pos_spec(),
            smem(), smem(), smem(),
            pl.BlockSpec((tm, D_HALF), lambda i: (i, 0)),
        ],
        out_specs=pl.BlockSpec(memory_space=pl.ANY),
        out_shape=jax.ShapeDtypeStruct((n_rows, D_HALF), U32),
        scratch_shapes=[pltpu.VMEM((ROW_BLOCK, D_HALF), U32), pltpu.SemaphoreType.DMA(())],
        compiler_params=pltpu.CompilerParams(dimension_semantics=("arbitrary",),
                                             vmem_limit_bytes=VMEM_LIMIT, has_side_effects=True),
        name="dispatch",
    )(pos0, pos1, counts_i32, starts, nvalid, xn2)


def _expert_body(blk_exp_ref, blk_src_ref, blk_meta_ref, nvalid_ref, xs_ref, wg_hbm, wu_hbm, wd_hbm,
                 ys_ref, wg_buf, wu_buf, wd_buf, sems):
    i = pl.program_id(0)
    valid = i < nvalid_ref[0]
    meta = blk_meta_ref[i]
    first = meta % 2
    slot = (meta // 2) % 2
    next_expert = meta // 4 - 1

    def weight_copies(expert, to_slot):
        return [pltpu.make_async_copy(hbm.at[expert], buf.at[to_slot], sems.at[to_slot])
                for hbm, buf in ((wg_hbm, wg_buf), (wu_hbm, wu_buf), (wd_hbm, wd_buf))]

    @pl.when(i == 0)
    def _():
        for cp in weight_copies(blk_exp_ref[0], 0):
            cp.start()

    @pl.when(first == 1)
    def _():
        for cp in weight_copies(blk_exp_ref[i], slot):
            cp.wait()

        @pl.when(next_expert >= 0)
        def _():
            for cp in weight_copies(next_expert, 1 - slot):
                cp.start(priority=1)

    @pl.when(valid)
    def _():
        lo, hi = _unpack_halves(xs_ref[...])
        xb = jnp.concatenate([lo.astype(BF16), hi.astype(BF16)], axis=1)
        g = jnp.dot(xb, wg_buf[slot].astype(BF16), preferred_element_type=F32)
        u = jnp.dot(xb, wu_buf[slot].astype(BF16), preferred_element_type=F32)
        hmid = (g * jax.nn.sigmoid(g) * u).astype(BF16)
        ys_ref[...] = _pack_halves(jnp.dot(hmid, wd_buf[slot].astype(BF16), preferred_element_type=F32))

    @pl.when(jnp.logical_not(valid))
    def _():
        ys_ref[...] = jnp.zeros_like(ys_ref)


def _experts(blk_exp, blk_src, blk_meta, nvalid, xs, wg, wu, wd):
    n_rows = xs.shape[0]
    n_blocks = n_rows // ROW_BLOCK
    hbm = lambda: pl.BlockSpec(memory_space=pl.ANY)
    grid_spec = pltpu.PrefetchScalarGridSpec(
        num_scalar_prefetch=4,
        grid=(n_blocks,),
        in_specs=[
            pl.BlockSpec((ROW_BLOCK, D_HALF), lambda i, be, bs, bm, nv: (bs[i], 0)),
            hbm(), hbm(), hbm(),
        ],
        out_specs=pl.BlockSpec((ROW_BLOCK, D_HALF), lambda i, be, bs, bm, nv: (i, 0)),
        scratch_shapes=[
            pltpu.VMEM((2, D_MODEL, D_EXPERT), F32),
            pltpu.VMEM((2, D_MODEL, D_EXPERT), F32),
            pltpu.VMEM((2, D_EXPERT, D_MODEL), F32),
            pltpu.SemaphoreType.DMA((2,)),
        ],
    )
    return pl.pallas_call(
        _expert_body,
        grid_spec=grid_spec,
        out_shape=jax.ShapeDtypeStruct((n_rows, D_HALF), U32),
        compiler_params=_cparams(("arbitrary",)),
        name="experts",
    )(blk_exp, blk_src, blk_meta, nvalid, xs, wg, wu, wd)


def _combine_body(pos0_ref, pos1_ref, pos0_next_ref, pos1_next_ref, ys_hbm, h_ref, route_ref,
                  ya_ref, yb_ref, buf_ref, sems, *, steps_a):
    n = pl.program_id(0)
    tm = h_ref.shape[0]
    slot = n % 2

    def row_copy(row, k, t, to_slot):
        return pltpu.make_async_copy(ys_hbm.at[pl.ds(row, 1)],
                                     buf_ref.at[to_slot, k, pl.ds(t, 1)], sems.at[to_slot])

    def gather(p0_ref, p1_ref, to_slot):
        for t in range(tm):
            row_copy(p0_ref[t], 0, t, to_slot).start(priority=0)
            row_copy(p1_ref[t], 1, t, to_slot).start(priority=1)

    @pl.when(n == 0)
    def _():
        gather(pos0_ref, pos1_ref, 0)

    @pl.when(n + 1 < pl.num_programs(0))
    def _():
        gather(pos0_next_ref, pos1_next_ref, 1 - slot)

    for t in range(tm):
        row_copy(0, 0, t, slot).wait()
        row_copy(0, 1, t, slot).wait()
    route = route_ref[...]
    lo1, hi1 = _unpack_halves(buf_ref[slot, 0])
    lo2, hi2 = _unpack_halves(buf_ref[slot, 1])
    g1, g2 = route[:, 2:3], route[:, 3:4]
    y_lo = h_ref[:, :D_HALF] + g1 * lo1 + g2 * lo2
    y_hi = h_ref[:, D_HALF:] + g1 * hi1 + g2 * hi2

    @pl.when(n < steps_a)
    def _():
        ya_ref[:, :D_HALF] = y_lo
        ya_ref[:, D_HALF:] = y_hi

    @pl.when(n >= steps_a)
    def _():
        yb_ref[:, :D_HALF] = y_lo
        yb_ref[:, D_HALF:] = y_hi


def _combine(pos0, pos1, ys, h, route, t_a, tm):
    t = h.shape[0]
    nt = t // tm
    steps_a = t_a // tm
    cur = lambda: pl.BlockSpec((tm,), lambda i: (i,), memory_space=pltpu.SMEM)
    nxt = lambda: pl.BlockSpec((tm,), lambda i: (jnp.minimum(i + 1, nt - 1),), memory_space=pltpu.SMEM)
    return pl.pallas_call(
        functools.partial(_combine_body, steps_a=steps_a),
        grid=(nt,),
        in_specs=[
            cur(), cur(), nxt(), nxt(),
            pl.BlockSpec(memory_space=pl.ANY),
            pl.BlockSpec((tm, D_MODEL), lambda i: (i, 0)),
            pl.BlockSpec((tm, LANES), lambda i: (i, 0)),
        ],
        out_specs=[
            pl.BlockSpec((tm, D_MODEL), lambda i: (jnp.minimum(i, steps_a - 1), 0)),
            pl.BlockSpec((tm, D_MODEL), lambda i: (jnp.maximum(i - steps_a, 0), 0)),
        ],
        out_shape=[
            jax.ShapeDtypeStruct((t_a, D_MODEL), F32),
            jax.ShapeDtypeStruct((t - t_a, D_MODEL), F32),
        ],
        scratch_shapes=[pltpu.VMEM((2, 2, tm, D_HALF), U32), pltpu.SemaphoreType.DMA((2,))],
        compiler_params=_cparams(("arbitrary",)),
        name="combine",
    )(pos0, pos1, pos0, pos1, ys, h, route)


def _pad_rope(a, axis):
    x1, x2 = jnp.split(a, 2, axis=axis)
    z = jnp.zeros_like(x1)
    return jnp.concatenate([x1, z, x2, z], axis=axis)


def _prepare(ln1_g, w_in, conv_w, q_a_norm_g, w_uq, kv_a_norm_g, w_ukv, q_norm_g, k_norm_g,
             w_conv_out, w_attn_out, w_out, ln2_g, w_router_group, b_router_group,
             w_router_exp, b_router_exp, w_gate, w_up, w_down):
    w_in0 = w_in[0]
    w_main = jnp.concatenate([w_in0[:, :KPE_OFF], w_in0[:, KPE_OFF + QK_ROPE:]], axis=1).astype(BF16)
    w_pe = _pad_rope(w_in0[:, KPE_OFF:KPE_OFF + QK_ROPE], 1).astype(BF16)

    wq = w_uq[0].reshape(Q_LORA, N_HEADS, QK_DIM)
    wq = jnp.concatenate([wq[:, :, :QK_NOPE], _pad_rope(wq[:, :, QK_NOPE:], 2)], axis=2)
    wq_t = wq.reshape(Q_LORA, N_HEADS * QK_PAD).T.astype(BF16)
    wkv = w_ukv[0].reshape(KV_LORA, N_HEADS, QK_NOPE + V_DIM)
    wkn = wkv[:, :, :QK_NOPE].reshape(KV_LORA, N_HEADS * QK_NOPE).astype(BF16)
    wv_t = wkv[:, :, QK_NOPE:].reshape(KV_LORA, N_HEADS * V_DIM).T.astype(BF16)

    qg = q_norm_g[0]
    score_scale = QK_DIM ** -0.5 * math.log2(math.e)
    gq = (jnp.concatenate([qg[:QK_NOPE], _pad_rope(qg[QK_NOPE:], 0)]) * score_scale).reshape(QK_PAD, 1)
    kg = k_norm_g[0]
    gkn = kg[:QK_NOPE].reshape(1, LANES)
    gkr = _pad_rope(kg[QK_NOPE:], 0).reshape(1, LANES)

    wr = jnp.concatenate([w_router_group[0], w_router_exp[0],
                          jnp.zeros((D_MODEL, LANES - N_GROUPS - N_EXPERTS), F32)], axis=1)
    wr_hi = wr.astype(BF16)
    wr_hl = jnp.concatenate([wr_hi, (wr - wr_hi.astype(F32)).astype(BF16)], axis=1)
    br = jnp.concatenate([b_router_group[0], b_router_exp[0],
                          jnp.zeros((LANES - N_GROUPS - N_EXPERTS,), F32)]).reshape(1, LANES)
    return dict(
        ln1_g=ln1_g, w_main=w_main, w_pe=w_pe, conv_w=conv_w[0],
        qkv=(q_a_norm_g, kv_a_norm_g, gq, gkn, gkr, wq_t, wkn, wv_t),
        wc=w_conv_out[0].astype(BF16), wa=w_attn_out[0].astype(BF16), wo=w_out[0].astype(BF16),
        ln2_g=ln2_g, wr_hl=wr_hl, br=br,
        wg=w_gate[0], wu=w_up[0], wd=w_down[0],
    )


def _rope_tables(s):
    inv = ROPE_THETA ** (-jnp.arange(0, QK_ROPE, 2, dtype=F32) / QK_ROPE)
    ang = jnp.arange(s, dtype=F32)[:, None] * inv[None, :]
    cos, sin = jnp.cos(ang), jnp.sin(ang)
    z = jnp.zeros_like(cos)
    cos_k = jnp.concatenate([cos, z, cos, z], axis=1)
    sin_k = jnp.concatenate([-sin, z, sin, z], axis=1)
    return cos_k, sin_k, cos.T, sin.T


def _tiles(b, s):
    t = b * s
    pick = lambda n, pref: pref if n % pref == 0 else n
    return dict(
        in_tm=pick(t, 1024), in_tn=2048,
        qkv_tm=pick(s, 512),
        attn_tq=pick(s, 1024), attn_kc=pick(s, 256),
        merge_tm=pick(s, 512), halo=16,
    )


def _moe_tiles(t_a, t_b):
    both = math.gcd(t_a, t_b)
    pick = lambda pref: pref if both % pref == 0 else both
    return dict(route_tm=pick(512), disp_tm=pick(2048), comb_tm=pick(256))


def _mixer(x, p):
    b, s, _ = x.shape
    t = b * s
    tl = _tiles(b, s)
    x2d = x.reshape(t, D_MODEL)
    proj, kpe = _in_proj(x2d, p["ln1_g"], p["w_main"], p["w_pe"], tl["in_tm"], tl["in_tn"])
    qt, k, vt = _qkv(proj, kpe, _rope_tables(s), p["qkv"], b, s, tl["qkv_tm"])
    o = _attention(qt, k, vt, tl["attn_tq"], tl["attn_kc"])
    merged = _merge(proj, o.reshape(t, D_MODEL), p["conv_w"], p["wc"], p["wa"], b, s,
                    tl["merge_tm"], tl["halo"])
    return x2d, merged


def _forward(x_a, x_b, p):
    xa, mga = _mixer(x_a, p)
    xb, mgb = _mixer(x_b, p)
    t_a, t_b = xa.shape[0], xb.shape[0]
    t = t_a + t_b
    tl = _moe_tiles(t_a, t_b)
    h, xn2, route, route_t, counts = _out_route(xa, mga, xb, mgb, p["wo"], p["ln2_g"], p["wr_hl"],
                                                p["br"], tl["route_tm"])
    n_blocks = 2 * t // ROW_BLOCK + N_EXPERTS
    counts_i32 = counts.reshape(LANES).astype(jnp.int32)
    pos_t, starts, blk_exp, blk_src, blk_meta, nvalid = _plan(counts_i32, route_t, n_blocks)
    pos0, pos1 = pos_t[0], pos_t[1]
    xs = _dispatch(pos0, pos1, counts_i32, starts, nvalid, xn2, n_blocks * ROW_BLOCK, tl["disp_tm"])
    ys = _experts(blk_exp, blk_src, blk_meta, nvalid, xs, p["wg"], p["wu"], p["wd"])
    y_a, y_b = _combine(pos0, pos1, ys, h, route, t_a, tl["comb_tm"])
    return y_a.reshape(x_a.shape), y_b.reshape(x_b.shape)


def kernel(x_prompt, x_sample, ln1_g, w_in, conv_w, q_a_norm_g, w_uq, kv_a_norm_g, w_ukv, q_norm_g,
           k_norm_g, w_conv_out, w_attn_out, w_out, ln2_g, w_router_group, b_router_group,
           w_router_exp, b_router_exp, w_gate, w_up, w_down):
    p = _prepare(ln1_g, w_in, conv_w, q_a_norm_g, w_uq, kv_a_norm_g, w_ukv, q_norm_g, k_norm_g,
                 w_conv_out, w_attn_out, w_out, ln2_g, w_router_group, b_router_group,
                 w_router_exp, b_router_exp, w_gate, w_up, w_down)
    return _forward(x_prompt, x_sample, p)
```

```python
import functools
import math

import jax
import jax.numpy as jnp
from jax import lax
from jax.experimental import pallas as pl
from jax.experimental.pallas import tpu as pltpu

F32 = jnp.float32
BF16 = jnp.bfloat16

D_MODEL = 2048
CONV_CH = 1024
N_HEADS = 16
QK_NOPE = 128
QK_ROPE = 64
HALF_ROPE = QK_ROPE // 2
QK_DIM = QK_NOPE + QK_ROPE
QK_PAD = 256
V_DIM = 128
Q_LORA = 512
KV_LORA = 512
ROPE_THETA = 10000.0
N_GROUPS = 8
EXP_PER_GROUP = 8
N_EXPERTS = 64
D_EXPERT = 512
EPS = 1e-6
PROJ_W = 8192
KPE_OFF = 4096
LANES = 128
EXP_LANE0 = N_GROUPS
ROW_BLOCK = 512
VMEM_LIMIT = 56 * 1024 * 1024

_NT = (((1,), (1,)), ((), ()))


def _cparams(sem):
    return pltpu.CompilerParams(dimension_semantics=sem, vmem_limit_bytes=VMEM_LIMIT)


D_HALF = D_MODEL // 2
U32 = jnp.uint32


def _pack_halves(x):
    lo = lax.bitcast_convert_type(x[:, :D_HALF].astype(BF16).astype(F32), U32)
    hi = lax.bitcast_convert_type(x[:, D_HALF:].astype(BF16).astype(F32), U32)
    return (lo >> 16) | hi


def _unpack_halves(w):
    lo = lax.bitcast_convert_type(w << 16, F32)
    hi = lax.bitcast_convert_type(w & jnp.uint32(0xFFFF0000), F32)
    return lo, hi


def _inproj_body(x_ref, g_ref, w_ref, wpe_ref, out_ref, kpe_ref, xn_ref):
    @pl.when(pl.program_id(1) == 0)
    def _():
        x = x_ref[...]
        ms = jnp.mean(x * x, axis=-1, keepdims=True)
        xn = (x * lax.rsqrt(ms + EPS) * g_ref[...]).astype(BF16)
        xn_ref[...] = xn
        kpe_ref[...] = jnp.dot(xn, wpe_ref[...], preferred_element_type=F32)

    out_ref[...] = jnp.dot(xn_ref[...], w_ref[...], preferred_element_type=F32).astype(BF16)


def _in_proj(x2d, ln1_g, w_main, w_pe, tm, tn):
    t = x2d.shape[0]
    return pl.pallas_call(
        _inproj_body,
        grid=(t // tm, PROJ_W // tn),
        in_specs=[
            pl.BlockSpec((tm, D_MODEL), lambda i, j: (i, 0)),
            pl.BlockSpec((1, D_MODEL), lambda i, j: (0, 0)),
            pl.BlockSpec((D_MODEL, tn), lambda i, j: (0, j)),
            pl.BlockSpec((D_MODEL, LANES), lambda i, j: (0, 0)),
        ],
        out_specs=[
            pl.BlockSpec((tm, tn), lambda i, j: (i, j)),
            pl.BlockSpec((tm, LANES), lambda i, j: (i, 0)),
        ],
        out_shape=[
            jax.ShapeDtypeStruct((t, PROJ_W), BF16),
            jax.ShapeDtypeStruct((t, LANES), F32),
        ],
        scratch_shapes=[pltpu.VMEM((tm, D_MODEL), BF16)],
        compiler_params=_cparams(("arbitrary", "arbitrary")),
        name="in_proj",
    )(x2d, ln1_g, w_main, w_pe)


def _qkv_body(ql_ref, kvl_ref, kpe_ref, cosk_ref, sink_ref, cost_ref, sint_ref,
              gqa_ref, gkva_ref, gq_ref, gkn_ref, gkr_ref, wq_ref, wkn_ref, wv_ref,
              qt_ref, k_ref, vt_ref):
    def latent_norm(ref, g_ref):
        v = ref[...].astype(F32)
        ms = jnp.mean(v * v, axis=-1, keepdims=True)
        return (v * lax.rsqrt(ms + EPS) * g_ref[...]).astype(BF16)

    qn = latent_norm(ql_ref, gqa_ref)
    kvn = latent_norm(kvl_ref, gkva_ref)
    tm = qn.shape[0]

    kn = jnp.dot(kvn, wkn_ref[...], preferred_element_type=F32)
    kpe = kpe_ref[...]
    ss_pe = jnp.sum(kpe * kpe, axis=-1, keepdims=True)
    kr = kpe * gkr_ref[...]
    kr = kr * cosk_ref[...] + pltpu.roll(kr, 2 * HALF_ROPE, axis=1) * sink_ref[...]
    gkn = gkn_ref[...]
    for h in range(N_HEADS):
        kh = kn[:, h * QK_NOPE:(h + 1) * QK_NOPE]
        ss = jnp.sum(kh * kh, axis=-1, keepdims=True) + ss_pe
        r = lax.rsqrt(ss * (1.0 / QK_DIM) + EPS)
        k_ref[0, h, :, 0:QK_NOPE] = (kh * gkn * r).astype(BF16)
        k_ref[0, h, :, QK_NOPE:QK_PAD] = (kr * r).astype(BF16)

    vt = lax.dot_general(wv_ref[...], kvn, _NT, preferred_element_type=F32)
    for h in range(N_HEADS):
        vt_ref[0, h] = vt[h * V_DIM:(h + 1) * V_DIM, :].astype(BF16)

    cost = cost_ref[...]
    sint = sint_ref[...]
    gq = gq_ref[...]
    zeros = jnp.zeros((HALF_ROPE, tm), BF16)
    for h in range(N_HEADS):
        qt = lax.dot_general(wq_ref[h * QK_PAD:(h + 1) * QK_PAD, :], qn, _NT,
                             preferred_element_type=F32)
        ss = jnp.sum(qt * qt, axis=0, keepdims=True)
        r = lax.rsqrt(ss * (1.0 / QK_DIM) + EPS)
        qs = qt * gq * r
        x1 = qs[QK_NOPE:QK_NOPE + HALF_ROPE]
        x2 = qs[QK_NOPE + 2 * HALF_ROPE:QK_NOPE + 3 * HALF_ROPE]
        qt_ref[0, h, 0:QK_NOPE, :] = qs[0:QK_NOPE].astype(BF16)
        qt_ref[0, h, QK_NOPE:QK_NOPE + HALF_ROPE, :] = (x1 * cost - x2 * sint).astype(BF16)
        qt_ref[0, h, QK_NOPE + HALF_ROPE:QK_NOPE + 2 * HALF_ROPE, :] = zeros
        qt_ref[0, h, QK_NOPE + 2 * HALF_ROPE:QK_NOPE + 3 * HALF_ROPE, :] = (
            x1 * sint + x2 * cost).astype(BF16)
        qt_ref[0, h, QK_NOPE + 3 * HALF_ROPE:QK_PAD, :] = zeros


def _qkv(proj, kpe, tabs, wts, b, s, tm):
    ns = s // tm
    cos_k, sin_k, cos_t, sin_t = tabs
    gqa, gkva, gq, gkn, gkr, wq_t, wkn, wv_t = wts
    const = lambda shape: pl.BlockSpec(shape, lambda bi, i: (0,) * len(shape))
    return pl.pallas_call(
        _qkv_body,
        grid=(b, ns),
        in_specs=[
            pl.BlockSpec((tm, Q_LORA), lambda bi, i: (bi * ns + i, 3 * CONV_CH // Q_LORA)),
            pl.BlockSpec((tm, KV_LORA), lambda bi, i: (bi * ns + i, 3 * CONV_CH // KV_LORA + 1)),
            pl.BlockSpec((tm, LANES), lambda bi, i: (bi * ns + i, 0)),
            pl.BlockSpec((tm, LANES), lambda bi, i: (i, 0)),
            pl.BlockSpec((tm, LANES), lambda bi, i: (i, 0)),
            pl.BlockSpec((HALF_ROPE, tm), lambda bi, i: (0, i)),
            pl.BlockSpec((HALF_ROPE, tm), lambda bi, i: (0, i)),
            const((1, Q_LORA)), const((1, KV_LORA)), const((QK_PAD, 1)),
            const((1, LANES)), const((1, LANES)),
            const((N_HEADS * QK_PAD, Q_LORA)),
            const((KV_LORA, N_HEADS * QK_NOPE)),
            const((N_HEADS * V_DIM, KV_LORA)),
        ],
        out_specs=[
            pl.BlockSpec((1, N_HEADS, QK_PAD, tm), lambda bi, i: (bi, 0, 0, i)),
            pl.BlockSpec((1, N_HEADS, tm, QK_PAD), lambda bi, i: (bi, 0, i, 0)),
            pl.BlockSpec((1, N_HEADS, V_DIM, tm), lambda bi, i: (bi, 0, 0, i)),
        ],
        out_shape=[
            jax.ShapeDtypeStruct((b, N_HEADS, QK_PAD, s), BF16),
            jax.ShapeDtypeStruct((b, N_HEADS, s, QK_PAD), BF16),
            jax.ShapeDtypeStruct((b, N_HEADS, V_DIM, s), BF16),
        ],
        compiler_params=_cparams(("arbitrary", "arbitrary")),
        name="qkv",
    )(proj, proj, kpe, cos_k, sin_k, cos_t, sin_t, gqa, gkva, gq, gkn, gkr, wq_t, wkn, wv_t)


def _attn_body(qt_ref, k_ref, vt_ref, o_ref, s_a, m_a, s_b, m_b, *, kc):
    n = pl.program_id(0)
    s_len = k_ref.shape[2]
    chunks = [(c * kc, (c + 1) * kc) for c in range(s_len // kc)]

    @pl.when(n == 0)
    def _():
        s_b[...] = jnp.zeros_like(s_b)
        m_b[...] = jnp.zeros_like(m_b)

    def step(s_w, m_w, s_r, m_r):
        m_prev = m_r[...]
        qt = qt_ref[0, 0]
        l = None
        acc = None
        m = None
        for lo, hi in chunks:
            p = jnp.exp2(s_r[lo:hi, :] - m_prev)
            lc = jnp.sum(p, axis=0, keepdims=True)
            l = lc if l is None else l + lc
            pv = jnp.dot(vt_ref[0, 0, :, lo:hi], p.astype(BF16), preferred_element_type=F32)
            acc = pv if acc is None else acc + pv

            sc = jnp.dot(k_ref[0, 0, lo:hi, :], qt, preferred_element_type=F32)
            s_w[lo:hi, :] = sc
            mc = jnp.max(sc, axis=0, keepdims=True)
            m = mc if m is None else jnp.maximum(m, mc)
        o_ref[0] = (acc * (1.0 / l)).T.astype(BF16)
        m_w[...] = m

    @pl.when(n % 2 == 0)
    def _():
        step(s_a, m_a, s_b, m_b)

    @pl.when(n % 2 == 1)
    def _():
        step(s_b, m_b, s_a, m_a)


def _attention(qt, k, vt, tq, kc):
    b, _, _, s = qt.shape
    nq = s // tq
    n_tiles = b * N_HEADS * nq

    def bhi(tile):
        return tile // (N_HEADS * nq), (tile // nq) % N_HEADS, tile % nq

    def score_tile(n):
        return bhi(jnp.minimum(n, n_tiles - 1))

    def value_tile(n):
        return bhi(jnp.maximum(n - 1, 0))

    def qt_map(n):
        bi, h, i = score_tile(n)
        return bi, h, 0, i

    def k_map(n):
        bi, h, _ = score_tile(n)
        return bi, h, 0, 0

    def vt_map(n):
        bi, h, _ = value_tile(n)
        return bi, h, 0, 0

    def o_map(n):
        bi, h, i = value_tile(n)
        return bi, i, h

    return pl.pallas_call(
        functools.partial(_attn_body, kc=kc),
        grid=(n_tiles + 1,),
        in_specs=[
            pl.BlockSpec((1, 1, QK_PAD, tq), qt_map),
            pl.BlockSpec((1, 1, s, QK_PAD), k_map),
            pl.BlockSpec((1, 1, V_DIM, s), vt_map),
        ],
        out_specs=pl.BlockSpec((1, tq, V_DIM), o_map),
        out_shape=jax.ShapeDtypeStruct((b, s, N_HEADS * V_DIM), BF16),
        scratch_shapes=[pltpu.VMEM((s, tq), F32), pltpu.VMEM((1, tq), F32),
                        pltpu.VMEM((s, tq), F32), pltpu.VMEM((1, tq), F32)],
        compiler_params=_cparams(("arbitrary",)),
        name="attention",
    )(qt, k, vt)


def _merge_body(u_ref, gb_ref, gc_ref, up_ref, gcp_ref, un_ref, gcn_ref, o_ref, gtc_ref, gta_ref,
                cw_ref, wc_ref, wa_ref, out_ref):
    i = pl.program_id(1)
    last = pl.num_programs(1) - 1
    halo = up_ref.shape[0]
    v = gc_ref[...].astype(F32) * u_ref[...].astype(F32)
    tm = v.shape[0]
    v_before = gcp_ref[halo - 1:halo, :].astype(F32) * up_ref[halo - 1:halo, :].astype(F32)
    v_after = gcn_ref[0:1, :].astype(F32) * un_ref[0:1, :].astype(F32)
    v_before = jnp.where(i == 0, 0.0, v_before)
    v_after = jnp.where(i == last, 0.0, v_after)
    row = lax.broadcasted_iota(jnp.int32, (tm, 1), 0)
    v_prev = jnp.where(row == 0, v_before, pltpu.roll(v, 1, axis=0))
    v_next = jnp.where(row == tm - 1, v_after, pltpu.roll(v, tm - 1, axis=0))
    cw = cw_ref[...]
    conv = v_prev * cw[0:1, :] + v * cw[1:2, :] + v_next * cw[2:3, :]
    z = (gb_ref[...].astype(F32) * conv).astype(BF16)
    conv_out = jnp.dot(z, wc_ref[...], preferred_element_type=F32)
    attn_out = jnp.dot(o_ref[...], wa_ref[...], preferred_element_type=F32)
    merged = (jax.nn.sigmoid(gtc_ref[...].astype(F32)) * conv_out
              + jax.nn.sigmoid(gta_ref[...].astype(F32)) * attn_out)
    out_ref[...] = merged.astype(BF16)


def _merge(proj, o2d, conv_w, wc, wa, b, s, tm, halo):
    ns = s // tm
    t = b * s
    hb = tm // halo
    nh = t // halo
    row = lambda bi, i: bi * ns + i
    prev = lambda bi, i: (jnp.maximum(row(bi, i) * hb - 1, 0))
    nxt = lambda bi, i: (jnp.minimum((row(bi, i) + 1) * hb, nh - 1))
    gate0 = KPE_OFF // D_MODEL
    return pl.pallas_call(
        _merge_body,
        grid=(b, ns),
        in_specs=[
            pl.BlockSpec((tm, CONV_CH), lambda bi, i: (row(bi, i), 0)),
            pl.BlockSpec((tm, CONV_CH), lambda bi, i: (row(bi, i), 1)),
            pl.BlockSpec((tm, CONV_CH), lambda bi, i: (row(bi, i), 2)),
            pl.BlockSpec((halo, CONV_CH), lambda bi, i: (prev(bi, i), 0)),
            pl.BlockSpec((halo, CONV_CH), lambda bi, i: (prev(bi, i), 2)),
            pl.BlockSpec((halo, CONV_CH), lambda bi, i: (nxt(bi, i), 0)),
            pl.BlockSpec((halo, CONV_CH), lambda bi, i: (nxt(bi, i), 2)),
            pl.BlockSpec((tm, D_MODEL), lambda bi, i: (row(bi, i), 0)),
            pl.BlockSpec((tm, D_MODEL), lambda bi, i: (row(bi, i), gate0)),
            pl.BlockSpec((tm, D_MODEL), lambda bi, i: (row(bi, i), gate0 + 1)),
            pl.BlockSpec((3, CONV_CH), lambda bi, i: (0, 0)),
            pl.BlockSpec((CONV_CH, D_MODEL), lambda bi, i: (0, 0), pipeline_mode=pl.Buffered(1)),
            pl.BlockSpec((D_MODEL, D_MODEL), lambda bi, i: (0, 0), pipeline_mode=pl.Buffered(1)),
        ],
        out_specs=pl.BlockSpec((tm, D_MODEL), lambda bi, i: (row(bi, i), 0)),
        out_shape=jax.ShapeDtypeStruct((t, D_MODEL), BF16),
        compiler_params=_cparams(("arbitrary", "arbitrary")),
        name="merge",
    )(proj, proj, proj, proj, proj, proj, proj, o2d, proj, proj, conv_w, wc, wa)


def _out_route_body(xa_ref, mga_ref, xb_ref, mgb_ref, wo_ref, g2_ref, wr_ref, br_ref,
                    h_ref, xn_ref, route_ref, routet_ref, cnt_ref, base_ref, logit_ref, *, steps_a):
    step = pl.program_id(0)
    tm = h_ref.shape[0]
    slot = step % 2

    @pl.when(step == 0)
    def _():
        base_ref[...] = jnp.zeros_like(base_ref)
        logit_ref[...] = jnp.zeros_like(logit_ref)

    in_a = step < steps_a
    mg = jnp.where(in_a, mga_ref[...], mgb_ref[...])
    n_chunks = 4
    cw = D_MODEL // n_chunks

    def h_chunk(c):
        x = jnp.where(in_a, xa_ref[:, c * cw:(c + 1) * cw], xb_ref[:, c * cw:(c + 1) * cw])
        return x + jnp.dot(mg, wo_ref[:, c * cw:(c + 1) * cw], preferred_element_type=F32)

    logits = logit_ref[1 - slot]
    lane = lax.broadcasted_iota(jnp.int32, (tm, LANES), 1)
    neg = -jnp.inf
    lg = jnp.where(lane < N_GROUPS, logits, neg)
    gmax = jnp.max(lg, axis=-1, keepdims=True)
    g_p = 1.0 / jnp.sum(jnp.exp(lg - gmax), axis=-1, keepdims=True)
    g_sel = jnp.min(jnp.where(lg == gmax, lane, LANES), axis=-1, keepdims=True)
    h_parts = [h_chunk(0)]

    lo_lane = EXP_LANE0 + g_sel * EXP_PER_GROUP
    le = jnp.where((lane >= lo_lane) & (lane < lo_lane + EXP_PER_GROUP), logits, neg)
    m1 = jnp.max(le, axis=-1, keepdims=True)
    i1 = jnp.min(jnp.where(le == m1, lane, LANES), axis=-1, keepdims=True)
    le2 = jnp.where(lane == i1, neg, le)
    m2 = jnp.max(le2, axis=-1, keepdims=True)
    i2 = jnp.min(jnp.where(le2 == m2, lane, LANES), axis=-1, keepdims=True)
    e2 = jnp.exp(m2 - m1)
    gate1 = g_p / (1.0 + e2)
    gate2 = g_p * e2 / (1.0 + e2)
    h_parts.append(h_chunk(1))

    sel1 = lane == i1
    sel2 = lane == i2
    onehot = jnp.where(sel1 | sel2, 1.0, 0.0)
    r_i = lax.broadcasted_iota(jnp.int32, (tm, tm), 0)
    c_i = lax.broadcasted_iota(jnp.int32, (tm, tm), 1)
    lower = jnp.where(r_i > c_i, 1.0, 0.0).astype(BF16)
    before = jnp.dot(lower, onehot.astype(BF16), preferred_element_type=F32) + base_ref[...]
    rank1 = jnp.sum(jnp.where(sel1, before, 0.0), axis=-1, keepdims=True)
    rank2 = jnp.sum(jnp.where(sel2, before, 0.0), axis=-1, keepdims=True)
    counted = jnp.where(step > 0, 1.0, 0.0)
    base_ref[...] = base_ref[...] + counted * jnp.sum(onehot, axis=0, keepdims=True)
    cnt_ref[...] = base_ref[...]
    h_parts.append(h_chunk(2))

    e1f = (i1 - EXP_LANE0).astype(F32)
    e2f = (i2 - EXP_LANE0).astype(F32)
    cols = (e1f, e2f, gate1, gate2, rank1, rank2)
    route = jnp.zeros((tm, LANES), F32)
    for c, val in enumerate(cols):
        route = jnp.where(lane == c, val, route)
    route_ref[...] = route
    routet_ref[...] = route.T[0:8, :]
    h_parts.append(h_chunk(3))

    h = jnp.concatenate(h_parts, axis=1)
    h_ref[...] = h
    ms = jnp.mean(h * h, axis=-1, keepdims=True)
    xn = h * lax.rsqrt(ms + EPS) * g2_ref[...]
    xn_ref[...] = _pack_halves(xn)

    hi = xn.astype(BF16)
    lo = (xn - hi.astype(F32)).astype(BF16)
    w_hl = wr_ref[...]
    both = jnp.dot(hi, w_hl, preferred_element_type=F32)
    logit_ref[slot] = (both[:, :LANES] + both[:, LANES:]
                       + jnp.dot(lo, w_hl[:, :LANES], preferred_element_type=F32)) + br_ref[...]


def _out_route(xa, mga, xb, mgb, wo, ln2_g, wr_hl, br, tm):
    steps_a, steps_b = xa.shape[0] // tm, xb.shape[0] // tm
    t = xa.shape[0] + xb.shape[0]
    const = lambda shape: pl.BlockSpec(shape, lambda i: (0,) * len(shape))
    n_tiles = steps_a + steps_b
    seg_a = pl.BlockSpec((tm, D_MODEL), lambda i: (jnp.minimum(i, steps_a - 1), 0))
    seg_b = pl.BlockSpec((tm, D_MODEL), lambda i: (jnp.clip(i - steps_a, 0, steps_b - 1), 0))
    this_tile = lambda i: jnp.minimum(i, n_tiles - 1)
    prev_tile = lambda i: jnp.maximum(i - 1, 0)
    return pl.pallas_call(
        functools.partial(_out_route_body, steps_a=steps_a),
        grid=(n_tiles + 1,),
        in_specs=[
            seg_a, seg_a, seg_b, seg_b,
            const((D_MODEL, D_MODEL)), const((1, D_MODEL)),
            const((D_MODEL, 2 * LANES)), const((1, LANES)),
        ],
        out_specs=[
            pl.BlockSpec((tm, D_MODEL), lambda i: (this_tile(i), 0)),
            pl.BlockSpec((tm, D_HALF), lambda i: (this_tile(i), 0)),
            pl.BlockSpec((tm, LANES), lambda i: (prev_tile(i), 0)),
            pl.BlockSpec((8, tm), lambda i: (0, prev_tile(i))),
            const((1, LANES)),
        ],
        out_shape=[
            jax.ShapeDtypeStruct((t, D_MODEL), F32),
            jax.ShapeDtypeStruct((t, D_HALF), U32),
            jax.ShapeDtypeStruct((t, LANES), F32),
            jax.ShapeDtypeStruct((8, t), F32),
            jax.ShapeDtypeStruct((1, LANES), F32),
        ],
        scratch_shapes=[pltpu.VMEM((1, LANES), F32), pltpu.VMEM((2, tm, LANES), F32)],
        compiler_params=_cparams(("arbitrary",)),
        name="out_route",
    )(xa, mga, xb, mgb, wo, ln2_g, wr_hl, br)


def _block_meta(next_expert, slot, first):
    return (next_expert + 1) * 4 + slot * 2 + first


def _plan_body(cnt_ref, routet_ref, pos_ref, start_ref, blk_exp_ref, blk_src_ref, blk_meta_ref,
               nvalid_ref, next_ref):
    n_blocks = blk_exp_ref.shape[0]

    def backwards(k, following):
        e = N_EXPERTS - 1 - k
        next_ref[e] = following
        return jnp.where(cnt_ref[EXP_LANE0 + e] > 0, e, following)

    lax.fori_loop(0, N_EXPERTS, backwards, -1)

    def per_expert(e, carry):
        acc, ordinal = carry
        c = cnt_ref[EXP_LANE0 + e]
        nb = (c + ROW_BLOCK - 1) // ROW_BLOCK
        start_ref[e] = acc * ROW_BLOCK

        def fill(j, inner):
            blk_exp_ref[acc + j] = e
            blk_src_ref[acc + j] = acc + j
            blk_meta_ref[acc + j] = _block_meta(next_ref[e], ordinal % 2, jnp.where(j == 0, 1, 0))
            return inner

        lax.fori_loop(0, nb, fill, 0)
        return acc + nb, ordinal + jnp.where(nb > 0, 1, 0)

    nvalid, _ = lax.fori_loop(0, N_EXPERTS, per_expert, (0, 0))
    nvalid_ref[0] = nvalid
    last_exp = blk_exp_ref[nvalid - 1]

    def tail(j, carry):
        blk_exp_ref[j] = last_exp
        blk_src_ref[j] = nvalid - 1
        blk_meta_ref[j] = 0
        return carry

    lax.fori_loop(nvalid, n_blocks, tail, 0)

    ef = routet_ref[0:2, :]
    off = jnp.zeros_like(ef)
    for e in range(N_EXPERTS):
        off = jnp.where(ef == float(e), start_ref[e].astype(F32), off)
    pos_ref[...] = (off + routet_ref[4:6, :]).astype(jnp.int32)


def _plan(counts_i32, routet, n_blocks):
    t = routet.shape[1]
    smem = lambda: pl.BlockSpec(memory_space=pltpu.SMEM)
    return pl.pallas_call(
        _plan_body,
        in_specs=[smem(), pl.BlockSpec(memory_space=pltpu.VMEM)],
        out_specs=[pl.BlockSpec(memory_space=pltpu.VMEM), smem(), smem(), smem(), smem(), smem()],
        out_shape=[
            jax.ShapeDtypeStruct((2, t), jnp.int32),
            jax.ShapeDtypeStruct((N_EXPERTS,), jnp.int32),
            jax.ShapeDtypeStruct((n_blocks,), jnp.int32),
            jax.ShapeDtypeStruct((n_blocks,), jnp.int32),
            jax.ShapeDtypeStruct((n_blocks,), jnp.int32),
            jax.ShapeDtypeStruct((1,), jnp.int32),
        ],
        scratch_shapes=[pltpu.SMEM((N_EXPERTS,), jnp.int32)],
        compiler_params=pltpu.CompilerParams(vmem_limit_bytes=VMEM_LIMIT),
        name="plan",
    )(counts_i32, routet)


def _dispatch_body(pos0_ref, pos1_ref, cnt_ref, start_ref, nvalid_ref, xn_ref, xs_hbm, zero_ref, sem):
    tm = xn_ref.shape[0]
    n_blocks = xs_hbm.shape[0] // ROW_BLOCK

    def put(src_ref, src_row, dst_row):
        return pltpu.make_async_copy(src_ref.at[pl.ds(src_row, 1)], xs_hbm.at[pl.ds(dst_row, 1)], sem)

    def put_block(blk):
        return pltpu.make_async_copy(zero_ref, xs_hbm.at[pl.ds(blk * ROW_BLOCK, ROW_BLOCK)], sem)

    @pl.when(pl.program_id(0) == 0)
    def _():
        zero_ref[...] = jnp.zeros_like(zero_ref)

        def tail(blk, carry):
            put_block(blk).start()
            return carry

        def tail_done(blk, carry):
            put_block(0).wait()
            return carry

        lax.fori_loop(nvalid_ref[0], n_blocks, tail, 0)

        def pad_rows(wait):
            def per_expert(e, carry):
                c = cnt_ref[EXP_LANE0 + e]
                end = (c + ROW_BLOCK - 1) // ROW_BLOCK * ROW_BLOCK
                base = start_ref[e]
                head = (8 - c % 8) % 8

                def fill(r, inner):
                    if wait:
                        put(zero_ref, 0, 0).wait()
                    else:
                        put(zero_ref, 0, base + c + r).start()
                    return inner

                lax.fori_loop(0, jnp.minimum(head, end - c), fill, 0)
                groups = (end - c - head) // 8
                cur = base + c + head
                run = ROW_BLOCK // 16
                while run >= 1:
                    rows = 8 * run

                    @pl.when((groups // run) % 2 == 1)
                    def _(rows=rows, cur=cur):
                        cp = pltpu.make_async_copy(zero_ref.at[pl.ds(0, rows)],
                                                   xs_hbm.at[pl.ds(pl.multiple_of(cur, 8), rows)], sem)
                        if wait:
                            cp.wait()
                        else:
                            cp.start()

                    cur = cur + rows * ((groups // run) % 2)
                    run //= 2
                return carry

            lax.fori_loop(0, N_EXPERTS, per_expert, 0)

        pad_rows(wait=False)
        lax.fori_loop(nvalid_ref[0], n_blocks, tail_done, 0)
        pad_rows(wait=True)

    for t in range(tm):
        put(xn_ref, t, pos0_ref[t]).start(priority=0)
        put(xn_ref, t, pos1_ref[t]).start(priority=1)

    for t in range(2 * tm):
        put(xn_ref, 0, 0).wait()


def _dispatch(pos0, pos1, counts_i32, starts, nvalid, xn2, n_rows, tm):
    t = xn2.shape[0]
    smem = lambda: pl.BlockSpec(memory_space=pltpu.SMEM)
    pos_spec = lambda: pl.BlockSpec((tm,), lambda i: (i,), memory_space=pltpu.SMEM)
    return pl.pallas_call(
        _dispatch_body,
        grid=(t // tm,),
        in_specs=[
            pos_spec(), pos_spec(),
            smem(), smem(), smem(),
            pl.BlockSpec((tm, D_HALF), lambda i: (i, 0)),
        ],
        out_specs=pl.BlockSpec(memory_space=pl.ANY),
        out_shape=jax.ShapeDtypeStruct((n_rows, D_HALF), U32),
        scratch_shapes=[pltpu.VMEM((ROW_BLOCK, D_HALF), U32), pltpu.SemaphoreType.DMA(())],
        compiler_params=pltpu.CompilerParams(dimension_semantics=("arbitrary",),
                                             vmem_limit_bytes=VMEM_LIMIT, has_side_effects=True),
        name="dispatch",
    )(pos0, pos1, counts_i32, starts, nvalid, xn2)


def _expert_body(blk_exp_ref, blk_src_ref, blk_meta_ref, nvalid_ref, xs_ref, wg_hbm, wu_hbm, wd_hbm,
                 ys_ref, wg_buf, wu_buf, wd_buf, sems):
    i = pl.program_id(0)
    valid = i < nvalid_ref[0]
    meta = blk_meta_ref[i]
    first = meta % 2
    slot = (meta // 2) % 2
    next_expert = meta // 4 - 1

    def weight_copies(expert, to_slot):
        return [pltpu.make_async_copy(hbm.at[expert], buf.at[to_slot], sems.at[to_slot])
                for hbm, buf in ((wg_hbm, wg_buf), (wu_hbm, wu_buf), (wd_hbm, wd_buf))]

    @pl.when(i == 0)
    def _():
        for cp in weight_copies(blk_exp_ref[0], 0):
            cp.start()

    @pl.when(first == 1)
    def _():
        for cp in weight_copies(blk_exp_ref[i], slot):
            cp.wait()

        @pl.when(next_expert >= 0)
        def _():
            for cp in weight_copies(next_expert, 1 - slot):
                cp.start(priority=1)

    @pl.when(valid)
    def _():
        lo, hi = _unpack_halves(xs_ref[...])
        xb = jnp.concatenate([lo.astype(BF16), hi.astype(BF16)], axis=1)
        g = jnp.dot(xb, wg_buf[slot].astype(BF16), preferred_element_type=F32)
        u = jnp.dot(xb, wu_buf[slot].astype(BF16), preferred_element_type=F32)
        hmid = (g * jax.nn.sigmoid(g) * u).astype(BF16)
        ys_ref[...] = _pack_halves(jnp.dot(hmid, wd_buf[slot].astype(BF16), preferred_element_type=F32))

    @pl.when(jnp.logical_not(valid))
    def _():
        ys_ref[...] = jnp.zeros_like(ys_ref)


def _experts(blk_exp, blk_src, blk_meta, nvalid, xs, wg, wu, wd):
    n_rows = xs.shape[0]
    n_blocks = n_rows // ROW_BLOCK
    hbm = lambda: pl.BlockSpec(memory_space=pl.ANY)
    grid_spec = pltpu.PrefetchScalarGridSpec(
        num_scalar_prefetch=4,
        grid=(n_blocks,),
        in_specs=[
            pl.BlockSpec((ROW_BLOCK, D_HALF), lambda i, be, bs, bm, nv: (bs[i], 0)),
            hbm(), hbm(), hbm(),
        ],
        out_specs=pl.BlockSpec((ROW_BLOCK, D_HALF), lambda i, be, bs, bm, nv: (i, 0)),
        scratch_shapes=[
            pltpu.VMEM((2, D_MODEL, D_EXPERT), F32),
            pltpu.VMEM((2, D_MODEL, D_EXPERT), F32),
            pltpu.VMEM((2, D_EXPERT, D_MODEL), F32),
            pltpu.SemaphoreType.DMA((2,)),
        ],
    )
    return pl.pallas_call(
        _expert_body,
        grid_spec=grid_spec,
        out_shape=jax.ShapeDtypeStruct((n_rows, D_HALF), U32),
        compiler_params=_cparams(("arbitrary",)),
        name="experts",
    )(blk_exp, blk_src, blk_meta, nvalid, xs, wg, wu, wd)


def _combine_body(pos0_ref, pos1_ref, pos0_next_ref, pos1_next_ref, ys_hbm, h_ref, route_ref,
                  ya_ref, yb_ref, buf_ref, sems, *, steps_a):
    n = pl.program_id(0)
    tm = h_ref.shape[0]
    slot = n % 2

    def row_copy(row, k, t, to_slot):
        return pltpu.make_async_copy(ys_hbm.at[pl.ds(row, 1)],
                                     buf_ref.at[to_slot, k, pl.ds(t, 1)], sems.at[to_slot])

    def gather(p0_ref, p1_ref, to_slot):
        for t in range(tm):
            row_copy(p0_ref[t], 0, t, to_slot).start(priority=0)
            row_copy(p1_ref[t], 1, t, to_slot).start(priority=1)

    @pl.when(n == 0)
    def _():
        gather(pos0_ref, pos1_ref, 0)

    @pl.when(n + 1 < pl.num_programs(0))
    def _():
        gather(pos0_next_ref, pos1_next_ref, 1 - slot)

    for t in range(tm):
        row_copy(0, 0, t, slot).wait()
        row_copy(0, 1, t, slot).wait()
    route = route_ref[...]
    lo1, hi1 = _unpack_halves(buf_ref[slot, 0])
    lo2, hi2 = _unpack_halves(buf_ref[slot, 1])
    g1, g2 = route[:, 2:3], route[:, 3:4]
    y_lo = h_ref[:, :D_HALF] + g1 * lo1 + g2 * lo2
    y_hi = h_ref[:, D_HALF:] + g1 * hi1 + g2 * hi2

    @pl.when(n < steps_a)
    def _():
        ya_ref[:, :D_HALF] = y_lo
        ya_ref[:, D_HALF:] = y_hi

    @pl.when(n >= steps_a)
    def _():
        yb_ref[:, :D_HALF] = y_lo
        yb_ref[:, D_HALF:] = y_hi


def _combine(pos0, pos1, ys, h, route, t_a, tm):
    t = h.shape[0]
    nt = t // tm
    steps_a = t_a // tm
    cur = lambda: pl.BlockSpec((tm,), lambda i: (i,), memory_space=pltpu.SMEM)
    nxt = lambda: pl.BlockSpec((tm,), lambda i: (jnp.minimum(i + 1, nt - 1),), memory_space=pltpu.SMEM)
    return pl.pallas_call(
        functools.partial(_combine_body, steps_a=steps_a),
        grid=(nt,),
        in_specs=[
            cur(), cur(), nxt(), nxt(),
            pl.BlockSpec(memory_space=pl.ANY),
            pl.BlockSpec((tm, D_MODEL), lambda i: (i, 0)),
            pl.BlockSpec((tm, LANES), lambda i: (i, 0)),
        ],
        out_specs=[
            pl.BlockSpec((tm, D_MODEL), lambda i: (jnp.minimum(i, steps_a - 1), 0)),
            pl.BlockSpec((tm, D_MODEL), lambda i: (jnp.maximum(i - steps_a, 0), 0)),
        ],
        out_shape=[
            jax.ShapeDtypeStruct((t_a, D_MODEL), F32),
            jax.ShapeDtypeStruct((t - t_a, D_MODEL), F32),
        ],
        scratch_shapes=[pltpu.VMEM((2, 2, tm, D_HALF), U32), pltpu.SemaphoreType.DMA((2,))],
        compiler_params=_cparams(("arbitrary",)),
        name="combine",
    )(pos0, pos1, pos0, pos1, ys, h, route)


def _pad_rope(a, axis):
    x1, x2 = jnp.split(a, 2, axis=axis)
    z = jnp.zeros_like(x1)
    return jnp.concatenate([x1, z, x2, z], axis=axis)


def _prepare(ln1_g, w_in, conv_w, q_a_norm_g, w_uq, kv_a_norm_g, w_ukv, q_norm_g, k_norm_g,
             w_conv_out, w_attn_out, w_out, ln2_g, w_router_group, b_router_group,
             w_router_exp, b_router_exp, w_gate, w_up, w_down):
    w_in0 = w_in[0]
    w_main = jnp.concatenate([w_in0[:, :KPE_OFF], w_in0[:, KPE_OFF + QK_ROPE:]], axis=1).astype(BF16)
    w_pe = _pad_rope(w_in0[:, KPE_OFF:KPE_OFF + QK_ROPE], 1).astype(BF16)

    wq = w_uq[0].reshape(Q_LORA, N_HEADS, QK_DIM)
    wq = jnp.concatenate([wq[:, :, :QK_NOPE], _pad_rope(wq[:, :, QK_NOPE:], 2)], axis=2)
    wq_t = wq.reshape(Q_LORA, N_HEADS * QK_PAD).T.astype(BF16)
    wkv = w_ukv[0].reshape(KV_LORA, N_HEADS, QK_NOPE + V_DIM)
    wkn = wkv[:, :, :QK_NOPE].reshape(KV_LORA, N_HEADS * QK_NOPE).astype(BF16)
    wv_t = wkv[:, :, QK_NOPE:].reshape(KV_LORA, N_HEADS * V_DIM).T.astype(BF16)

    qg = q_norm_g[0]
    score_scale = QK_DIM ** -0.5 * math.log2(math.e)
    gq = (jnp.concatenate([qg[:QK_NOPE], _pad_rope(qg[QK_NOPE:], 0)]) * score_scale).reshape(QK_PAD, 1)
    kg = k_norm_g[0]
    gkn = kg[:QK_NOPE].reshape(1, LANES)
    gkr = _pad_rope(kg[QK_NOPE:], 0).reshape(1, LANES)

    wr = jnp.concatenate([w_router_group[0], w_router_exp[0],
                          jnp.zeros((D_MODEL, LANES - N_GROUPS - N_EXPERTS), F32)], axis=1)
    wr_hi = wr.astype(BF16)
    wr_hl = jnp.concatenate([wr_hi, (wr - wr_hi.astype(F32)).astype(BF16)], axis=1)
    br = jnp.concatenate([b_router_group[0], b_router_exp[0],
                          jnp.zeros((LANES - N_GROUPS - N_EXPERTS,), F32)]).reshape(1, LANES)
    return dict(
        ln1_g=ln1_g, w_main=w_main, w_pe=w_pe, conv_w=conv_w[0],
        qkv=(q_a_norm_g, kv_a_norm_g, gq, gkn, gkr, wq_t, wkn, wv_t),
        wc=w_conv_out[0].astype(BF16), wa=w_attn_out[0].astype(BF16), wo=w_out[0].astype(BF16),
        ln2_g=ln2_g, wr_hl=wr_hl, br=br,
        wg=w_gate[0], wu=w_up[0], wd=w_down[0],
    )


def _rope_tables(s):
    inv = ROPE_THETA ** (-jnp.arange(0, QK_ROPE, 2, dtype=F32) / QK_ROPE)
    ang = jnp.arange(s, dtype=F32)[:, None] * inv[None, :]
    cos, sin = jnp.cos(ang), jnp.sin(ang)
    z = jnp.zeros_like(cos)
    cos_k = jnp.concatenate([cos, z, cos, z], axis=1)
    sin_k = jnp.concatenate([-sin, z, sin, z], axis=1)
    return cos_k, sin_k, cos.T, sin.T


def _tiles(b, s):
    t = b * s
    pick = lambda n, pref: pref if n % pref == 0 else n
    return dict(
        in_tm=pick(t, 1024), in_tn=2048,
        qkv_tm=pick(s, 512),
        attn_tq=pick(s, 1024), attn_kc=pick(s, 256),
        merge_tm=pick(s, 512), halo=16,
    )


def _moe_tiles(t_a, t_b):
    both = math.gcd(t_a, t_b)
    pick = lambda pref: pref if both % pref == 0 else both
    return dict(route_tm=pick(512), disp_tm=pick(2048), comb_tm=pick(256))


def _mixer(x, p):
    b, s, _ = x.shape
    t = b * s
    tl = _tiles(b, s)
    x2d = x.reshape(t, D_MODEL)
    proj, kpe = _in_proj(x2d, p["ln1_g"], p["w_main"], p["w_pe"], tl["in_tm"], tl["in_tn"])
    qt, k, vt = _qkv(proj, kpe, _rope_tables(s), p["qkv"], b, s, tl["qkv_tm"])
    o = _attention(qt, k, vt, tl["attn_tq"], tl["attn_kc"])
    merged = _merge(proj, o.reshape(t, D_MODEL), p["conv_w"], p["wc"], p["wa"], b, s,
                    tl["merge_tm"], tl["halo"])
    return x2d, merged


def _forward(x_a, x_b, p):
    xa, mga = _mixer(x_a, p)
    xb, mgb = _mixer(x_b, p)
    t_a, t_b = xa.shape[0], xb.shape[0]
    t = t_a + t_b
    tl = _moe_tiles(t_a, t_b)
    h, xn2, route, route_t, counts = _out_route(xa, mga, xb, mgb, p["wo"], p["ln2_g"], p["wr_hl"],
                                                p["br"], tl["route_tm"])
    n_blocks = 2 * t // ROW_BLOCK + N_EXPERTS
    counts_i32 = counts.reshape(LANES).astype(jnp.int32)
    pos_t, starts, blk_exp, blk_src, blk_meta, nvalid = _plan(counts_i32, route_t, n_blocks)
    pos0, pos1 = pos_t[0], pos_t[1]
    xs = _dispatch(pos0, pos1, counts_i32, starts, nvalid, xn2, n_blocks * ROW_BLOCK, tl["disp_tm"])
    ys = _experts(blk_exp, blk_src, blk_meta, nvalid, xs, p["wg"], p["wu"], p["wd"])
    y_a, y_b = _combine(pos0, pos1, ys, h, route, t_a, tl["comb_tm"])
    return y_a.reshape(x_a.shape), y_b.reshape(x_b.shape)


def kernel(x_prompt, x_sample, ln1_g, w_in, conv_w, q_a_norm_g, w_uq, kv_a_norm_g, w_ukv, q_norm_g,
           k_norm_g, w_conv_out, w_attn_out, w_out, ln2_g, w_router_group, b_router_group,
           w_router_exp, b_router_exp, w_gate, w_up, w_down):
    p = _prepare(ln1_g, w_in, conv_w, q_a_norm_g, w_uq, kv_a_norm_g, w_ukv, q_norm_g, k_norm_g,
                 w_conv_out, w_attn_out, w_out, ln2_g, w_router_group, b_router_group,
                 w_router_exp, b_router_exp, w_gate, w_up, w_down)
    return _forward(x_prompt, x_sample, p)
```
